```python
import jax, jax.numpy as jnp
from jax import lax
import numpy as np

D_MODEL = 1024
BATCH = 8
SEQ = 2048
DEPTH = 2
DEC_BATCH = 128
DEC_SEQ = 8
PAST_LEN = 8192
PAGE_SIZE = 128

N_MIXERS = 2
N_ATTN_LAYERS = (DEPTH + N_MIXERS - 1) // N_MIXERS
N_REC_LAYERS = DEPTH // N_MIXERS
N_HEADS = 16
QK_NOPE = 64
QK_ROPE = 32
V_HEAD = 64
Q_LORA = 512
KV_LORA = 256
ROPE_THETA = 10000.0
SOFTMAX_SCALE = (QK_NOPE + QK_ROPE) ** -0.5
Q_BLOCK = 128
LRU_WIDTH = D_MODEL
N_LRU_BLOCKS = 8
LRU_BLOCK = LRU_WIDTH // N_LRU_BLOCKS
LRU_C = 8.0
REC_CONV_W = 4
D_FF = 3 * D_MODEL
FFN_CONV_W = 3
EPS = 1e-6

kernel_name = "hybrid_mla_rglru_convffn_step"


def rmsnorm(x, g):
    xf = x.astype(jnp.float32)
    y = xf * lax.rsqrt(jnp.mean(xf * xf, axis=-1, keepdims=True) + EPS)
    return (y * g.astype(jnp.float32)).astype(x.dtype)


def rope_cos_sin(pos):
    inv = ROPE_THETA ** (-jnp.arange(0, QK_ROPE, 2, dtype=jnp.float32) / QK_ROPE)
    ang = pos.astype(jnp.float32)[:, None] * inv[None, :]
    return jnp.cos(ang), jnp.sin(ang)


def apply_rope(x, cos, sin):
    half = x.shape[-1] // 2
    x1 = x[..., :half].astype(jnp.float32)
    x2 = x[..., half:].astype(jnp.float32)
    return jnp.concatenate([x1 * cos - x2 * sin, x1 * sin + x2 * cos], axis=-1).astype(x.dtype)


def causal_dwconv(x, buf, w, b):
    t = x.shape[1]
    width = w.shape[0]
    xp = jnp.concatenate([buf.astype(x.dtype), x], axis=1)
    y = xp[:, 0:t] * w[0] + b
    for k in range(1, width):
        y = y + xp[:, k:k + t] * w[k]
    return y, xp[:, t:]


def block_diag(x, w, b):
    xb = x.reshape(x.shape[:-1] + (N_LRU_BLOCKS, LRU_BLOCK))
    y = jnp.einsum('btnc,ncd->btnd', xb, w) + b
    return y.reshape(x.shape)


def rglru(x, h0, w_a, b_a, w_i, b_i, lam):
    r = jax.nn.sigmoid(block_diag(x, w_a, b_a).astype(jnp.float32))
    gi = jax.nn.sigmoid(block_diag(x, w_i, b_i).astype(jnp.float32))
    log_a = -LRU_C * r * jax.nn.softplus(-lam.astype(jnp.float32))
    a = jnp.exp(log_a)
    bx = jnp.sqrt(-jnp.expm1(2.0 * log_a)) * (gi * x.astype(jnp.float32))

    def step(h, inp):
        a_t, b_t = inp
        h = a_t * h + b_t
        return h, h

    h_last, hs = lax.scan(step, h0.astype(jnp.float32), (jnp.moveaxis(a, 1, 0), jnp.moveaxis(bx, 1, 0)))
    return jnp.moveaxis(hs, 0, 1).astype(x.dtype), h_last.astype(x.dtype)


def recurrent_mixer(xn, h0, conv_buf, w_in, conv_w, conv_b, w_a, b_a, w_i, b_i, lam, w_out):
    z = xn @ w_in
    xr, gate_in = z[..., :LRU_WIDTH], z[..., LRU_WIDTH:]
    gate = jax.nn.gelu(gate_in, approximate=True)
    xc, new_buf = causal_dwconv(xr, conv_buf, conv_w, conv_b)
    y, h_last = rglru(xc, h0, w_a, b_a, w_i, b_i, lam)
    return (y * gate) @ w_out, h_last, new_buf


def conv_ffn(xn, buf, w_up, conv_w, conv_b, w_down):
    gu = xn @ w_up
    g, u = gu[..., :D_FF], gu[..., D_FF:]
    gc, new_buf = causal_dwconv(g, buf, conv_w, conv_b)
    return (jax.nn.silu(gc) * u) @ w_down, new_buf


def mla_project(xn, pos, w_down, g_q_lora, w_uq, g_kv_lora, g_qn, g_qr, g_kr):
    b, t = xn.shape[:2]
    d = xn @ w_down
    cq = rmsnorm(d[..., :Q_LORA], g_q_lora)
    c_kv = rmsnorm(d[..., Q_LORA:Q_LORA + KV_LORA], g_kv_lora)
    kr = d[..., Q_LORA + KV_LORA:]
    q = (cq @ w_uq).reshape(b, t, N_HEADS, QK_NOPE + QK_ROPE)
    cos, sin = rope_cos_sin(pos)
    q_nope = rmsnorm(q[..., :QK_NOPE], g_qn)
    q_rope = apply_rope(rmsnorm(q[..., QK_NOPE:], g_qr), cos[:, None], sin[:, None])
    k_rope = apply_rope(rmsnorm(kr, g_kr), cos, sin)
    return q_nope, q_rope, c_kv, k_rope


def mla_prompt(q_nope, q_rope, c_kv, k_rope, w_uk, w_uv, g_kn):
    b, t = c_kv.shape[:2]
    k_nope = rmsnorm(jnp.einsum('btc,chd->bthd', c_kv, w_uk), g_kn)
    v = jnp.einsum('btc,chd->bthd', c_kv, w_uv)
    key_pos = jnp.arange(t)

    def attend_block(i):
        start = i * Q_BLOCK
        qn = lax.dynamic_slice_in_dim(q_nope, start, Q_BLOCK, axis=1)
        qr = lax.dynamic_slice_in_dim(q_rope, start, Q_BLOCK, axis=1)
        s = jnp.einsum('bqhd,bkhd->bhqk', qn, k_nope) + jnp.einsum('bqhr,bkr->bhqk', qr, k_rope)
        q_pos = start + jnp.arange(Q_BLOCK)
        mask = key_pos[None, :] <= q_pos[:, None]
        p = jax.nn.softmax(jnp.where(mask, s.astype(jnp.float32) * SOFTMAX_SCALE, -jnp.inf), axis=-1)
        return jnp.einsum('bhqk,bkhd->bqhd', p.astype(v.dtype), v)

    o = lax.map(attend_block, jnp.arange(t // Q_BLOCK))
    return jnp.moveaxis(o, 0, 1).reshape(b, t, N_HEADS * V_HEAD)


def mla_sample(q_nope, q_rope, c_kv, k_rope, page_table, cache_ckv, cache_krope, layer, w_uk, w_uv, g_kn):
    s_new = c_kv.shape[1]
    past = page_table.shape[1] * PAGE_SIZE
    key_pos = jnp.arange(past + s_new)
    q_pos = past + jnp.arange(s_new)
    mask = key_pos[None, :] <= q_pos[:, None]

    def attend_seq(args):
        pt, qn, qr, cn, krn = args
        c_all = jnp.concatenate([cache_ckv[layer, pt].reshape(past, KV_LORA).astype(cn.dtype), cn], axis=0)
        kr_all = jnp.concatenate([cache_krope[layer, pt].reshape(past, QK_ROPE).astype(krn.dtype), krn], axis=0)
        k_nope = rmsnorm(jnp.einsum('lc,chd->lhd', c_all, w_uk), g_kn)
        s = jnp.einsum('qhd,lhd->hql', qn, k_nope) + jnp.einsum('qhr,lr->hql', qr, kr_all)
        p = jax.nn.softmax(jnp.where(mask[None], s.astype(jnp.float32) * SOFTMAX_SCALE, -jnp.inf), axis=-1)
        ctx = jnp.einsum('hql,lc->qhc', p.astype(c_all.dtype), c_all)
        return jnp.einsum('qhc,chd->qhd', ctx, w_uv).reshape(s_new, N_HEADS * V_HEAD)

    return lax.map(attend_seq, (page_table, q_nope, q_rope, c_kv, k_rope))


def setup_inputs(seed: int = 0) -> dict:
    key = jax.random.key(seed)
    ks = iter(jax.random.split(key, 48))

    def nrm(shape, scale):
        return jax.random.normal(next(ks), shape, jnp.float32) * scale

    def gain(shape):
        return 1.0 + nrm(shape, 0.02)

    n_pages = PAST_LEN // PAGE_SIZE
    n_used = DEC_BATCH * n_pages
    n_pool = n_used + max(1, n_used // 4)
    page_table = jax.random.permutation(next(ks), n_pool)[:n_used].astype(jnp.int32).reshape(DEC_BATCH, n_pages)

    u = jax.random.uniform(next(ks), (N_REC_LAYERS, LRU_WIDTH), jnp.float32, 0.9, 0.999)
    s_lam = u ** (1.0 / LRU_C)
    rec_lambda = jnp.log(s_lam / (1.0 - s_lam))

    return {
        "x_prompt": nrm((BATCH, SEQ, D_MODEL), 1.0),
        "x_sample": nrm((DEC_BATCH, DEC_SEQ, D_MODEL), 1.0),
        "cache_ckv": nrm((N_ATTN_LAYERS, n_pool, PAGE_SIZE, KV_LORA), 1.0),
        "cache_krope": nrm((N_ATTN_LAYERS, n_pool, PAGE_SIZE, QK_ROPE), 1.0),
        "page_table": page_table,
        "state_lru_h": nrm((N_REC_LAYERS, DEC_BATCH, LRU_WIDTH), 0.5),
        "state_lru_conv": nrm((N_REC_LAYERS, DEC_BATCH, REC_CONV_W - 1, LRU_WIDTH), 1.0),
        "state_ffn_conv": nrm((DEPTH, DEC_BATCH, FFN_CONV_W - 1, D_FF), 1.0),
        "norm_mix": gain((DEPTH, D_MODEL)),
        "norm_ffn": gain((DEPTH, D_MODEL)),
        "attn_w_down": nrm((N_ATTN_LAYERS, D_MODEL, Q_LORA + KV_LORA + QK_ROPE), D_MODEL ** -0.5),
        "attn_g_q_lora": gain((N_ATTN_LAYERS, Q_LORA)),
        "attn_w_uq": nrm((N_ATTN_LAYERS, Q_LORA, N_HEADS * (QK_NOPE + QK_ROPE)), Q_LORA ** -0.5),
        "attn_g_kv_lora": gain((N_ATTN_LAYERS, KV_LORA)),
        "attn_g_qn": gain((N_ATTN_LAYERS, QK_NOPE)),
        "attn_g_qr": gain((N_ATTN_LAYERS, QK_ROPE)),
        "attn_g_kn": gain((N_ATTN_LAYERS, QK_NOPE)),
        "attn_g_kr": gain((N_ATTN_LAYERS, QK_ROPE)),
        "attn_w_uk": nrm((N_ATTN_LAYERS, KV_LORA, N_HEADS, QK_NOPE), KV_LORA ** -0.5),
        "attn_w_uv": nrm((N_ATTN_LAYERS, KV_LORA, N_HEADS, V_HEAD), KV_LORA ** -0.5),
        "attn_w_o": nrm((N_ATTN_LAYERS, N_HEADS * V_HEAD, D_MODEL), (N_HEADS * V_HEAD) ** -0.5),
        "rec_w_in": nrm((N_REC_LAYERS, D_MODEL, 2 * LRU_WIDTH), D_MODEL ** -0.5),
        "rec_conv_w": nrm((N_REC_LAYERS, REC_CONV_W, LRU_WIDTH), REC_CONV_W ** -0.5),
        "rec_conv_b": nrm((N_REC_LAYERS, LRU_WIDTH), 0.01),
        "rec_w_a": nrm((N_REC_LAYERS, N_LRU_BLOCKS, LRU_BLOCK, LRU_BLOCK), LRU_BLOCK ** -0.5),
        "rec_b_a": nrm((N_REC_LAYERS, N_LRU_BLOCKS, LRU_BLOCK), 0.01),
        "rec_w_i": nrm((N_REC_LAYERS, N_LRU_BLOCKS, LRU_BLOCK, LRU_BLOCK), LRU_BLOCK ** -0.5),
        "rec_b_i": nrm((N_REC_LAYERS, N_LRU_BLOCKS, LRU_BLOCK), 0.01),
        "rec_lambda": rec_lambda,
        "rec_w_out": nrm((N_REC_LAYERS, LRU_WIDTH, D_MODEL), LRU_WIDTH ** -0.5),
        "ffn_w_up": nrm((DEPTH, D_MODEL, 2 * D_FF), D_MODEL ** -0.5),
        "ffn_conv_w": nrm((DEPTH, FFN_CONV_W, D_FF), FFN_CONV_W ** -0.5),
        "ffn_conv_b": nrm((DEPTH, D_FF), 0.01),
        "ffn_w_down": nrm((DEPTH, D_FF, D_MODEL), D_FF ** -0.5),
    }


def reference(x_prompt, x_sample, cache_ckv, cache_krope, page_table, state_lru_h, state_lru_conv, state_ffn_conv,
              norm_mix, norm_ffn, attn_w_down, attn_g_q_lora, attn_w_uq, attn_g_kv_lora, attn_g_qn, attn_g_qr,
              attn_g_kn, attn_g_kr, attn_w_uk, attn_w_uv, attn_w_o, rec_w_in, rec_conv_w, rec_conv_b, rec_w_a,
              rec_b_a, rec_w_i, rec_b_i, rec_lambda, rec_w_out, ffn_w_up, ffn_conv_w, ffn_conv_b, ffn_w_down):
    bp, tp = x_prompt.shape[:2]
    bs, ts = x_sample.shape[:2]
    past = page_table.shape[1] * PAGE_SIZE
    pos_p = jnp.arange(tp)
    pos_s = past + jnp.arange(ts)
    xp, xs = x_prompt, x_sample
    ckv_p, kr_p, ckv_s, kr_s = [], [], [], []
    lh_p, lh_s, lc_p, lc_s = [], [], [], []
    fc_p, fc_s = [], []
    for i in range(DEPTH):
        j = i // N_MIXERS
        hp = rmsnorm(xp, norm_mix[i])
        hs = rmsnorm(xs, norm_mix[i])
        if i % N_MIXERS == 0:
            proj = (attn_w_down[j], attn_g_q_lora[j], attn_w_uq[j], attn_g_kv_lora[j], attn_g_qn[j], attn_g_qr[j], attn_g_kr[j])
            qn_p, qr_p, c_p, k_p = mla_project(hp, pos_p, *proj)
            qn_s, qr_s, c_s, k_s = mla_project(hs, pos_s, *proj)
            o_p = mla_prompt(qn_p, qr_p, c_p, k_p, attn_w_uk[j], attn_w_uv[j], attn_g_kn[j])
            o_s = mla_sample(qn_s, qr_s, c_s, k_s, page_table, cache_ckv, cache_krope, j,
                             attn_w_uk[j], attn_w_uv[j], attn_g_kn[j])
            xp = xp + o_p @ attn_w_o[j]
            xs = xs + o_s @ attn_w_o[j]
            ckv_p.append(c_p)
            kr_p.append(k_p)
            ckv_s.append(c_s)
            kr_s.append(k_s)
        else:
            rec = (rec_w_in[j], rec_conv_w[j], rec_conv_b[j], rec_w_a[j], rec_b_a[j], rec_w_i[j], rec_b_i[j],
                   rec_lambda[j], rec_w_out[j])
            h0_p = jnp.zeros((bp, LRU_WIDTH), xp.dtype)
            buf0_p = jnp.zeros((bp, REC_CONV_W - 1, LRU_WIDTH), xp.dtype)
            o_p, h_p, cv_p = recurrent_mixer(hp, h0_p, buf0_p, *rec)
            o_s, h_s, cv_s = recurrent_mixer(hs, state_lru_h[j], state_lru_conv[j], *rec)
            xp = xp + o_p
            xs = xs + o_s
            lh_p.append(h_p)
            lh_s.append(h_s)
            lc_p.append(cv_p)
            lc_s.append(cv_s)
        gp = rmsnorm(xp, norm_ffn[i])
        gs = rmsnorm(xs, norm_ffn[i])
        fbuf0_p = jnp.zeros((bp, FFN_CONV_W - 1, D_FF), xp.dtype)
        f_p, fb_p = conv_ffn(gp, fbuf0_p, ffn_w_up[i], ffn_conv_w[i], ffn_conv_b[i], ffn_w_down[i])
        f_s, fb_s = conv_ffn(gs, state_ffn_conv[i], ffn_w_up[i], ffn_conv_w[i], ffn_conv_b[i], ffn_w_down[i])
        xp = xp + f_p
        xs = xs + f_s
        fc_p.append(fb_p)
        fc_s.append(fb_s)
    return (xp, xs, jnp.stack(ckv_p), jnp.stack(kr_p), jnp.stack(ckv_s), jnp.stack(kr_s),
            jnp.stack(lh_p), jnp.stack(lh_s), jnp.stack(lc_p), jnp.stack(lc_s), jnp.stack(fc_p), jnp.stack(fc_s))
```

```python
import functools

import jax
import jax.numpy as jnp
from jax import lax
from jax.experimental import pallas as pl
from jax.experimental.pallas import tpu as pltpu

EPS = 1e-6
ROPE_THETA = 10000.0
LRU_C = 8.0

V7X_LANES = 128
V7X_SUBLANES = 8
V7X_VMEM_BYTES = 64 * 1024 * 1024
VMEM_LIMIT = V7X_VMEM_BYTES * 7 // 8

F32 = jnp.float32
BF16 = jnp.bfloat16


def _params(*sem):
    return pltpu.CompilerParams(dimension_semantics=sem, vmem_limit_bytes=VMEM_LIMIT)


def _const_spec(a):
    nd = a.ndim
    return pl.BlockSpec(a.shape, lambda *_: (0,) * nd)


def _rms(x, g):
    ms = jnp.mean(x * x, axis=-1, keepdims=True)
    return x * lax.rsqrt(ms + EPS) * g


def _sigmoid(x):
    return 1.0 / (1.0 + jnp.exp(-x))


def _rope(y, cos, sin_lo, sin_hi, half):
    return y * cos + pltpu.roll(y, V7X_LANES - half, axis=1) * sin_lo + pltpu.roll(y, half, axis=1) * sin_hi


def _tile_rows(tab_ref, bb):
    t = tab_ref[...]
    if bb == 1:
        return t
    return jnp.broadcast_to(t[None], (bb,) + t.shape).reshape(bb * t.shape[0], t.shape[1])


def _mla_proj_kernel(x_ref, gmix_ref, wd_ref, gql_ref, gkv_ref, gkr_ref, wuq_ref, gq_ref, wuk_ref, gk_ref, wuv_ref,
                     cos_ref, sinlo_ref, sinhi_ref, *out_refs, n_heads, q_lora, kv_lora, nope, rope, scale, with_kv):
    if with_kv:
        q_ref, ckv_ref, kr_ref, k_ref, v_ref = out_refs
    else:
        q_ref, ckv_ref, kr_ref = out_refs
    bb, tt, d = x_ref.shape
    rows = bb * tt
    half = rope // 2
    xn = _rms(x_ref[...].reshape(rows, d), gmix_ref[...])
    dd = jnp.dot(xn.astype(BF16), wd_ref[...], preferred_element_type=F32)
    cq = _rms(dd[:, :q_lora], gql_ref[...])
    ckv = _rms(dd[:, q_lora:q_lora + kv_lora], gkv_ref[...])
    kr = dd[:, q_lora + kv_lora:]
    cos = _tile_rows(cos_ref, bb)
    sin_lo = _tile_rows(sinlo_ref, bb)
    sin_hi = _tile_rows(sinhi_ref, bb)
    lane = lax.broadcasted_iota(jnp.int32, (1, V7X_LANES), 1)
    is_rope = lane < rope

    kr_ss = jnp.sum(kr * kr, axis=-1, keepdims=True)
    krr = _rope(kr * lax.rsqrt(kr_ss / rope + EPS) * gkr_ref[...], cos, sin_lo, sin_hi, half)
    kr_ref[...] = krr.reshape(bb, tt, V7X_LANES)
    ckv_ref[...] = ckv.reshape(bb, tt, kv_lora)

    q = jnp.dot(cq.astype(BF16), wuq_ref[...], preferred_element_type=F32)
    ckv_b = ckv.astype(BF16)
    if with_kv:
        kk = jnp.dot(ckv_b, wuk_ref[...], preferred_element_type=F32)
        v_ref[...] = jnp.dot(ckv_b, wuv_ref[...], preferred_element_type=F32).astype(BF16).reshape(v_ref.shape)
    for h in range(n_heads):
        sl = slice(V7X_LANES * h, V7X_LANES * (h + 1))
        qb = q[:, sl]
        sq = qb * qb
        ss_r = jnp.sum(jnp.where(is_rope, sq, 0.0), axis=-1, keepdims=True)
        ss_n = jnp.sum(jnp.where(is_rope, 0.0, sq), axis=-1, keepdims=True)
        rs = jnp.where(is_rope, lax.rsqrt(ss_r / rope + EPS), lax.rsqrt(ss_n / nope + EPS))
        y = _rope(qb * rs * gq_ref[...], cos, sin_lo, sin_hi, half)
        q_ref[:, :, sl] = (y * scale).astype(BF16).reshape(bb, tt, V7X_LANES)
        if with_kv:
            kb = kk[:, sl]
            ss_k = jnp.sum(kb * kb, axis=-1, keepdims=True)
            kn = kb * lax.rsqrt(ss_k / nope + EPS) * gk_ref[...]
            k_ref[:, :, sl] = (kn + krr).astype(BF16).reshape(bb, tt, V7X_LANES)


def _mla_proj(x, gmix, wd, gql, gkv, gkr, wuq, gq, wuk, gk, wuv, cos, sin_lo, sin_hi, *, bb, tt, dims, with_kv):
    n_heads, q_lora, kv_lora, nope, rope, vh, scale = dims
    b, t, d = x.shape
    hl = n_heads * V7X_LANES
    tok = lambda c: pl.BlockSpec((bb, tt, c), lambda i, j: (i, j, 0))
    tab = pl.BlockSpec((tt, V7X_LANES), lambda i, j: (j, 0))
    out_shape = [jax.ShapeDtypeStruct((b, t, hl), BF16), jax.ShapeDtypeStruct((b, t, kv_lora), F32),
                 jax.ShapeDtypeStruct((b, t, V7X_LANES), F32)]
    out_specs = [tok(hl), tok(kv_lora), tok(V7X_LANES)]
    if with_kv:
        out_shape += [jax.ShapeDtypeStruct((b, t, hl), BF16), jax.ShapeDtypeStruct((b, t, n_heads * vh), BF16)]
        out_specs += [tok(hl), tok(n_heads * vh)]
    consts = (gmix, wd, gql, gkv, gkr, wuq, gq, wuk, gk, wuv)
    return pl.pallas_call(
        functools.partial(_mla_proj_kernel, n_heads=n_heads, q_lora=q_lora, kv_lora=kv_lora, nope=nope, rope=rope,
                          scale=scale, with_kv=with_kv),
        grid=(b // bb, t // tt),
        in_specs=[tok(d)] + [_const_spec(a) for a in consts] + [tab, tab, tab],
        out_specs=out_specs, out_shape=out_shape,
        compiler_params=_params("parallel", "parallel"), name="mla_proj",
    )(x, *consts, cos, sin_lo, sin_hi)


def _flash_kernel(q_ref, k_ref, v_ref, o_ref, *, tq, vh):
    qi = pl.program_id(2)
    row = lax.broadcasted_iota(jnp.int32, (tq, tq), 0)
    col = lax.broadcasted_iota(jnp.int32, (tq, tq), 1)
    outs = []
    for hh in range(2):
        q = q_ref[0, :, V7X_LANES * hh:V7X_LANES * (hh + 1)]

        def tile(j, carry, masked, hh=hh, q=q):
            m, l, acc = carry
            start = pl.multiple_of(j * tq, tq)
            k = k_ref[0, pl.ds(start, tq), V7X_LANES * hh:V7X_LANES * (hh + 1)]
            v = v_ref[0, pl.ds(start, tq), vh * hh:vh * (hh + 1)]
            s = lax.dot_general(q, k, (((1,), (1,)), ((), ())), preferred_element_type=F32)
            if masked:
                s = jnp.where(col <= row, s, -jnp.inf)
            m_new = jnp.maximum(m, jnp.max(s, axis=1, keepdims=True))
            p = jnp.exp(s - m_new)
            alpha = jnp.exp(m - m_new)
            l = alpha * l + jnp.sum(p, axis=1, keepdims=True)
            acc = alpha * acc + jnp.dot(p.astype(BF16), v, preferred_element_type=F32)
            return m_new, l, acc

        init = (jnp.full((tq, 1), -jnp.inf, F32), jnp.zeros((tq, 1), F32), jnp.zeros((tq, vh), F32))
        carry = lax.fori_loop(0, qi, lambda j, c: tile(j, c, False), init)
        _, l, acc = tile(qi, carry, True)
        outs.append(acc / l)
    o_ref[0] = jnp.concatenate(outs, axis=1).astype(BF16)


def _flash(q, k, v, *, tq, n_heads, vh):
    b, t, _ = q.shape
    return pl.pallas_call(
        functools.partial(_flash_kernel, tq=tq, vh=vh),
        grid=(b, n_heads // 2, t // tq),
        in_specs=[pl.BlockSpec((1, tq, 2 * V7X_LANES), lambda i, h, j: (i, j, h)),
                  pl.BlockSpec((1, t, 2 * V7X_LANES), lambda i, h, j: (i, 0, h)),
                  pl.BlockSpec((1, t, 2 * vh), lambda i, h, j: (i, 0, h))],
        out_specs=pl.BlockSpec((1, tq, 2 * vh), lambda i, h, j: (i, j, h)),
        out_shape=jax.ShapeDtypeStruct((b, t, n_heads * vh), BF16),
        compiler_params=_params("parallel", "parallel", "arbitrary"), name="mla_prompt_attn",
    )(q, k, v)


def _absorb_kernel(q_ref, wa_ref, qa_ref, *, n_heads, kv_lora):
    for h in range(n_heads):
        qh = q_ref[:, V7X_LANES * h:V7X_LANES * (h + 1)]
        qa_ref[:, kv_lora * h:kv_lora * (h + 1)] = jnp.dot(qh, wa_ref[h], preferred_element_type=F32).astype(BF16)


def _absorb(q2d, wa, *, tr):
    r = q2d.shape[0]
    n_heads, _, kv_lora = wa.shape
    return pl.pallas_call(
        functools.partial(_absorb_kernel, n_heads=n_heads, kv_lora=kv_lora),
        grid=(r // tr,),
        in_specs=[pl.BlockSpec((tr, n_heads * V7X_LANES), lambda i: (i, 0)), _const_spec(wa)],
        out_specs=pl.BlockSpec((tr, n_heads * kv_lora), lambda i: (i, 0)),
        out_shape=jax.ShapeDtypeStruct((r, n_heads * kv_lora), BF16),
        compiler_params=_params("parallel"), name="mla_absorb_q",
    )(q2d, wa)


def _decode_attn_kernel(pt_ref, *refs, pg, n_heads, nope, rope, n_new):
    del pt_ref
    c_refs = refs[:pg]
    kr_refs = refs[pg:2 * pg]
    wk_ref, qa_ref, qr_ref, cnew_ref, krnew_ref, o_ref, m_scr, l_scr, acc_scr = refs[2 * pg:]
    g = pl.program_id(1)
    page = c_refs[0].shape[0]
    kv_lora = c_refs[0].shape[1]
    n_tiles = wk_ref.shape[1] // V7X_LANES
    qa = qa_ref[...]
    qr = qr_ref[:, :rope]

    def process(c, kr, mask):
        cb = c.astype(BF16)
        k = jnp.dot(cb, wk_ref[...], preferred_element_type=F32)
        ksq = k * k
        part = ksq[:, :V7X_LANES]
        for i in range(1, n_tiles):
            part = part + ksq[:, V7X_LANES * i:V7X_LANES * (i + 1)]
        shift = V7X_LANES // 2
        while shift >= n_heads:
            part = part + pltpu.roll(part, shift, axis=1)
            shift //= 2
        rs = lax.rsqrt(part / nope + EPS)
        s = lax.dot_general(cb, qa, (((1,), (1,)), ((), ())), preferred_element_type=F32) * rs
        s = s + lax.dot_general(kr.astype(BF16), qr, (((1,), (1,)), ((), ())), preferred_element_type=F32)
        if mask is not None:
            s = jnp.where(mask, s, -jnp.inf)
        m_old = m_scr[...]
        m_new = jnp.maximum(m_old, jnp.max(s, axis=0, keepdims=True))
        p = jnp.exp(s - m_new)
        alpha = jnp.exp(m_old - m_new)
        l_scr[...] = alpha * l_scr[...] + jnp.sum(p, axis=0, keepdims=True)
        pv = lax.dot_general(cb, p.astype(BF16), (((0,), (0,)), ((), ())), preferred_element_type=F32)
        acc_scr[...] = alpha * acc_scr[...] + pv
        m_scr[...] = m_new

    @pl.when(g == 0)
    def _():
        m_scr[...] = jnp.full(m_scr.shape, -jnp.inf, F32)
        l_scr[...] = jnp.zeros(l_scr.shape, F32)
        acc_scr[...] = jnp.zeros(acc_scr.shape, F32)
        pad = page - n_new
        c = jnp.concatenate([cnew_ref[...], jnp.zeros((pad, kv_lora), F32)], axis=0)
        kr = jnp.concatenate([krnew_ref[:, :rope], jnp.zeros((pad, rope), F32)], axis=0)
        key = lax.broadcasted_iota(jnp.int32, (page, V7X_LANES), 0)
        qry = lax.broadcasted_iota(jnp.int32, (page, V7X_LANES), 1) // n_heads
        process(c, kr, key <= qry)

    c = jnp.concatenate([r[...] for r in c_refs], axis=0)
    kr = jnp.concatenate([r[...] for r in kr_refs], axis=0)
    process(c, kr, None)

    @pl.when(g == pl.num_programs(1) - 1)
    def _():
        o_ref[...] = (acc_scr[...] / l_scr[...]).T


def _decode_attn(page_table, cache_ckv, cache_krope, layer, wk_perm, qa, qr, c_new, kr_new, *, pg, dims):
    n_heads, _, kv_lora, nope, rope, _, _ = dims
    b, n_pages = page_table.shape
    page = cache_ckv.shape[2]
    n_new = c_new.shape[1]
    rows = n_new * n_heads
    assert rows == V7X_LANES and n_pages % pg == 0 and n_new <= page
    c_specs = [pl.BlockSpec((None, None, page, kv_lora), lambda i, g, pt, s=s: (layer, pt[i, g * pg + s], 0, 0))
               for s in range(pg)]
    kr_specs = [pl.BlockSpec((None, None, page, rope), lambda i, g, pt, s=s: (layer, pt[i, g * pg + s], 0, 0))
                for s in range(pg)]
    per_seq = lambda r, c: pl.BlockSpec((None, r, c), lambda i, g, pt: (i, 0, 0))
    grid_spec = pltpu.PrefetchScalarGridSpec(
        num_scalar_prefetch=1, grid=(b, n_pages // pg),
        in_specs=c_specs + kr_specs + [pl.BlockSpec(wk_perm.shape, lambda i, g, pt: (0, 0)),
                                       per_seq(rows, kv_lora), per_seq(rows, V7X_LANES),
                                       per_seq(n_new, kv_lora), per_seq(n_new, V7X_LANES)],
        out_specs=per_seq(rows, kv_lora),
        scratch_shapes=[pltpu.VMEM((1, V7X_LANES), F32), pltpu.VMEM((1, V7X_LANES), F32),
                        pltpu.VMEM((kv_lora, V7X_LANES), F32)])
    return pl.pallas_call(
        functools.partial(_decode_attn_kernel, pg=pg, n_heads=n_heads, nope=nope, rope=rope, n_new=n_new),
        grid_spec=grid_spec, out_shape=jax.ShapeDtypeStruct((b, rows, kv_lora), F32),
        compiler_params=_params("parallel", "arbitrary"), name="mla_decode_attn",
    )(page_table, *([cache_ckv] * pg), *([cache_krope] * pg), wk_perm, qa, qr, c_new, kr_new)


def _unabsorb_kernel(ctx_ref, wv_ref, o_ref, *, n_pairs, kv_lora):
    for p in range(n_pairs):
        x = ctx_ref[:, 2 * kv_lora * p:2 * kv_lora * (p + 1)].astype(BF16)
        o_ref[:, V7X_LANES * p:V7X_LANES * (p + 1)] = jnp.dot(x, wv_ref[p], preferred_element_type=F32).astype(BF16)


def _unabsorb(ctx2d, wv2, *, tr):
    r = ctx2d.shape[0]
    n_pairs, k2, _ = wv2.shape
    return pl.pallas_call(
        functools.partial(_unabsorb_kernel, n_pairs=n_pairs, kv_lora=k2 // 2),
        grid=(r // tr,),
        in_specs=[pl.BlockSpec((tr, n_pairs * k2), lambda i: (i, 0)), _const_spec(wv2)],
        out_specs=pl.BlockSpec((tr, n_pairs * V7X_LANES), lambda i: (i, 0)),
        out_shape=jax.ShapeDtypeStruct((r, n_pairs * V7X_LANES), BF16),
        compiler_params=_params("parallel"), name="mla_unabsorb_v",
    )(ctx2d, wv2)


def _conv_prompt(g, w_ref, b_ref, buf_ref, carry):
    rows = g.shape[0]
    width = w_ref.shape[0]
    buf_ref[0:V7X_SUBLANES, :] = carry
    buf_ref[V7X_SUBLANES:, :] = g
    y = b_ref[...] + g * w_ref[width - 1:width, :]
    for s in range(1, width):
        y = y + buf_ref[V7X_SUBLANES - s:V7X_SUBLANES - s + rows, :] * w_ref[width - 1 - s:width - s, :]
    return y


def _conv_decode(g, w_ref, b_ref, prev, t_idx):
    width = w_ref.shape[0]
    rows = g.shape[0]
    y = b_ref[...] + g * w_ref[width - 1:width, :]
    for s in range(1, width):
        sh = jnp.where(t_idx >= s, pltpu.roll(g, s, axis=0), pltpu.roll(prev, rows - V7X_SUBLANES + s, axis=0))
        y = y + sh * w_ref[width - 1 - s:width - s, :]
    return y


def _ffn_kernel(*refs, decode):
    if decode:
        (x_ref, a_ref, wpre_ref, gn_ref, wg_ref, wu_ref, cw_ref, cb_ref, wd_ref, st_ref,
         y_ref, so_ref, x1_scr, xn_scr, acc_scr) = refs
    else:
        (x_ref, a_ref, wpre_ref, gn_ref, wg_ref, wu_ref, cw_ref, cb_ref, wd_ref,
         y_ref, so_ref, x1_scr, xn_scr, acc_scr, carry_scr, buf_scr) = refs
    bb, tt, d = x_ref.shape
    rows = bb * tt
    fc = wg_ref.shape[1]
    t = pl.program_id(1)
    c = pl.program_id(2)

    @pl.when(c == 0)
    def _():
        a = a_ref[...].reshape(rows, a_ref.shape[2])
        x1 = x_ref[...].reshape(rows, d) + jnp.dot(a, wpre_ref[...], preferred_element_type=F32)
        x1_scr[...] = x1
        xn_scr[...] = _rms(x1, gn_ref[...]).astype(BF16)
        acc_scr[...] = jnp.zeros(acc_scr.shape, F32)

    xn = xn_scr[...]
    g = jnp.dot(xn, wg_ref[...], preferred_element_type=F32)
    u = jnp.dot(xn, wu_ref[...], preferred_element_type=F32)
    if decode:
        t_idx = lax.broadcasted_iota(jnp.int32, (rows, fc), 0) % tt
        gc = _conv_decode(g, cw_ref, cb_ref, st_ref[...].reshape(rows, fc), t_idx)
        so_ref[...] = g.reshape(bb, tt, fc)
    else:
        @pl.when(t == 0)
        def _():
            carry_scr[c] = jnp.zeros((V7X_SUBLANES, fc), F32)

        gc = _conv_prompt(g, cw_ref, cb_ref, buf_scr, carry_scr[c])
        last = g[rows - V7X_SUBLANES:, :]
        carry_scr[c] = last
        so_ref[0] = last
    h = (gc * _sigmoid(gc) * u).astype(BF16)
    acc_scr[...] += jnp.dot(h, wd_ref[...], preferred_element_type=F32)

    @pl.when(c == pl.num_programs(2) - 1)
    def _():
        y_ref[...] = (x1_scr[...] + acc_scr[...]).reshape(bb, tt, d)


def _ffn(x, a, w_pre, gn, wg, wu, cw, cb, wd, state, *, bb, tt, fc):
    b, t, d = x.shape
    da = a.shape[2]
    dff = wg.shape[1]
    nc = dff // fc
    rows = bb * tt
    decode = state is not None
    tok = lambda c: pl.BlockSpec((bb, tt, c), lambda i, j, k: (i, j, 0))
    in_specs = [tok(d), tok(da), _const_spec(w_pre), _const_spec(gn),
                pl.BlockSpec((d, fc), lambda i, j, k: (0, k)), pl.BlockSpec((d, fc), lambda i, j, k: (0, k)),
                pl.BlockSpec((cw.shape[0], fc), lambda i, j, k: (0, k)), pl.BlockSpec((1, fc), lambda i, j, k: (0, k)),
                pl.BlockSpec((fc, d), lambda i, j, k: (k, 0))]
    args = [x, a, w_pre, gn, wg, wu, cw, cb, wd]
    scratch = [pltpu.VMEM((rows, d), F32), pltpu.VMEM((rows, d), BF16), pltpu.VMEM((rows, d), F32)]
    if decode:
        in_specs.append(pl.BlockSpec((bb, tt, fc), lambda i, j, k: (i, 0, k)))
        args.append(state)
        so_spec = pl.BlockSpec((bb, tt, fc), lambda i, j, k: (i, 0, k))
    else:
        assert bb == 1
        scratch += [pltpu.VMEM((nc, V7X_SUBLANES, fc), F32), pltpu.VMEM((V7X_SUBLANES + rows, fc), F32)]
        so_spec = pl.BlockSpec((1, V7X_SUBLANES, fc), lambda i, j, k: (i, 0, k))
    return pl.pallas_call(
        functools.partial(_ffn_kernel, decode=decode),
        grid=(b // bb, t // tt, nc), in_specs=in_specs,
        out_specs=[tok(d), so_spec],
        out_shape=[jax.ShapeDtypeStruct((b, t, d), F32), jax.ShapeDtypeStruct((b, V7X_SUBLANES, dff), F32)],
        scratch_shapes=scratch,
        compiler_params=_params("parallel", "arbitrary", "arbitrary"), name="conv_ffn",
    )(*args)


def _rec_kernel(*refs, decode):
    if decode:
        (x_ref, gn_ref, win_ref, cw_ref, cb_ref, wgate_ref, ba_ref, bi_ref, lam_ref, st_ref, h0_ref,
         a_ref, co_ref, ho_ref, a_scr, b_scr) = refs
    else:
        (x_ref, gn_ref, win_ref, cw_ref, cb_ref, wgate_ref, ba_ref, bi_ref, lam_ref,
         a_ref, co_ref, ho_ref, a_scr, b_scr, h_scr, carry_scr, hc_scr, buf_scr) = refs
    bb, tt, d = x_ref.shape
    rows = bb * tt
    w = cw_ref.shape[1]
    t = pl.program_id(1)
    xn = _rms(x_ref[...].reshape(rows, d), gn_ref[...]).astype(BF16)
    z = jnp.dot(xn, win_ref[...], preferred_element_type=F32)
    xr = z[:, :w]
    gin = z[:, w:]
    gate = 0.5 * gin * (1.0 + jnp.tanh(0.7978845608028654 * (gin + 0.044715 * (gin * gin * gin))))
    t_idx = lax.broadcasted_iota(jnp.int32, (rows, w), 0) % V7X_SUBLANES
    if decode:
        xc = _conv_decode(xr, cw_ref, cb_ref, st_ref[...].reshape(rows, w), t_idx)
        co_ref[...] = xr.reshape(bb, tt, w)
    else:
        @pl.when(t == 0)
        def _():
            carry_scr[...] = jnp.zeros(carry_scr.shape, F32)
            hc_scr[...] = jnp.zeros(hc_scr.shape, F32)

        xc = _conv_prompt(xr, cw_ref, cb_ref, buf_scr, carry_scr[...])
        last = xr[rows - V7X_SUBLANES:, :]
        carry_scr[...] = last
        co_ref[0] = last

    n_blk = wgate_ref.shape[0]
    kb = wgate_ref.shape[1]
    xcb = xc.astype(BF16)
    ra, ri = [], []
    for p in range(n_blk):
        zz = jnp.dot(xcb[:, kb * p:kb * (p + 1)], wgate_ref[p], preferred_element_type=F32)
        ra.append(zz[:, :kb])
        ri.append(zz[:, kb:])
    r = _sigmoid(jnp.concatenate(ra, axis=1) + ba_ref[...])
    gi = _sigmoid(jnp.concatenate(ri, axis=1) + bi_ref[...])
    neg_lam = -lam_ref[...]
    softplus = jnp.maximum(neg_lam, 0.0) + jnp.log(1.0 + jnp.exp(-jnp.abs(neg_lam)))
    log_a = -LRU_C * r * softplus
    av = jnp.exp(log_a)
    bv = jnp.sqrt(1.0 - jnp.exp(2.0 * log_a)) * (gi * xc)

    for s in (1, 2, 4):
        keep = t_idx >= s
        bv = jnp.where(keep, av * pltpu.roll(bv, s, axis=0) + bv, bv)
        av = jnp.where(keep, av * pltpu.roll(av, s, axis=0), av)
    if decode:
        hs = av * h0_ref[...].reshape(rows, w) + bv
        ho_ref[...] = hs.reshape(bb, tt, w)
    else:
        a_scr[...] = av
        b_scr[...] = bv

        def group(i, h_in):
            sl = pl.ds(pl.multiple_of(i * V7X_SUBLANES, V7X_SUBLANES), V7X_SUBLANES)
            hg = a_scr[sl, :] * h_in + b_scr[sl, :]
            h_scr[sl, :] = hg
            return jnp.broadcast_to(hg[V7X_SUBLANES - 1:, :], (V7X_SUBLANES, w))

        h_last = lax.fori_loop(0, rows // V7X_SUBLANES, group, hc_scr[...])
        hc_scr[...] = h_last
        ho_ref[0] = h_last
        hs = h_scr[...]
    a_ref[...] = (hs * gate).astype(BF16).reshape(bb, tt, w)


def _rec(x, gn, w_in, cw, cb, w_gate, b_a, b_i, lam, state, h0, *, bb, tt):
    b, t, d = x.shape
    w = cw.shape[1]
    rows = bb * tt
    decode = state is not None
    tok = lambda c: pl.BlockSpec((bb, tt, c), lambda i, j: (i, j, 0))
    consts = (gn, w_in, cw, cb, w_gate, b_a, b_i, lam)
    in_specs = [tok(d)] + [_const_spec(a) for a in consts]
    args = [x, *consts]
    scratch = [pltpu.VMEM((rows, w), F32), pltpu.VMEM((rows, w), F32)]
    grp = pl.BlockSpec((bb, V7X_SUBLANES, w), lambda i, j: (i, 0, 0))
    if decode:
        in_specs += [grp, grp]
        args += [state, h0]
    else:
        assert bb == 1
        scratch += [pltpu.VMEM((rows, w), F32), pltpu.VMEM((V7X_SUBLANES, w), F32), pltpu.VMEM((V7X_SUBLANES, w), F32),
                    pltpu.VMEM((V7X_SUBLANES + rows, w), F32)]
    return pl.pallas_call(
        functools.partial(_rec_kernel, decode=decode),
        grid=(b // bb, t // tt), in_specs=in_specs,
        out_specs=[tok(w), grp, grp],
        out_shape=[jax.ShapeDtypeStruct((b, t, w), BF16), jax.ShapeDtypeStruct((b, V7X_SUBLANES, w), F32),
                   jax.ShapeDtypeStruct((b, V7X_SUBLANES, w), F32)],
        scratch_shapes=scratch,
        compiler_params=_params("parallel", "arbitrary"), name="rglru_mixer",
    )(*args)


def _pad_state(buf):
    return jnp.pad(buf, ((0, 0), (V7X_SUBLANES - buf.shape[1], 0), (0, 0)))


def kernel(x_prompt, x_sample, cache_ckv, cache_krope, page_table, state_lru_h, state_lru_conv, state_ffn_conv,
           norm_mix, norm_ffn, attn_w_down, attn_g_q_lora, attn_w_uq, attn_g_kv_lora, attn_g_qn, attn_g_qr,
           attn_g_kn, attn_g_kr, attn_w_uk, attn_w_uv, attn_w_o, rec_w_in, rec_conv_w, rec_conv_b, rec_w_a,
           rec_b_a, rec_w_i, rec_b_i, rec_lambda, rec_w_out, ffn_w_up, ffn_conv_w, ffn_conv_b, ffn_w_down):
    bp, tp, d = x_prompt.shape
    bs, ts, _ = x_sample.shape
    depth = norm_mix.shape[0]
    n_mixers = 2
    kv_lora, n_heads, nope = attn_w_uk.shape[1:]
    vh = attn_w_uv.shape[3]
    rope = attn_g_qr.shape[1]
    q_lora = attn_g_q_lora.shape[1]
    half = rope // 2
    page = cache_ckv.shape[2]
    past = page_table.shape[1] * page
    dff = ffn_conv_w.shape[2]
    lru_w = rec_conv_w.shape[2]
    assert ts == V7X_SUBLANES and rope + nope <= V7X_LANES and n_heads % 2 == 0
    scale = float((nope + rope) ** -0.5)
    dims = (n_heads, q_lora, kv_lora, nope, rope, vh, scale)
    pad_l = V7X_LANES - rope - nope
    row = lambda v: v.reshape(1, -1).astype(F32)

    def rope_tables(pos):
        inv = ROPE_THETA ** (-jnp.arange(0, rope, 2, dtype=F32) / rope)
        ang = pos.astype(F32)[:, None] * inv[None, :]
        cos, sin = jnp.cos(ang), jnp.sin(ang)
        n = pos.shape[0]
        ones = jnp.ones((n, V7X_LANES - rope), F32)
        zeros = jnp.zeros((n, V7X_LANES - rope), F32)
        z_half = jnp.zeros((n, half), F32)
        return (jnp.concatenate([cos, cos, ones], axis=1), jnp.concatenate([-sin, z_half, zeros], axis=1),
                jnp.concatenate([z_half, sin, zeros], axis=1))

    tabs_p = rope_tables(jnp.arange(tp))
    tabs_s = rope_tables(past + jnp.arange(ts))

    xp, xs = x_prompt, x_sample
    a_p = a_s = w_pre = None
    ckv_p, kr_p, ckv_s, kr_s = [], [], [], []
    lh_p, lh_s, lc_p, lc_s = [], [], [], []
    fc_p, fc_s = [], []
    tt_proj = min(tp, 256)
    bb_proj = min(bs, 256 // ts)
    tt_ffn = min(tp, 512)
    bb_ffn = min(bs, 512 // ts)
    tt_rec = min(tp, 256)
    bb_rec = min(bs, 256 // ts)
    for i in range(depth):
        j = i // n_mixers
        gmix = row(norm_mix[i])
        if i % n_mixers == 0:
            wdn = attn_w_down[j]
            wd = jnp.concatenate([wdn, jnp.zeros((d, V7X_LANES - rope), F32)], axis=1).astype(BF16)
            gkr = jnp.concatenate([attn_g_kr[j], jnp.zeros((V7X_LANES - rope,), F32)]).reshape(1, -1)
            wq = attn_w_uq[j].reshape(q_lora, n_heads, nope + rope)
            wuq = jnp.concatenate([wq[:, :, nope:], wq[:, :, :nope], jnp.zeros((q_lora, n_heads, pad_l), F32)],
                                  axis=2).reshape(q_lora, n_heads * V7X_LANES).astype(BF16)
            gq = jnp.concatenate([attn_g_qr[j], attn_g_qn[j], jnp.zeros((pad_l,), F32)]).reshape(1, -1)
            wk = attn_w_uk[j]
            wuk = jnp.concatenate([jnp.zeros((kv_lora, n_heads, rope), F32), wk,
                                   jnp.zeros((kv_lora, n_heads, pad_l), F32)],
                                  axis=2).reshape(kv_lora, n_heads * V7X_LANES).astype(BF16)
            gk = jnp.concatenate([jnp.zeros((rope,), F32), attn_g_kn[j], jnp.zeros((pad_l,), F32)]).reshape(1, -1)
            wuv = attn_w_uv[j].reshape(kv_lora, n_heads * vh).astype(BF16)
            proj_w = (gmix, wd, row(attn_g_q_lora[j]), row(attn_g_kv_lora[j]), gkr, wuq, gq, wuk, gk, wuv)

            q_p, c_p, krb_p, k_p, v_p = _mla_proj(xp, *proj_w, *tabs_p, bb=1, tt=tt_proj, dims=dims, with_kv=True)
            q_s, c_s, krb_s = _mla_proj(xs, *proj_w, *tabs_s, bb=bb_proj, tt=ts, dims=dims, with_kv=False)
            a_p = _flash(q_p, k_p, v_p, tq=min(tp, 256), n_heads=n_heads, vh=vh)

            wa = jnp.transpose(wk * attn_g_kn[j][None, None, :], (1, 2, 0))
            wa = jnp.concatenate([jnp.zeros((n_heads, rope, kv_lora), F32), wa,
                                  jnp.zeros((n_heads, pad_l, kv_lora), F32)], axis=1).astype(BF16)
            qa = _absorb(q_s.reshape(bs * ts, n_heads * V7X_LANES), wa, tr=min(bs * ts, 256))
            qa = qa.reshape(bs, ts * n_heads, kv_lora)
            qr = q_s.reshape(bs, ts * n_heads, V7X_LANES)
            wk_perm = jnp.transpose(wk, (0, 2, 1)).reshape(kv_lora, nope * n_heads).astype(BF16)
            ctx = _decode_attn(page_table, cache_ckv, cache_krope, j, wk_perm, qa, qr, c_s, krb_s,
                               pg=min(page_table.shape[1], 8), dims=dims)
            wv = attn_w_uv[j]
            zer = jnp.zeros((kv_lora, n_heads // 2, vh), F32)
            wv2 = jnp.concatenate([jnp.concatenate([wv[:, 0::2], zer], axis=2),
                                   jnp.concatenate([zer, wv[:, 1::2]], axis=2)], axis=0)
            wv2 = jnp.transpose(wv2, (1, 0, 2)).astype(BF16)
            a_s = _unabsorb(ctx.reshape(bs * ts, n_heads * kv_lora), wv2, tr=min(bs * ts, 256))
            a_s = a_s.reshape(bs, ts, n_heads * vh)
            w_pre = attn_w_o[j].astype(BF16)
            ckv_p.append(c_p)
            kr_p.append(krb_p[:, :, :rope])
            ckv_s.append(c_s)
            kr_s.append(krb_s[:, :, :rope])
        else:
            n_blk, blk = rec_w_a.shape[1:3]
            assert n_blk % 2 == 0

            def pair(wm):
                z = jnp.zeros((n_blk // 2, blk, blk), F32)
                return jnp.concatenate([jnp.concatenate([wm[0::2], z], axis=2),
                                        jnp.concatenate([z, wm[1::2]], axis=2)], axis=1)

            w_gate = jnp.concatenate([pair(rec_w_a[j]), pair(rec_w_i[j])], axis=2).astype(BF16)
            rec_w = (gmix, rec_w_in[j].astype(BF16), rec_conv_w[j], row(rec_conv_b[j]), w_gate, row(rec_b_a[j]),
                     row(rec_b_i[j]), row(rec_lambda[j]))
            a_p, cv_p, h_p = _rec(xp, *rec_w, None, None, bb=1, tt=tt_rec)
            h0 = jnp.broadcast_to(state_lru_h[j][:, None, :], (bs, ts, lru_w))
            a_s, cv_s, h_s = _rec(xs, *rec_w, _pad_state(state_lru_conv[j]), h0, bb=bb_rec, tt=ts)
            w_pre = rec_w_out[j].astype(BF16)
            kw = rec_conv_w.shape[1] - 1
            lh_p.append(h_p[:, V7X_SUBLANES - 1])
            lh_s.append(h_s[:, V7X_SUBLANES - 1])
            lc_p.append(cv_p[:, V7X_SUBLANES - kw:])
            lc_s.append(cv_s[:, V7X_SUBLANES - kw:])
        wup = ffn_w_up[i]
        ffn_w = (w_pre, row(norm_ffn[i]), wup[:, :dff].astype(BF16), wup[:, dff:].astype(BF16), ffn_conv_w[i],
                 row(ffn_conv_b[i]), ffn_w_down[i].astype(BF16))
        xp, fb_p = _ffn(xp, a_p, *ffn_w, None, bb=1, tt=tt_ffn, fc=min(dff, 512))
        xs, fb_s = _ffn(xs, a_s, *ffn_w, _pad_state(state_ffn_conv[i]), bb=bb_ffn, tt=ts, fc=min(dff, 512))
        kf = ffn_conv_w.shape[1] - 1
        fc_p.append(fb_p[:, V7X_SUBLANES - kf:])
        fc_s.append(fb_s[:, V7X_SUBLANES - kf:])
    return (xp, xs, jnp.stack(ckv_p), jnp.stack(kr_p), jnp.stack(ckv_s), jnp.stack(kr_s),
            jnp.stack(lh_p), jnp.stack(lh_s), jnp.stack(lc_p), jnp.stack(lc_s), jnp.stack(fc_p), jnp.stack(fc_s))
```

```python
import functools
import math

import jax
import jax.numpy as jnp
from jax import lax
from jax.experimental import pallas as pl
from jax.experimental.pallas import tpu as pltpu

EPS = 1e-6
ROPE_THETA = 10000.0
LRU_C = 8.0

V7X_LANES = 128
V7X_SUBLANES = 8
V7X_VMEM_BYTES = 64 * 1024 * 1024
VMEM_LIMIT = V7X_VMEM_BYTES * 7 // 8

F32 = jnp.float32
BF16 = jnp.bfloat16


def _params(*sem):
    return pltpu.CompilerParams(dimension_semantics=sem, vmem_limit_bytes=VMEM_LIMIT)


def _const_spec(a):
    nd = a.ndim
    return pl.BlockSpec(a.shape, lambda *_: (0,) * nd)


def _rms(x, g):
    ms = jnp.mean(x * x, axis=-1, keepdims=True)
    return x * lax.rsqrt(ms + EPS) * g


def _sigmoid(x):
    return 1.0 / (1.0 + jnp.exp(-x))


def _rope(y, cos, sin_lo, sin_hi, half):
    return y * cos + pltpu.roll(y, V7X_LANES - half, axis=1) * sin_lo + pltpu.roll(y, half, axis=1) * sin_hi


def _tile_rows(tab_ref, bb):
    t = tab_ref[...]
    if bb == 1:
        return t
    return jnp.broadcast_to(t[None], (bb,) + t.shape).reshape(bb * t.shape[0], t.shape[1])


def _mla_proj_kernel(x_ref, gmix_ref, wd_ref, gql_ref, gkv_ref, gkr_ref, wuq_ref, gq_ref, wuk_ref, gk_ref, wuv_ref,
                     cos_ref, sinlo_ref, sinhi_ref, *out_refs, n_heads, q_lora, kv_lora, nope, rope, scale, with_kv):
    if with_kv:
        q_ref, ckv_ref, kr_ref, k_ref, v_ref = out_refs
    else:
        q_ref, ckv_ref, kr_ref = out_refs
    bb, tt, d = x_ref.shape
    rows = bb * tt
    half = rope // 2
    xn = _rms(x_ref[...].reshape(rows, d), gmix_ref[...])
    dd = jnp.dot(xn.astype(BF16), wd_ref[...], preferred_element_type=F32)
    cq = _rms(dd[:, :q_lora], gql_ref[...])
    ckv = _rms(dd[:, q_lora:q_lora + kv_lora], gkv_ref[...])
    kr = dd[:, q_lora + kv_lora:]
    cos = _tile_rows(cos_ref, bb)
    sin_lo = _tile_rows(sinlo_ref, bb)
    sin_hi = _tile_rows(sinhi_ref, bb)
    lane = lax.broadcasted_iota(jnp.int32, (1, V7X_LANES), 1)
    is_rope = lane < rope

    kr_ss = jnp.sum(kr * kr, axis=-1, keepdims=True)
    krr = _rope(kr * lax.rsqrt(kr_ss / rope + EPS) * gkr_ref[...], cos, sin_lo, sin_hi, half)
    kr_ref[...] = krr.reshape(bb, tt, V7X_LANES)
    ckv_ref[...] = ckv.reshape(bb, tt, kv_lora)

    q = jnp.dot(cq.astype(BF16), wuq_ref[...], preferred_element_type=F32)
    ckv_b = ckv.astype(BF16)
    if with_kv:
        kk = jnp.dot(ckv_b, wuk_ref[...], preferred_element_type=F32)
        v_ref[...] = jnp.dot(ckv_b, wuv_ref[...], preferred_element_type=F32).astype(BF16).reshape(v_ref.shape)
    for h in range(n_heads):
        sl = slice(V7X_LANES * h, V7X_LANES * (h + 1))
        qb = q[:, sl]
        sq = qb * qb
        ss_r = jnp.sum(jnp.where(is_rope, sq, 0.0), axis=-1, keepdims=True)
        ss_n = jnp.sum(jnp.where(is_rope, 0.0, sq), axis=-1, keepdims=True)
        rs = jnp.where(is_rope, lax.rsqrt(ss_r / rope + EPS), lax.rsqrt(ss_n / nope + EPS))
        y = _rope(qb * rs * gq_ref[...], cos, sin_lo, sin_hi, half)
        q_ref[:, :, sl] = (y * scale).astype(BF16).reshape(bb, tt, V7X_LANES)
        if with_kv:
            kb = kk[:, sl]
            ss_k = jnp.sum(kb * kb, axis=-1, keepdims=True)
            kn = kb * lax.rsqrt(ss_k / nope + EPS) * gk_ref[...]
            k_ref[:, :, sl] = (kn + krr).astype(BF16).reshape(bb, tt, V7X_LANES)


def _mla_proj(x, gmix, wd, gql, gkv, gkr, wuq, gq, wuk, gk, wuv, cos, sin_lo, sin_hi, *, bb, tt, dims, with_kv):
    n_heads, q_lora, kv_lora, nope, rope, vh, scale = dims
    b, t, d = x.shape
    hl = n_heads * V7X_LANES
    tok = lambda c: pl.BlockSpec((bb, tt, c), lambda i, j: (i, j, 0))
    tab = pl.BlockSpec((tt, V7X_LANES), lambda i, j: (j, 0))
    out_shape = [jax.ShapeDtypeStruct((b, t, hl), BF16), jax.ShapeDtypeStruct((b, t, kv_lora), F32),
                 jax.ShapeDtypeStruct((b, t, V7X_LANES), F32)]
    out_specs = [tok(hl), tok(kv_lora), tok(V7X_LANES)]
    if with_kv:
        out_shape += [jax.ShapeDtypeStruct((b, t, hl), BF16), jax.ShapeDtypeStruct((b, t, n_heads * vh), BF16)]
        out_specs += [tok(hl), tok(n_heads * vh)]
    consts = (gmix, wd, gql, gkv, gkr, wuq, gq, wuk, gk, wuv)
    return pl.pallas_call(
        functools.partial(_mla_proj_kernel, n_heads=n_heads, q_lora=q_lora, kv_lora=kv_lora, nope=nope, rope=rope,
                          scale=scale, with_kv=with_kv),
        grid=(b // bb, t // tt),
        in_specs=[tok(d)] + [_const_spec(a) for a in consts] + [tab, tab, tab],
        out_specs=out_specs, out_shape=out_shape,
        compiler_params=_params("parallel", "parallel"), name="mla_proj",
    )(x, *consts, cos, sin_lo, sin_hi)


def _attn_block(q, k_ref, v_ref, lanes, vlanes, past, tq):
    dn = (((1,), (1,)), ((), ()))
    row = lax.broadcasted_iota(jnp.int32, (tq, tq), 0)
    col = lax.broadcasted_iota(jnp.int32, (tq, tq), 1)
    s_d = lax.dot_general(q, k_ref[0, past:past + tq, lanes], dn, preferred_element_type=F32)
    s_d = jnp.where(col <= row, s_d, -jnp.inf)
    m = jnp.max(s_d, axis=1, keepdims=True)
    if past:
        s_p = lax.dot_general(q, k_ref[0, :past, lanes], dn, preferred_element_type=F32)
        m = jnp.maximum(m, jnp.max(s_p, axis=1, keepdims=True))
    p_d = jnp.exp2(s_d - m)
    l = jnp.sum(p_d, axis=1, keepdims=True)
    acc = jnp.dot(p_d.astype(BF16), v_ref[0, past:past + tq, vlanes], preferred_element_type=F32)
    if past:
        p_p = jnp.exp2(s_p - m)
        l = l + jnp.sum(p_p, axis=1, keepdims=True)
        acc = acc + jnp.dot(p_p.astype(BF16), v_ref[0, :past, vlanes], preferred_element_type=F32)
    return acc / l


def _flash_kernel(q_ref, k_ref, v_ref, o_ref, *, tq, vh, n_q):
    qi = pl.program_id(2)
    for n in range(n_q):
        @pl.when(qi == n)
        def _(n=n):
            outs = []
            for hh in range(2):
                lanes = slice(V7X_LANES * hh, V7X_LANES * (hh + 1))
                outs.append(_attn_block(q_ref[0, :, lanes], k_ref, v_ref, lanes, slice(vh * hh, vh * (hh + 1)),
                                        n * tq, tq))
            o_ref[0] = jnp.concatenate(outs, axis=1).astype(BF16)


def _flash(q, k, v, *, tq, n_heads, vh):
    b, t, _ = q.shape
    return pl.pallas_call(
        functools.partial(_flash_kernel, tq=tq, vh=vh, n_q=t // tq),
        grid=(b, n_heads // 2, t // tq),
        in_specs=[pl.BlockSpec((1, tq, 2 * V7X_LANES), lambda i, h, j: (i, j, h)),
                  pl.BlockSpec((1, t, 2 * V7X_LANES), lambda i, h, j: (i, 0, h)),
                  pl.BlockSpec((1, t, 2 * vh), lambda i, h, j: (i, 0, h))],
        out_specs=pl.BlockSpec((1, tq, 2 * vh), lambda i, h, j: (i, j, h)),
        out_shape=jax.ShapeDtypeStruct((b, t, n_heads * vh), BF16),
        compiler_params=_params("parallel", "parallel", "arbitrary"), name="mla_prompt_attn",
    )(q, k, v)


def _absorb_kernel(q_ref, wa_ref, qa_ref, *, n_heads, kv_lora):
    for h in range(n_heads):
        qh = q_ref[:, V7X_LANES * h:V7X_LANES * (h + 1)]
        qa_ref[:, kv_lora * h:kv_lora * (h + 1)] = jnp.dot(qh, wa_ref[h], preferred_element_type=F32).astype(BF16)


def _absorb(q2d, wa, *, tr):
    r = q2d.shape[0]
    n_heads, _, kv_lora = wa.shape
    return pl.pallas_call(
        functools.partial(_absorb_kernel, n_heads=n_heads, kv_lora=kv_lora),
        grid=(r // tr,),
        in_specs=[pl.BlockSpec((tr, n_heads * V7X_LANES), lambda i: (i, 0)), _const_spec(wa)],
        out_specs=pl.BlockSpec((tr, n_heads * kv_lora), lambda i: (i, 0)),
        out_shape=jax.ShapeDtypeStruct((r, n_heads * kv_lora), BF16),
        compiler_params=_params("parallel"), name="mla_absorb_q",
    )(q2d, wa)


def _decode_attn_kernel(pt_ref, *refs, pg, n_heads, nope, rope, n_new):
    del pt_ref
    c_refs = refs[:pg]
    kr_refs = refs[pg:2 * pg]
    wk_ref, qa_ref, qr_ref, cnew_ref, krnew_ref, o_ref, m_scr, l_scr, acc_scr = refs[2 * pg:]
    g = pl.program_id(1)
    page = c_refs[0].shape[0]
    kv_lora = c_refs[0].shape[1]
    n_tiles = wk_ref.shape[1] // V7X_LANES
    qa = qa_ref[...]
    qr = qr_ref[:, :rope]

    def process(c, kr, kr_dim, mask):
        cb = c.astype(BF16)
        k = jnp.dot(cb, wk_ref[...], preferred_element_type=F32)
        ksq = k * k
        part = ksq[:, :V7X_LANES]
        for i in range(1, n_tiles):
            part = part + ksq[:, V7X_LANES * i:V7X_LANES * (i + 1)]
        shift = V7X_LANES // 2
        while shift >= n_heads:
            part = part + pltpu.roll(part, shift, axis=1)
            shift //= 2
        rs = lax.rsqrt(part / nope + EPS)
        s = lax.dot_general(cb, qa, (((1,), (1,)), ((), ())), preferred_element_type=F32) * rs
        s = s + lax.dot_general(kr.astype(BF16), qr, (((kr_dim,), (1,)), ((), ())), preferred_element_type=F32)
        if mask is not None:
            s = jnp.where(mask, s, -jnp.inf)
        m_old = m_scr[...]
        m_new = jnp.maximum(m_old, jnp.max(s, axis=0, keepdims=True))
        p = jnp.exp2(s - m_new)
        alpha = jnp.exp2(m_old - m_new)
        l_scr[...] = alpha * l_scr[...] + jnp.sum(p, axis=0, keepdims=True)
        pv = lax.dot_general(cb, p.astype(BF16), (((0,), (0,)), ((), ())), preferred_element_type=F32)
        acc_scr[...] = alpha * acc_scr[...] + pv
        m_scr[...] = m_new

    @pl.when(g == 0)
    def _():
        m_scr[...] = jnp.full(m_scr.shape, -jnp.inf, F32)
        l_scr[...] = jnp.zeros(l_scr.shape, F32)
        acc_scr[...] = jnp.zeros(acc_scr.shape, F32)
        pad = page - n_new
        c = jnp.concatenate([cnew_ref[...], jnp.zeros((pad, kv_lora), F32)], axis=0)
        kr = jnp.concatenate([krnew_ref[:, :rope], jnp.zeros((pad, rope), F32)], axis=0)
        key = lax.broadcasted_iota(jnp.int32, (page, V7X_LANES), 0)
        qry = lax.broadcasted_iota(jnp.int32, (page, V7X_LANES), 1) // n_heads
        process(c, kr, 1, key <= qry)

    c = jnp.concatenate([r[...] for r in c_refs], axis=0)
    kr = jnp.concatenate([r[...] for r in kr_refs], axis=1)
    process(c, kr, 0, None)

    @pl.when(g == pl.num_programs(1) - 1)
    def _():
        o_ref[...] = (acc_scr[...] / l_scr[...]).T


def _decode_attn(page_table, cache_ckv, cache_krope, layer, wk_perm, qa, qr, c_new, kr_new, *, pg, dims):
    n_heads, _, kv_lora, nope, rope, _, _ = dims
    b, n_pages = page_table.shape
    page = cache_ckv.shape[2]
    n_new = c_new.shape[1]
    rows = n_new * n_heads
    assert rows == V7X_LANES and n_pages % pg == 0 and n_new <= page
    c_specs = [pl.BlockSpec((None, None, page, kv_lora), lambda i, g, pt, s=s: (layer, pt[i, g * pg + s], 0, 0))
               for s in range(pg)]
    kr_specs = [pl.BlockSpec((None, None, rope, page), lambda i, g, pt, s=s: (layer, pt[i, g * pg + s], 0, 0))
                for s in range(pg)]
    krope_t = jnp.swapaxes(cache_krope, 2, 3)
    per_seq = lambda r, c: pl.BlockSpec((None, r, c), lambda i, g, pt: (i, 0, 0))
    grid_spec = pltpu.PrefetchScalarGridSpec(
        num_scalar_prefetch=1, grid=(b, n_pages // pg),
        in_specs=c_specs + kr_specs + [pl.BlockSpec(wk_perm.shape, lambda i, g, pt: (0, 0)),
                                       per_seq(rows, kv_lora), per_seq(rows, V7X_LANES),
                                       per_seq(n_new, kv_lora), per_seq(n_new, V7X_LANES)],
        out_specs=per_seq(rows, kv_lora),
        scratch_shapes=[pltpu.VMEM((1, V7X_LANES), F32), pltpu.VMEM((1, V7X_LANES), F32),
                        pltpu.VMEM((kv_lora, V7X_LANES), F32)])
    return pl.pallas_call(
        functools.partial(_decode_attn_kernel, pg=pg, n_heads=n_heads, nope=nope, rope=rope, n_new=n_new),
        grid_spec=grid_spec, out_shape=jax.ShapeDtypeStruct((b, rows, kv_lora), F32),
        compiler_params=_params("parallel", "arbitrary"), name="mla_decode_attn",
    )(page_table, *([cache_ckv] * pg), *([krope_t] * pg), wk_perm, qa, qr, c_new, kr_new)


def _unabsorb_kernel(ctx_ref, wv_ref, o_ref, *, n_pairs, kv_lora):
    for p in range(n_pairs):
        x = ctx_ref[:, 2 * kv_lora * p:2 * kv_lora * (p + 1)].astype(BF16)
        o_ref[:, V7X_LANES * p:V7X_LANES * (p + 1)] = jnp.dot(x, wv_ref[p], preferred_element_type=F32).astype(BF16)


def _unabsorb(ctx2d, wv2, *, tr):
    r = ctx2d.shape[0]
    n_pairs, k2, _ = wv2.shape
    return pl.pallas_call(
        functools.partial(_unabsorb_kernel, n_pairs=n_pairs, kv_lora=k2 // 2),
        grid=(r // tr,),
        in_specs=[pl.BlockSpec((tr, n_pairs * k2), lambda i: (i, 0)), _const_spec(wv2)],
        out_specs=pl.BlockSpec((tr, n_pairs * V7X_LANES), lambda i: (i, 0)),
        out_shape=jax.ShapeDtypeStruct((r, n_pairs * V7X_LANES), BF16),
        compiler_params=_params("parallel"), name="mla_unabsorb_v",
    )(ctx2d, wv2)


def _conv_prompt(g, w_ref, b_ref, buf_ref, carry):
    rows = g.shape[0]
    width = w_ref.shape[0]
    buf_ref[0:V7X_SUBLANES, :] = carry
    buf_ref[V7X_SUBLANES:, :] = g
    y = b_ref[...] + g * w_ref[width - 1:width, :]
    for s in range(1, width):
        y = y + buf_ref[V7X_SUBLANES - s:V7X_SUBLANES - s + rows, :] * w_ref[width - 1 - s:width - s, :]
    return y


def _conv_decode(g, w_ref, b_ref, prev, t_idx):
    width = w_ref.shape[0]
    rows = g.shape[0]
    y = b_ref[...] + g * w_ref[width - 1:width, :]
    for s in range(1, width):
        sh = jnp.where(t_idx >= s, pltpu.roll(g, s, axis=0), pltpu.roll(prev, rows - V7X_SUBLANES + s, axis=0))
        y = y + sh * w_ref[width - 1 - s:width - s, :]
    return y


def _ffn_kernel(*refs, decode):
    if decode:
        (x_ref, a_ref, wpre_ref, gn_ref, wg_ref, wu_ref, cw_ref, cb_ref, wd_ref, st_ref,
         y_ref, so_ref, x1_scr, xn_scr, acc_scr) = refs
    else:
        (x_ref, a_ref, wpre_ref, gn_ref, wg_ref, wu_ref, cw_ref, cb_ref, wd_ref,
         y_ref, so_ref, x1_scr, xn_scr, acc_scr, carry_scr, buf_scr) = refs
    bb, tt, d = x_ref.shape
    rows = bb * tt
    fc = wg_ref.shape[1]
    t = pl.program_id(1)
    c = pl.program_id(2)

    @pl.when(c == 0)
    def _():
        a = a_ref[...].reshape(rows, a_ref.shape[2])
        x1 = x_ref[...].reshape(rows, d) + jnp.dot(a, wpre_ref[...], preferred_element_type=F32)
        x1_scr[...] = x1
        xn_scr[...] = _rms(x1, gn_ref[...]).astype(BF16)
        acc_scr[...] = jnp.zeros(acc_scr.shape, F32)

    xn = xn_scr[...]
    g = jnp.dot(xn, wg_ref[...], preferred_element_type=F32)
    u = jnp.dot(xn, wu_ref[...], preferred_element_type=F32)
    if decode:
        t_idx = lax.broadcasted_iota(jnp.int32, (rows, fc), 0) % tt
        gc = _conv_decode(g, cw_ref, cb_ref, st_ref[...].reshape(rows, fc), t_idx)
        so_ref[...] = g.reshape(bb, tt, fc)
    else:
        @pl.when(t == 0)
        def _():
            carry_scr[c] = jnp.zeros((V7X_SUBLANES, fc), F32)

        gc = _conv_prompt(g, cw_ref, cb_ref, buf_scr, carry_scr[c])
        last = g[rows - V7X_SUBLANES:, :]
        carry_scr[c] = last
        so_ref[0] = last
    h = (gc * _sigmoid(gc) * u).astype(BF16)
    acc_scr[...] += jnp.dot(h, wd_ref[...], preferred_element_type=F32)

    @pl.when(c == pl.num_programs(2) - 1)
    def _():
        y_ref[...] = (x1_scr[...] + acc_scr[...]).reshape(bb, tt, d)


def _ffn(x, a, w_pre, gn, wg, wu, cw, cb, wd, state, *, bb, tt, fc):
    b, t, d = x.shape
    da = a.shape[2]
    dff = wg.shape[1]
    nc = dff // fc
    rows = bb * tt
    decode = state is not None
    tok = lambda c: pl.BlockSpec((bb, tt, c), lambda i, j, k: (i, j, 0))
    in_specs = [tok(d), tok(da), _const_spec(w_pre), _const_spec(gn),
                pl.BlockSpec((d, fc), lambda i, j, k: (0, k)), pl.BlockSpec((d, fc), lambda i, j, k: (0, k)),
                pl.BlockSpec((cw.shape[0], fc), lambda i, j, k: (0, k)), pl.BlockSpec((1, fc), lambda i, j, k: (0, k)),
                pl.BlockSpec((fc, d), lambda i, j, k: (k, 0))]
    args = [x, a, w_pre, gn, wg, wu, cw, cb, wd]
    scratch = [pltpu.VMEM((rows, d), F32), pltpu.VMEM((rows, d), BF16), pltpu.VMEM((rows, d), F32)]
    if decode:
        in_specs.append(pl.BlockSpec((bb, tt, fc), lambda i, j, k: (i, 0, k)))
        args.append(state)
    else:
        assert bb == 1
        scratch += [pltpu.VMEM((nc, V7X_SUBLANES, fc), F32), pltpu.VMEM((V7X_SUBLANES + rows, fc), F32)]
    so_spec = pl.BlockSpec((bb, None, V7X_SUBLANES, fc), lambda i, j, k: (i, j, 0, k))
    y, so = pl.pallas_call(
        functools.partial(_ffn_kernel, decode=decode),
        grid=(b // bb, t // tt, nc), in_specs=in_specs,
        out_specs=[tok(d), so_spec],
        out_shape=[jax.ShapeDtypeStruct((b, t, d), F32),
                   jax.ShapeDtypeStruct((b, t // tt, V7X_SUBLANES, dff), F32)],
        scratch_shapes=scratch,
        compiler_params=_params("parallel", "arbitrary", "arbitrary"), name="conv_ffn",
    )(*args)
    return y, so[:, -1]


def _rec_kernel(*refs, decode):
    if decode:
        (x_ref, gn_ref, win_ref, cw_ref, cb_ref, wgate_ref, ba_ref, bi_ref, lam_ref, st_ref, h0_ref,
         a_ref, co_ref, ho_ref, a_scr, b_scr) = refs
    else:
        (x_ref, gn_ref, win_ref, cw_ref, cb_ref, wgate_ref, ba_ref, bi_ref, lam_ref,
         a_ref, co_ref, ho_ref, a_scr, b_scr, h_scr, carry_scr, hc_scr, buf_scr) = refs
    bb, tt, d = x_ref.shape
    rows = bb * tt
    w = cw_ref.shape[1]
    t = pl.program_id(1)
    xn = _rms(x_ref[...].reshape(rows, d), gn_ref[...]).astype(BF16)
    z = jnp.dot(xn, win_ref[...], preferred_element_type=F32)
    xr = z[:, :w]
    gin = z[:, w:]
    gate = 0.5 * gin * (1.0 + jnp.tanh(0.7978845608028654 * (gin + 0.044715 * (gin * gin * gin))))
    t_idx = lax.broadcasted_iota(jnp.int32, (rows, w), 0) % V7X_SUBLANES
    if decode:
        xc = _conv_decode(xr, cw_ref, cb_ref, st_ref[...].reshape(rows, w), t_idx)
        co_ref[...] = xr.reshape(bb, tt, w)
    else:
        @pl.when(t == 0)
        def _():
            carry_scr[...] = jnp.zeros(carry_scr.shape, F32)
            hc_scr[...] = jnp.zeros(hc_scr.shape, F32)

        xc = _conv_prompt(xr, cw_ref, cb_ref, buf_scr, carry_scr[...])
        last = xr[rows - V7X_SUBLANES:, :]
        carry_scr[...] = last
        co_ref[0] = last

    n_blk = wgate_ref.shape[0]
    kb = wgate_ref.shape[1]
    xcb = xc.astype(BF16)
    ra, ri = [], []
    for p in range(n_blk):
        zz = jnp.dot(xcb[:, kb * p:kb * (p + 1)], wgate_ref[p], preferred_element_type=F32)
        ra.append(zz[:, :kb])
        ri.append(zz[:, kb:])
    r = _sigmoid(jnp.concatenate(ra, axis=1) + ba_ref[...])
    gi = _sigmoid(jnp.concatenate(ri, axis=1) + bi_ref[...])
    neg_lam = -lam_ref[...]
    softplus = jnp.maximum(neg_lam, 0.0) + jnp.log(1.0 + jnp.exp(-jnp.abs(neg_lam)))
    log_a = -LRU_C * r * softplus
    av = jnp.exp(log_a)
    bv = jnp.sqrt(1.0 - jnp.exp(2.0 * log_a)) * (gi * xc)

    for s in (1, 2, 4):
        keep = t_idx >= s
        bv = jnp.where(keep, av * pltpu.roll(bv, s, axis=0) + bv, bv)
        av = jnp.where(keep, av * pltpu.roll(av, s, axis=0), av)
    if decode:
        hs = av * h0_ref[...].reshape(rows, w) + bv
        ho_ref[...] = hs.reshape(bb, tt, w)
    else:
        a_scr[...] = av
        b_scr[...] = bv

        def group(i, h_in):
            sl = pl.ds(pl.multiple_of(i * V7X_SUBLANES, V7X_SUBLANES), V7X_SUBLANES)
            hg = a_scr[sl, :] * h_in + b_scr[sl, :]
            h_scr[sl, :] = hg
            return jnp.broadcast_to(hg[V7X_SUBLANES - 1:, :], (V7X_SUBLANES, w))

        h_last = lax.fori_loop(0, rows // V7X_SUBLANES, group, hc_scr[...])
        hc_scr[...] = h_last
        ho_ref[0] = h_last
        hs = h_scr[...]
    a_ref[...] = (hs * gate).astype(BF16).reshape(bb, tt, w)


def _rec(x, gn, w_in, cw, cb, w_gate, b_a, b_i, lam, state, h0, *, bb, tt):
    b, t, d = x.shape
    w = cw.shape[1]
    rows = bb * tt
    decode = state is not None
    tok = lambda c: pl.BlockSpec((bb, tt, c), lambda i, j: (i, j, 0))
    consts = (gn, w_in, cw, cb, w_gate, b_a, b_i, lam)
    in_specs = [tok(d)] + [_const_spec(a) for a in consts]
    args = [x, *consts]
    scratch = [pltpu.VMEM((rows, w), F32), pltpu.VMEM((rows, w), F32)]
    grp = pl.BlockSpec((bb, V7X_SUBLANES, w), lambda i, j: (i, 0, 0))
    if decode:
        in_specs += [grp, grp]
        args += [state, h0]
    else:
        assert bb == 1
        scratch += [pltpu.VMEM((rows, w), F32), pltpu.VMEM((V7X_SUBLANES, w), F32), pltpu.VMEM((V7X_SUBLANES, w), F32),
                    pltpu.VMEM((V7X_SUBLANES + rows, w), F32)]
    return pl.pallas_call(
        functools.partial(_rec_kernel, decode=decode),
        grid=(b // bb, t // tt), in_specs=in_specs,
        out_specs=[tok(w), grp, grp],
        out_shape=[jax.ShapeDtypeStruct((b, t, w), BF16), jax.ShapeDtypeStruct((b, V7X_SUBLANES, w), F32),
                   jax.ShapeDtypeStruct((b, V7X_SUBLANES, w), F32)],
        scratch_shapes=scratch,
        compiler_params=_params("parallel", "arbitrary"), name="rglru_mixer",
    )(*args)


def _pad_state(buf):
    return jnp.pad(buf, ((0, 0), (V7X_SUBLANES - buf.shape[1], 0), (0, 0)))


def kernel(x_prompt, x_sample, cache_ckv, cache_krope, page_table, state_lru_h, state_lru_conv, state_ffn_conv,
           norm_mix, norm_ffn, attn_w_down, attn_g_q_lora, attn_w_uq, attn_g_kv_lora, attn_g_qn, attn_g_qr,
           attn_g_kn, attn_g_kr, attn_w_uk, attn_w_uv, attn_w_o, rec_w_in, rec_conv_w, rec_conv_b, rec_w_a,
           rec_b_a, rec_w_i, rec_b_i, rec_lambda, rec_w_out, ffn_w_up, ffn_conv_w, ffn_conv_b, ffn_w_down):
    bp, tp, d = x_prompt.shape
    bs, ts, _ = x_sample.shape
    depth = norm_mix.shape[0]
    n_mixers = 2
    kv_lora, n_heads, nope = attn_w_uk.shape[1:]
    vh = attn_w_uv.shape[3]
    rope = attn_g_qr.shape[1]
    q_lora = attn_g_q_lora.shape[1]
    half = rope // 2
    page = cache_ckv.shape[2]
    past = page_table.shape[1] * page
    dff = ffn_conv_w.shape[2]
    lru_w = rec_conv_w.shape[2]
    assert ts == V7X_SUBLANES and rope + nope <= V7X_LANES and n_heads % 2 == 0
    scale = float((nope + rope) ** -0.5) * math.log2(math.e)
    dims = (n_heads, q_lora, kv_lora, nope, rope, vh, scale)
    pad_l = V7X_LANES - rope - nope
    row = lambda v: v.reshape(1, -1).astype(F32)

    def rope_tables(pos):
        inv = ROPE_THETA ** (-jnp.arange(0, rope, 2, dtype=F32) / rope)
        ang = pos.astype(F32)[:, None] * inv[None, :]
        cos, sin = jnp.cos(ang), jnp.sin(ang)
        n = pos.shape[0]
        ones = jnp.ones((n, V7X_LANES - rope), F32)
        zeros = jnp.zeros((n, V7X_LANES - rope), F32)
        z_half = jnp.zeros((n, half), F32)
        return (jnp.concatenate([cos, cos, ones], axis=1), jnp.concatenate([-sin, z_half, zeros], axis=1),
                jnp.concatenate([z_half, sin, zeros], axis=1))

    tabs_p = rope_tables(jnp.arange(tp))
    tabs_s = rope_tables(past + jnp.arange(ts))

    xp, xs = x_prompt, x_sample
    a_p = a_s = w_pre = None
    ckv_p, kr_p, ckv_s, kr_s = [], [], [], []
    lh_p, lh_s, lc_p, lc_s = [], [], [], []
    fc_p, fc_s = [], []
    tt_proj = min(tp, 256)
    bb_proj = min(bs, 256 // ts)
    tt_ffn = min(tp, 512)
    bb_ffn = min(bs, 512 // ts)
    tt_rec = min(tp, 256)
    bb_rec = min(bs, 256 // ts)
    for i in range(depth):
        j = i // n_mixers
        gmix = row(norm_mix[i])
        if i % n_mixers == 0:
            wdn = attn_w_down[j]
            wd = jnp.concatenate([wdn, jnp.zeros((d, V7X_LANES - rope), F32)], axis=1).astype(BF16)
            gkr = jnp.concatenate([attn_g_kr[j], jnp.zeros((V7X_LANES - rope,), F32)]).reshape(1, -1)
            wq = attn_w_uq[j].reshape(q_lora, n_heads, nope + rope)
            wuq = jnp.concatenate([wq[:, :, nope:], wq[:, :, :nope], jnp.zeros((q_lora, n_heads, pad_l), F32)],
                                  axis=2).reshape(q_lora, n_heads * V7X_LANES).astype(BF16)
            gq = jnp.concatenate([attn_g_qr[j], attn_g_qn[j], jnp.zeros((pad_l,), F32)]).reshape(1, -1)
            wk = attn_w_uk[j]
            wuk = jnp.concatenate([jnp.zeros((kv_lora, n_heads, rope), F32), wk,
                                   jnp.zeros((kv_lora, n_heads, pad_l), F32)],
                                  axis=2).reshape(kv_lora, n_heads * V7X_LANES).astype(BF16)
            gk = jnp.concatenate([jnp.zeros((rope,), F32), attn_g_kn[j], jnp.zeros((pad_l,), F32)]).reshape(1, -1)
            wuv = attn_w_uv[j].reshape(kv_lora, n_heads * vh).astype(BF16)
            proj_w = (gmix, wd, row(attn_g_q_lora[j]), row(attn_g_kv_lora[j]), gkr, wuq, gq, wuk, gk, wuv)

            q_p, c_p, krb_p, k_p, v_p = _mla_proj(xp, *proj_w, *tabs_p, bb=1, tt=tt_proj, dims=dims, with_kv=True)
            q_s, c_s, krb_s = _mla_proj(xs, *proj_w, *tabs_s, bb=bb_proj, tt=ts, dims=dims, with_kv=False)
            a_p = _flash(q_p, k_p, v_p, tq=min(tp, 256), n_heads=n_heads, vh=vh)

            wa = jnp.transpose(wk * attn_g_kn[j][None, None, :], (1, 2, 0))
            wa = jnp.concatenate([jnp.zeros((n_heads, rope, kv_lora), F32), wa,
                                  jnp.zeros((n_heads, pad_l, kv_lora), F32)], axis=1).astype(BF16)
            qa = _absorb(q_s.reshape(bs * ts, n_heads * V7X_LANES), wa, tr=min(bs * ts, 256))
            qa = qa.reshape(bs, ts * n_heads, kv_lora)
            qr = q_s.reshape(bs, ts * n_heads, V7X_LANES)
            wk_perm = jnp.transpose(wk, (0, 2, 1)).reshape(kv_lora, nope * n_heads).astype(BF16)
            ctx = _decode_attn(page_table, cache_ckv, cache_krope, j, wk_perm, qa, qr, c_s, krb_s,
                               pg=min(page_table.shape[1], 8), dims=dims)
            wv = attn_w_uv[j]
            zer = jnp.zeros((kv_lora, n_heads // 2, vh), F32)
            wv2 = jnp.concatenate([jnp.concatenate([wv[:, 0::2], zer], axis=2),
                                   jnp.concatenate([zer, wv[:, 1::2]], axis=2)], axis=0)
            wv2 = jnp.transpose(wv2, (1, 0, 2)).astype(BF16)
            a_s = _unabsorb(ctx.reshape(bs * ts, n_heads * kv_lora), wv2, tr=min(bs * ts, 256))
            a_s = a_s.reshape(bs, ts, n_heads * vh)
            w_pre = attn_w_o[j].astype(BF16)
            ckv_p.append(c_p)
            kr_p.append(krb_p[:, :, :rope])
            ckv_s.append(c_s)
            kr_s.append(krb_s[:, :, :rope])
        else:
            n_blk, blk = rec_w_a.shape[1:3]
            assert n_blk % 2 == 0

            def pair(wm):
                z = jnp.zeros((n_blk // 2, blk, blk), F32)
                return jnp.concatenate([jnp.concatenate([wm[0::2], z], axis=2),
                                        jnp.concatenate([z, wm[1::2]], axis=2)], axis=1)

            w_gate = jnp.concatenate([pair(rec_w_a[j]), pair(rec_w_i[j])], axis=2).astype(BF16)
            rec_w = (gmix, rec_w_in[j].astype(BF16), rec_conv_w[j], row(rec_conv_b[j]), w_gate, row(rec_b_a[j]),
                     row(rec_b_i[j]), row(rec_lambda[j]))
            a_p, cv_p, h_p = _rec(xp, *rec_w, None, None, bb=1, tt=tt_rec)
            h0 = jnp.broadcast_to(state_lru_h[j][:, None, :], (bs, ts, lru_w))
            a_s, cv_s, h_s = _rec(xs, *rec_w, _pad_state(state_lru_conv[j]), h0, bb=bb_rec, tt=ts)
            w_pre = rec_w_out[j].astype(BF16)
            kw = rec_conv_w.shape[1] - 1
            lh_p.append(h_p[:, V7X_SUBLANES - 1])
            lh_s.append(h_s[:, V7X_SUBLANES - 1])
            lc_p.append(cv_p[:, V7X_SUBLANES - kw:])
            lc_s.append(cv_s[:, V7X_SUBLANES - kw:])
        wup = ffn_w_up[i]
        ffn_w = (w_pre, row(norm_ffn[i]), wup[:, :dff].astype(BF16), wup[:, dff:].astype(BF16), ffn_conv_w[i],
                 row(ffn_conv_b[i]), ffn_w_down[i].astype(BF16))
        xp, fb_p = _ffn(xp, a_p, *ffn_w, None, bb=1, tt=tt_ffn, fc=min(dff, 512))
        xs, fb_s = _ffn(xs, a_s, *ffn_w, _pad_state(state_ffn_conv[i]), bb=bb_ffn, tt=ts, fc=min(dff, 512))
        kf = ffn_conv_w.shape[1] - 1
        fc_p.append(fb_p[:, V7X_SUBLANES - kf:])
        fc_s.append(fb_s[:, V7X_SUBLANES - kf:])
    return (xp, xs, jnp.stack(ckv_p), jnp.stack(kr_p), jnp.stack(ckv_s), jnp.stack(kr_s),
            jnp.stack(lh_p), jnp.stack(lh_s), jnp.stack(lc_p), jnp.stack(lc_s), jnp.stack(fc_p), jnp.stack(fc_s))
```

```python
import functools
import math

import jax
import jax.numpy as jnp
from jax import lax
from jax.experimental import pallas as pl
from jax.experimental.pallas import tpu as pltpu

EPS = 1e-6
ROPE_THETA = 10000.0
LRU_C = 8.0

V7X_LANES = 128
V7X_SUBLANES = 8
V7X_VMEM_BYTES = 64 * 1024 * 1024
VMEM_LIMIT = V7X_VMEM_BYTES * 7 // 8

F32 = jnp.float32
BF16 = jnp.bfloat16


def _params(*sem):
    return pltpu.CompilerParams(dimension_semantics=sem, vmem_limit_bytes=VMEM_LIMIT)


def _const_spec(a):
    nd = a.ndim
    return pl.BlockSpec(a.shape, lambda *_: (0,) * nd)


def _rms(x, g):
    ms = jnp.mean(x * x, axis=-1, keepdims=True)
    return x * lax.rsqrt(ms + EPS) * g


def _sigmoid(x):
    return 1.0 / (1.0 + jnp.exp(-x))


def _rope(y, cos, sin_lo, sin_hi, half):
    return y * cos + pltpu.roll(y, V7X_LANES - half, axis=1) * sin_lo + pltpu.roll(y, half, axis=1) * sin_hi


def _tile_rows(tab_ref, bb):
    t = tab_ref[...]
    if bb == 1:
        return t
    return jnp.broadcast_to(t[None], (bb,) + t.shape).reshape(bb * t.shape[0], t.shape[1])


def _key_norm_factor(cb, wk_ref, n_heads, nope):
    k = jnp.dot(cb, wk_ref[...], preferred_element_type=F32)
    ksq = k * k
    part = ksq[:, :V7X_LANES]
    for i in range(1, wk_ref.shape[1] // V7X_LANES):
        part = part + ksq[:, V7X_LANES * i:V7X_LANES * (i + 1)]
    shift = V7X_LANES // 2
    while shift >= n_heads:
        part = part + pltpu.roll(part, shift, axis=1)
        shift //= 2
    return lax.rsqrt(part / nope + EPS)


def _mla_proj_kernel(x_ref, gmix_ref, wd_ref, gql_ref, gkv_ref, gkr_ref, wuq_ref, gq_ref, wuk_ref, gk_ref, wuv_ref,
                     cos_ref, sinlo_ref, sinhi_ref, *out_refs, n_heads, q_lora, kv_lora, nope, rope, scale, with_kv):
    if with_kv:
        q_ref, ckv_ref, kr_ref, k_ref, v_ref = out_refs
    else:
        q_ref, ckv_ref, kr_ref, rs_ref = out_refs
    bb, tt, d = x_ref.shape
    rows = bb * tt
    half = rope // 2
    xn = _rms(x_ref[...].reshape(rows, d), gmix_ref[...])
    dd = jnp.dot(xn.astype(BF16), wd_ref[...], preferred_element_type=F32)
    cq = _rms(dd[:, :q_lora], gql_ref[...])
    ckv = _rms(dd[:, q_lora:q_lora + kv_lora], gkv_ref[...])
    kr = dd[:, q_lora + kv_lora:]
    cos = _tile_rows(cos_ref, bb)
    sin_lo = _tile_rows(sinlo_ref, bb)
    sin_hi = _tile_rows(sinhi_ref, bb)
    lane = lax.broadcasted_iota(jnp.int32, (1, V7X_LANES), 1)
    is_rope = lane < rope

    kr_ss = jnp.sum(kr * kr, axis=-1, keepdims=True)
    krr = _rope(kr * lax.rsqrt(kr_ss / rope + EPS) * gkr_ref[...], cos, sin_lo, sin_hi, half)
    kr_ref[...] = krr.reshape(bb, tt, V7X_LANES)
    ckv_ref[...] = ckv.reshape(bb, tt, kv_lora)

    q = jnp.dot(cq.astype(BF16), wuq_ref[...], preferred_element_type=F32)
    ckv_b = ckv.astype(BF16)
    if with_kv:
        kk = jnp.dot(ckv_b, wuk_ref[...], preferred_element_type=F32)
        v_ref[...] = jnp.dot(ckv_b, wuv_ref[...], preferred_element_type=F32).astype(BF16).reshape(v_ref.shape)
    else:
        rs_ref[...] = _key_norm_factor(ckv_b, wuk_ref, n_heads, nope).reshape(bb, tt, V7X_LANES)
    for h in range(n_heads):
        sl = slice(V7X_LANES * h, V7X_LANES * (h + 1))
        qb = q[:, sl]
        sq = qb * qb
        ss_r = jnp.sum(jnp.where(is_rope, sq, 0.0), axis=-1, keepdims=True)
        ss_n = jnp.sum(jnp.where(is_rope, 0.0, sq), axis=-1, keepdims=True)
        rs = jnp.where(is_rope, lax.rsqrt(ss_r / rope + EPS), lax.rsqrt(ss_n / nope + EPS))
        y = _rope(qb * rs * gq_ref[...], cos, sin_lo, sin_hi, half)
        q_ref[:, :, sl] = (y * scale).astype(BF16).reshape(bb, tt, V7X_LANES)
        if with_kv:
            kb = kk[:, sl]
            ss_k = jnp.sum(kb * kb, axis=-1, keepdims=True)
            kn = kb * lax.rsqrt(ss_k / nope + EPS) * gk_ref[...]
            k_ref[:, :, sl] = (kn + krr).astype(BF16).reshape(bb, tt, V7X_LANES)


def _mla_proj(x, gmix, wd, gql, gkv, gkr, wuq, gq, wuk, gk, wuv, cos, sin_lo, sin_hi, *, bb, tt, dims, with_kv):
    n_heads, q_lora, kv_lora, nope, rope, vh, scale = dims
    b, t, d = x.shape
    hl = n_heads * V7X_LANES
    tok = lambda c: pl.BlockSpec((bb, tt, c), lambda i, j: (i, j, 0))
    tab = pl.BlockSpec((tt, V7X_LANES), lambda i, j: (j, 0))
    out_shape = [jax.ShapeDtypeStruct((b, t, hl), BF16), jax.ShapeDtypeStruct((b, t, kv_lora), F32),
                 jax.ShapeDtypeStruct((b, t, V7X_LANES), F32)]
    out_specs = [tok(hl), tok(kv_lora), tok(V7X_LANES)]
    if with_kv:
        out_shape += [jax.ShapeDtypeStruct((b, t, hl), BF16), jax.ShapeDtypeStruct((b, t, n_heads * vh), BF16)]
        out_specs += [tok(hl), tok(n_heads * vh)]
    else:
        out_shape.append(jax.ShapeDtypeStruct((b, t, V7X_LANES), F32))
        out_specs.append(tok(V7X_LANES))
    consts = (gmix, wd, gql, gkv, gkr, wuq, gq, wuk, gk, wuv)
    return pl.pallas_call(
        functools.partial(_mla_proj_kernel, n_heads=n_heads, q_lora=q_lora, kv_lora=kv_lora, nope=nope, rope=rope,
                          scale=scale, with_kv=with_kv),
        grid=(b // bb, t // tt),
        in_specs=[tok(d)] + [_const_spec(a) for a in consts] + [tab, tab, tab],
        out_specs=out_specs, out_shape=out_shape,
        compiler_params=_params("parallel", "parallel"), name="mla_proj",
    )(x, *consts, cos, sin_lo, sin_hi)


def _attn_block(q, k_ref, v_ref, lanes, vlanes, past, tq):
    dn = (((1,), (1,)), ((), ()))
    row = lax.broadcasted_iota(jnp.int32, (tq, tq), 0)
    col = lax.broadcasted_iota(jnp.int32, (tq, tq), 1)
    s_d = lax.dot_general(q, k_ref[0, past:past + tq, lanes], dn, preferred_element_type=F32)
    s_d = jnp.where(col <= row, s_d, -jnp.inf)
    m = jnp.max(s_d, axis=1, keepdims=True)
    if past:
        s_p = lax.dot_general(q, k_ref[0, :past, lanes], dn, preferred_element_type=F32)
        m = jnp.maximum(m, jnp.max(s_p, axis=1, keepdims=True))
    p_d = jnp.exp2(s_d - m)
    l = jnp.sum(p_d, axis=1, keepdims=True)
    acc = jnp.dot(p_d.astype(BF16), v_ref[0, past:past + tq, vlanes], preferred_element_type=F32)
    if past:
        p_p = jnp.exp2(s_p - m)
        l = l + jnp.sum(p_p, axis=1, keepdims=True)
        acc = acc + jnp.dot(p_p.astype(BF16), v_ref[0, :past, vlanes], preferred_element_type=F32)
    return acc / l


def _flash_kernel(q_ref, k_ref, v_ref, o_ref, *, tq, vh, n_q):
    qi = pl.program_id(2)
    for n in range(n_q):
        @pl.when(qi == n)
        def _(n=n):
            outs = []
            for hh in range(2):
                lanes = slice(V7X_LANES * hh, V7X_LANES * (hh + 1))
                outs.append(_attn_block(q_ref[0, :, lanes], k_ref, v_ref, lanes, slice(vh * hh, vh * (hh + 1)),
                                        n * tq, tq))
            o_ref[0] = jnp.concatenate(outs, axis=1).astype(BF16)


def _flash(q, k, v, *, tq, n_heads, vh):
    b, t, _ = q.shape
    return pl.pallas_call(
        functools.partial(_flash_kernel, tq=tq, vh=vh, n_q=t // tq),
        grid=(b, n_heads // 2, t // tq),
        in_specs=[pl.BlockSpec((1, tq, 2 * V7X_LANES), lambda i, h, j: (i, j, h)),
                  pl.BlockSpec((1, t, 2 * V7X_LANES), lambda i, h, j: (i, 0, h)),
                  pl.BlockSpec((1, t, 2 * vh), lambda i, h, j: (i, 0, h))],
        out_specs=pl.BlockSpec((1, tq, 2 * vh), lambda i, h, j: (i, j, h)),
        out_shape=jax.ShapeDtypeStruct((b, t, n_heads * vh), BF16),
        compiler_params=_params("parallel", "parallel", "arbitrary"), name="mla_prompt_attn",
    )(q, k, v)


def _absorb_kernel(q_ref, wa_ref, qa_ref, *, n_heads, kv_lora):
    for h in range(n_heads):
        qh = q_ref[:, V7X_LANES * h:V7X_LANES * (h + 1)]
        qa_ref[:, kv_lora * h:kv_lora * (h + 1)] = jnp.dot(qh, wa_ref[h], preferred_element_type=F32).astype(BF16)


def _absorb(q2d, wa, *, tr):
    r = q2d.shape[0]
    n_heads, _, kv_lora = wa.shape
    return pl.pallas_call(
        functools.partial(_absorb_kernel, n_heads=n_heads, kv_lora=kv_lora),
        grid=(r // tr,),
        in_specs=[pl.BlockSpec((tr, n_heads * V7X_LANES), lambda i: (i, 0)), _const_spec(wa)],
        out_specs=pl.BlockSpec((tr, n_heads * kv_lora), lambda i: (i, 0)),
        out_shape=jax.ShapeDtypeStruct((r, n_heads * kv_lora), BF16),
        compiler_params=_params("parallel"), name="mla_absorb_q",
    )(q2d, wa)


def _decode_attn_kernel(pt_ref, ckv_hbm, krt_hbm, wk_ref, qa_ref, qr_ref, cnew_ref, krnew_ref, rsnew_ref, o_ref,
                        cbuf, krbuf, cb_scr, s_scr, m_scr, l_scr, acc_scr, csem, ksem,
                        *, layer, pg, n_groups, n_heads, nope, rope):
    b = pl.program_id(0)
    page = ckv_hbm.shape[2]
    kv_lora = ckv_hbm.shape[3]
    dn_last = (((1,), (1,)), ((), ()))
    dn_first = (((0,), (0,)), ((), ()))
    qa = qa_ref[...]
    qr = qr_ref[:, :rope]

    def copies(seq, grp, slot):
        out = []
        for s in range(pg):
            pid = pt_ref[seq, grp * pg + s]
            rows = pl.ds(s * page, page)
            out.append(pltpu.make_async_copy(ckv_hbm.at[layer, pid], cbuf.at[slot, rows, :], csem.at[slot]))
            out.append(pltpu.make_async_copy(krt_hbm.at[layer, pid], krbuf.at[slot, :, rows], ksem.at[slot]))
        return out

    def fetch(seq, grp, slot):
        for cp in copies(seq, grp, slot):
            cp.start()

    def wait(seq, grp, slot):
        for cp in copies(seq, grp, slot):
            cp.wait()

    @pl.when(b == 0)
    def _():
        fetch(0, 0, 0)

    n_new = cnew_ref.shape[0]
    pad = 2 * V7X_SUBLANES - n_new
    c_new = jnp.concatenate([cnew_ref[...], jnp.zeros((pad, kv_lora), F32)], axis=0).astype(BF16)
    kr_new = jnp.concatenate([krnew_ref[:, :rope], jnp.zeros((pad, rope), F32)], axis=0).astype(BF16)
    rs_new = jnp.concatenate([rsnew_ref[...], jnp.ones((pad, V7X_LANES), F32)], axis=0)
    s = lax.dot_general(c_new, qa, dn_last, preferred_element_type=F32) * rs_new
    s = s + lax.dot_general(kr_new, qr, dn_last, preferred_element_type=F32)
    key = lax.broadcasted_iota(jnp.int32, s.shape, 0)
    qry = lax.broadcasted_iota(jnp.int32, s.shape, 1) // n_heads
    s = jnp.where(key <= qry, s, -jnp.inf)
    m0 = jnp.max(s, axis=0, keepdims=True)
    p = jnp.exp2(s - m0)
    m_scr[...] = m0
    l_scr[...] = jnp.sum(p, axis=0, keepdims=True)
    acc_scr[...] = lax.dot_general(c_new, p.astype(BF16), dn_first, preferred_element_type=F32)

    def stage_a(slot):
        cb = cbuf[slot].astype(BF16)
        cb_scr[slot] = cb
        rs = _key_norm_factor(cb, wk_ref, n_heads, nope)
        s = lax.dot_general(cb, qa, dn_last, preferred_element_type=F32) * rs
        krb = krbuf[slot].astype(BF16)
        s_scr[slot] = s + lax.dot_general(krb, qr, (((0,), (1,)), ((), ())), preferred_element_type=F32)

    def stage_b(slot):
        s = s_scr[slot]
        m_old = m_scr[...]
        m_new = jnp.maximum(m_old, jnp.max(s, axis=0, keepdims=True))
        p = jnp.exp2(s - m_new)
        alpha = jnp.exp2(m_old - m_new)
        l_scr[...] = alpha * l_scr[...] + jnp.sum(p, axis=0, keepdims=True)
        pv = lax.dot_general(cb_scr[slot], p.astype(BF16), dn_first, preferred_element_type=F32)
        acc_scr[...] = alpha * acc_scr[...] + pv
        m_scr[...] = m_new

    for i in range(n_groups + 1):
        slot = i % 2
        if i + 1 < n_groups:
            fetch(b, i + 1, 1 - slot)
        elif i + 1 == n_groups:
            @pl.when(b + 1 < pl.num_programs(0))
            def _():
                fetch(b + 1, 0, 0)
        if i < n_groups:
            wait(b, i, slot)
            stage_a(slot)
        if i >= 1:
            stage_b(1 - slot)
    o_ref[...] = (acc_scr[...] / l_scr[...]).T


def _decode_attn(page_table, cache_ckv, cache_krope, layer, wk_perm, qa, qr, c_new, kr_new, rs_new, *, pg, dims):
    n_heads, _, kv_lora, nope, rope, _, _ = dims
    b, n_pages = page_table.shape
    page = cache_ckv.shape[2]
    n_new = c_new.shape[1]
    rows = n_new * n_heads
    n_groups = n_pages // pg
    keys = pg * page
    assert rows == V7X_LANES and n_pages % pg == 0 and n_groups % 2 == 0 and n_new <= 2 * V7X_SUBLANES
    krope_t = jnp.swapaxes(cache_krope, 2, 3)
    per_seq = lambda r, c: pl.BlockSpec((None, r, c), lambda i, pt: (i, 0, 0))
    hbm = pl.BlockSpec(memory_space=pl.ANY)
    grid_spec = pltpu.PrefetchScalarGridSpec(
        num_scalar_prefetch=1, grid=(b,),
        in_specs=[hbm, hbm, pl.BlockSpec(wk_perm.shape, lambda i, pt: (0, 0)),
                  per_seq(rows, kv_lora), per_seq(rows, V7X_LANES),
                  per_seq(n_new, kv_lora), per_seq(n_new, V7X_LANES), per_seq(n_new, V7X_LANES)],
        out_specs=per_seq(rows, kv_lora),
        scratch_shapes=[pltpu.VMEM((2, keys, kv_lora), F32), pltpu.VMEM((2, rope, keys), F32),
                        pltpu.VMEM((2, keys, kv_lora), BF16), pltpu.VMEM((2, keys, V7X_LANES), F32),
                        pltpu.VMEM((1, V7X_LANES), F32), pltpu.VMEM((1, V7X_LANES), F32),
                        pltpu.VMEM((kv_lora, V7X_LANES), F32),
                        pltpu.SemaphoreType.DMA((2,)), pltpu.SemaphoreType.DMA((2,))])
    return pl.pallas_call(
        functools.partial(_decode_attn_kernel, layer=layer, pg=pg, n_groups=n_groups, n_heads=n_heads, nope=nope,
                          rope=rope),
        grid_spec=grid_spec, out_shape=jax.ShapeDtypeStruct((b, rows, kv_lora), F32),
        compiler_params=_params("arbitrary"), name="mla_decode_attn",
    )(page_table, cache_ckv, krope_t, wk_perm, qa, qr, c_new, kr_new, rs_new)


def _unabsorb_kernel(ctx_ref, wv_ref, o_ref, *, n_pairs, kv_lora):
    for p in range(n_pairs):
        x = ctx_ref[:, 2 * kv_lora * p:2 * kv_lora * (p + 1)].astype(BF16)
        o_ref[:, V7X_LANES * p:V7X_LANES * (p + 1)] = jnp.dot(x, wv_ref[p], preferred_element_type=F32).astype(BF16)


def _unabsorb(ctx2d, wv2, *, tr):
    r = ctx2d.shape[0]
    n_pairs, k2, _ = wv2.shape
    return pl.pallas_call(
        functools.partial(_unabsorb_kernel, n_pairs=n_pairs, kv_lora=k2 // 2),
        grid=(r // tr,),
        in_specs=[pl.BlockSpec((tr, n_pairs * k2), lambda i: (i, 0)), _const_spec(wv2)],
        out_specs=pl.BlockSpec((tr, n_pairs * V7X_LANES), lambda i: (i, 0)),
        out_shape=jax.ShapeDtypeStruct((r, n_pairs * V7X_LANES), BF16),
        compiler_params=_params("parallel"), name="mla_unabsorb_v",
    )(ctx2d, wv2)


def _conv_prompt(g, w, b, buf_ref, carry):
    rows = g.shape[0]
    width = w.shape[0]
    buf_ref[0:V7X_SUBLANES, :] = carry
    buf_ref[V7X_SUBLANES:, :] = g
    y = b + g * w[width - 1:width, :]
    for s in range(1, width):
        y = y + buf_ref[V7X_SUBLANES - s:V7X_SUBLANES - s + rows, :] * w[width - 1 - s:width - s, :]
    return y


def _conv_decode(g, w, b, prev, t_idx):
    width = w.shape[0]
    rows = g.shape[0]
    y = b + g * w[width - 1:width, :]
    for s in range(1, width):
        sh = jnp.where(t_idx >= s, pltpu.roll(g, s, axis=0), pltpu.roll(prev, rows - V7X_SUBLANES + s, axis=0))
        y = y + sh * w[width - 1 - s:width - s, :]
    return y


def _ffn_kernel(*refs, decode, fc):
    if decode:
        x_ref, a_ref, wpre_ref, gn_ref, wg_ref, wu_ref, cw_ref, cb_ref, wd_ref, st_ref, y_ref, so_ref = refs
    else:
        (x_ref, a_ref, wpre_ref, gn_ref, wg_ref, wu_ref, cw_ref, cb_ref, wd_ref,
         y_ref, so_ref, carry_scr, buf_scr) = refs
    bb, tt, d = x_ref.shape
    rows = bb * tt
    dff = wg_ref.shape[1]
    a = a_ref[...].reshape(rows, a_ref.shape[2])
    x1 = x_ref[...].reshape(rows, d) + jnp.dot(a, wpre_ref[...], preferred_element_type=F32)
    xn = _rms(x1, gn_ref[...]).astype(BF16)
    if decode:
        t_idx = lax.broadcasted_iota(jnp.int32, (rows, fc), 0) % tt
    else:
        @pl.when(pl.program_id(1) == 0)
        def _():
            carry_scr[...] = jnp.zeros(carry_scr.shape, F32)

    acc = x1
    for c in range(dff // fc):
        cols = slice(c * fc, (c + 1) * fc)
        g = jnp.dot(xn, wg_ref[:, cols], preferred_element_type=F32)
        u = jnp.dot(xn, wu_ref[:, cols], preferred_element_type=F32)
        if decode:
            gc = _conv_decode(g, cw_ref[:, cols], cb_ref[:, cols], st_ref[:, :, cols].reshape(rows, fc), t_idx)
            so_ref[:, :, cols] = g.reshape(bb, tt, fc)
        else:
            gc = _conv_prompt(g, cw_ref[:, cols], cb_ref[:, cols], buf_scr.at[c], carry_scr[:, cols])
            last = g[rows - V7X_SUBLANES:, :]
            carry_scr[:, cols] = last
            so_ref[0, :, cols] = last
        h = (gc * _sigmoid(gc) * u).astype(BF16)
        acc = acc + jnp.dot(h, wd_ref[cols, :], preferred_element_type=F32)
    y_ref[...] = acc.reshape(bb, tt, d)


def _resident_spec(a):
    nd = a.ndim
    return pl.BlockSpec(a.shape, lambda *_: (0,) * nd, pipeline_mode=pl.Buffered(1))


def _ffn(x, a, w_pre, gn, wg, wu, cw, cb, wd, state, *, bb, tt, fc):
    b, t, d = x.shape
    da = a.shape[2]
    dff = wg.shape[1]
    rows = bb * tt
    decode = state is not None
    tok = lambda c: pl.BlockSpec((bb, tt, c), lambda i, j: (i, j, 0))
    weights = [w_pre, gn, wg, wu, cw, cb, wd]
    in_specs = [tok(d), tok(da)] + [_resident_spec(w) for w in weights]
    args = [x, a] + weights
    scratch = []
    if decode:
        in_specs.append(pl.BlockSpec((bb, tt, dff), lambda i, j: (i, 0, 0)))
        args.append(state)
    else:
        assert bb == 1
        scratch = [pltpu.VMEM((V7X_SUBLANES, dff), F32), pltpu.VMEM((dff // fc, V7X_SUBLANES + rows, fc), F32)]
    so_spec = pl.BlockSpec((bb, None, V7X_SUBLANES, dff), lambda i, j: (i, j, 0, 0))
    y, so = pl.pallas_call(
        functools.partial(_ffn_kernel, decode=decode, fc=fc),
        grid=(b // bb, t // tt), in_specs=in_specs,
        out_specs=[tok(d), so_spec],
        out_shape=[jax.ShapeDtypeStruct((b, t, d), F32),
                   jax.ShapeDtypeStruct((b, t // tt, V7X_SUBLANES, dff), F32)],
        scratch_shapes=scratch,
        compiler_params=_params("parallel", "arbitrary"), name="conv_ffn",
    )(*args)
    return y, so[:, -1]


def _rec_kernel(*refs, decode):
    if decode:
        (x_ref, gn_ref, win_ref, cw_ref, cb_ref, wgate_ref, ba_ref, bi_ref, lam_ref, st_ref, h0_ref,
         a_ref, co_ref, ho_ref, a_scr, b_scr) = refs
    else:
        (x_ref, gn_ref, win_ref, cw_ref, cb_ref, wgate_ref, ba_ref, bi_ref, lam_ref,
         a_ref, co_ref, ho_ref, a_scr, b_scr, h_scr, carry_scr, hc_scr, buf_scr) = refs
    bb, tt, d = x_ref.shape
    rows = bb * tt
    w = cw_ref.shape[1]
    t = pl.program_id(1)
    xn = _rms(x_ref[...].reshape(rows, d), gn_ref[...]).astype(BF16)
    z = jnp.dot(xn, win_ref[...], preferred_element_type=F32)
    xr = z[:, :w]
    gin = z[:, w:]
    gate = 0.5 * gin * (1.0 + jnp.tanh(0.7978845608028654 * (gin + 0.044715 * (gin * gin * gin))))
    t_idx = lax.broadcasted_iota(jnp.int32, (rows, w), 0) % V7X_SUBLANES
    if decode:
        xc = _conv_decode(xr, cw_ref[...], cb_ref[...], st_ref[...].reshape(rows, w), t_idx)
        co_ref[...] = xr.reshape(bb, tt, w)
    else:
        @pl.when(t == 0)
        def _():
            carry_scr[...] = jnp.zeros(carry_scr.shape, F32)
            hc_scr[...] = jnp.zeros(hc_scr.shape, F32)

        xc = _conv_prompt(xr, cw_ref[...], cb_ref[...], buf_scr, carry_scr[...])
        last = xr[rows - V7X_SUBLANES:, :]
        carry_scr[...] = last
        co_ref[0] = last

    n_blk = wgate_ref.shape[0]
    kb = wgate_ref.shape[1]
    xcb = xc.astype(BF16)
    ra, ri = [], []
    for p in range(n_blk):
        zz = jnp.dot(xcb[:, kb * p:kb * (p + 1)], wgate_ref[p], preferred_element_type=F32)
        ra.append(zz[:, :kb])
        ri.append(zz[:, kb:])
    r = _sigmoid(jnp.concatenate(ra, axis=1) + ba_ref[...])
    gi = _sigmoid(jnp.concatenate(ri, axis=1) + bi_ref[...])
    neg_lam = -lam_ref[...]
    softplus = jnp.maximum(neg_lam, 0.0) + jnp.log(1.0 + jnp.exp(-jnp.abs(neg_lam)))
    log_a = -LRU_C * r * softplus
    av = jnp.exp(log_a)
    bv = jnp.sqrt(1.0 - jnp.exp(2.0 * log_a)) * (gi * xc)

    for s in (1, 2, 4):
        keep = t_idx >= s
        bv = jnp.where(keep, av * pltpu.roll(bv, s, axis=0) + bv, bv)
        av = jnp.where(keep, av * pltpu.roll(av, s, axis=0), av)
    if decode:
        hs = av * h0_ref[...].reshape(rows, w) + bv
        ho_ref[...] = hs.reshape(bb, tt, w)
    else:
        a_scr[...] = av
        b_scr[...] = bv

        def group(i, h_in):
            sl = pl.ds(pl.multiple_of(i * V7X_SUBLANES, V7X_SUBLANES), V7X_SUBLANES)
            hg = a_scr[sl, :] * h_in + b_scr[sl, :]
            h_scr[sl, :] = hg
            return jnp.broadcast_to(hg[V7X_SUBLANES - 1:, :], (V7X_SUBLANES, w))

        h_last = lax.fori_loop(0, rows // V7X_SUBLANES, group, hc_scr[...])
        hc_scr[...] = h_last
        ho_ref[0] = h_last
        hs = h_scr[...]
    a_ref[...] = (hs * gate).astype(BF16).reshape(bb, tt, w)


def _rec(x, gn, w_in, cw, cb, w_gate, b_a, b_i, lam, state, h0, *, bb, tt):
    b, t, d = x.shape
    w = cw.shape[1]
    rows = bb * tt
    decode = state is not None
    tok = lambda c: pl.BlockSpec((bb, tt, c), lambda i, j: (i, j, 0))
    consts = (gn, w_in, cw, cb, w_gate, b_a, b_i, lam)
    in_specs = [tok(d)] + [_const_spec(a) for a in consts]
    args = [x, *consts]
    scratch = [pltpu.VMEM((rows, w), F32), pltpu.VMEM((rows, w), F32)]
    grp = pl.BlockSpec((bb, V7X_SUBLANES, w), lambda i, j: (i, 0, 0))
    if decode:
        in_specs += [grp, grp]
        args += [state, h0]
    else:
        assert bb == 1
        scratch += [pltpu.VMEM((rows, w), F32), pltpu.VMEM((V7X_SUBLANES, w), F32), pltpu.VMEM((V7X_SUBLANES, w), F32),
                    pltpu.VMEM((V7X_SUBLANES + rows, w), F32)]
    return pl.pallas_call(
        functools.partial(_rec_kernel, decode=decode),
        grid=(b // bb, t // tt), in_specs=in_specs,
        out_specs=[tok(w), grp, grp],
        out_shape=[jax.ShapeDtypeStruct((b, t, w), BF16), jax.ShapeDtypeStruct((b, V7X_SUBLANES, w), F32),
                   jax.ShapeDtypeStruct((b, V7X_SUBLANES, w), F32)],
        scratch_shapes=scratch,
        compiler_params=_params("parallel", "arbitrary"), name="rglru_mixer",
    )(*args)


def _pad_state(buf):
    return jnp.pad(buf, ((0, 0), (V7X_SUBLANES - buf.shape[1], 0), (0, 0)))


def kernel(x_prompt, x_sample, cache_ckv, cache_krope, page_table, state_lru_h, state_lru_conv, state_ffn_conv,
           norm_mix, norm_ffn, attn_w_down, attn_g_q_lora, attn_w_uq, attn_g_kv_lora, attn_g_qn, attn_g_qr,
           attn_g_kn, attn_g_kr, attn_w_uk, attn_w_uv, attn_w_o, rec_w_in, rec_conv_w, rec_conv_b, rec_w_a,
           rec_b_a, rec_w_i, rec_b_i, rec_lambda, rec_w_out, ffn_w_up, ffn_conv_w, ffn_conv_b, ffn_w_down):
    bp, tp, d = x_prompt.shape
    bs, ts, _ = x_sample.shape
    depth = norm_mix.shape[0]
    n_mixers = 2
    kv_lora, n_heads, nope = attn_w_uk.shape[1:]
    vh = attn_w_uv.shape[3]
    rope = attn_g_qr.shape[1]
    q_lora = attn_g_q_lora.shape[1]
    half = rope // 2
    page = cache_ckv.shape[2]
    past = page_table.shape[1] * page
    dff = ffn_conv_w.shape[2]
    lru_w = rec_conv_w.shape[2]
    assert ts == V7X_SUBLANES and rope + nope <= V7X_LANES and n_heads % 2 == 0
    scale = float((nope + rope) ** -0.5) * math.log2(math.e)
    dims = (n_heads, q_lora, kv_lora, nope, rope, vh, scale)
    pad_l = V7X_LANES - rope - nope
    row = lambda v: v.reshape(1, -1).astype(F32)

    def rope_tables(pos):
        inv = ROPE_THETA ** (-jnp.arange(0, rope, 2, dtype=F32) / rope)
        ang = pos.astype(F32)[:, None] * inv[None, :]
        cos, sin = jnp.cos(ang), jnp.sin(ang)
        n = pos.shape[0]
        ones = jnp.ones((n, V7X_LANES - rope), F32)
        zeros = jnp.zeros((n, V7X_LANES - rope), F32)
        z_half = jnp.zeros((n, half), F32)
        return (jnp.concatenate([cos, cos, ones], axis=1), jnp.concatenate([-sin, z_half, zeros], axis=1),
                jnp.concatenate([z_half, sin, zeros], axis=1))

    tabs_p = rope_tables(jnp.arange(tp))
    tabs_s = rope_tables(past + jnp.arange(ts))

    xp, xs = x_prompt, x_sample
    a_p = a_s = w_pre = None
    ckv_p, kr_p, ckv_s, kr_s = [], [], [], []
    lh_p, lh_s, lc_p, lc_s = [], [], [], []
    fc_p, fc_s = [], []
    tt_proj = min(tp, 256)
    bb_proj = min(bs, 256 // ts)
    tt_ffn = min(tp, 512)
    bb_ffn = min(bs, 256 // ts)
    tt_rec = min(tp, 256)
    bb_rec = min(bs, 256 // ts)
    for i in range(depth):
        j = i // n_mixers
        gmix = row(norm_mix[i])
        if i % n_mixers == 0:
            wdn = attn_w_down[j]
            wd = jnp.concatenate([wdn, jnp.zeros((d, V7X_LANES - rope), F32)], axis=1).astype(BF16)
            gkr = jnp.concatenate([attn_g_kr[j], jnp.zeros((V7X_LANES - rope,), F32)]).reshape(1, -1)
            wq = attn_w_uq[j].reshape(q_lora, n_heads, nope + rope)
            wuq = jnp.concatenate([wq[:, :, nope:], wq[:, :, :nope], jnp.zeros((q_lora, n_heads, pad_l), F32)],
                                  axis=2).reshape(q_lora, n_heads * V7X_LANES).astype(BF16)
            gq = jnp.concatenate([attn_g_qr[j], attn_g_qn[j], jnp.zeros((pad_l,), F32)]).reshape(1, -1)
            wk = attn_w_uk[j]
            wuk = jnp.concatenate([jnp.zeros((kv_lora, n_heads, rope), F32), wk,
                                   jnp.zeros((kv_lora, n_heads, pad_l), F32)],
                                  axis=2).reshape(kv_lora, n_heads * V7X_LANES).astype(BF16)
            gk = jnp.concatenate([jnp.zeros((rope,), F32), attn_g_kn[j], jnp.zeros((pad_l,), F32)]).reshape(1, -1)
            wuv = attn_w_uv[j].reshape(kv_lora, n_heads * vh).astype(BF16)
            proj_w = (gmix, wd, row(attn_g_q_lora[j]), row(attn_g_kv_lora[j]), gkr, wuq, gq, wuk, gk, wuv)

            q_p, c_p, krb_p, k_p, v_p = _mla_proj(xp, *proj_w, *tabs_p, bb=1, tt=tt_proj, dims=dims, with_kv=True)
            wk_perm = jnp.transpose(wk, (0, 2, 1)).reshape(kv_lora, nope * n_heads).astype(BF16)
            proj_s = proj_w[:7] + (wk_perm,) + proj_w[8:]
            q_s, c_s, krb_s, rs_s = _mla_proj(xs, *proj_s, *tabs_s, bb=bb_proj, tt=ts, dims=dims, with_kv=False)
            a_p = _flash(q_p, k_p, v_p, tq=min(tp, 256), n_heads=n_heads, vh=vh)

            wa = jnp.transpose(wk * attn_g_kn[j][None, None, :], (1, 2, 0))
            wa = jnp.concatenate([jnp.zeros((n_heads, rope, kv_lora), F32), wa,
                                  jnp.zeros((n_heads, pad_l, kv_lora), F32)], axis=1).astype(BF16)
            qa = _absorb(q_s.reshape(bs * ts, n_heads * V7X_LANES), wa, tr=min(bs * ts, 256))
            qa = qa.reshape(bs, ts * n_heads, kv_lora)
            qr = q_s.reshape(bs, ts * n_heads, V7X_LANES)
            ctx = _decode_attn(page_table, cache_ckv, cache_krope, j, wk_perm, qa, qr, c_s, krb_s, rs_s,
                               pg=min(page_table.shape[1], 8), dims=dims)
            wv = attn_w_uv[j]
            zer = jnp.zeros((kv_lora, n_heads // 2, vh), F32)
            wv2 = jnp.concatenate([jnp.concatenate([wv[:, 0::2], zer], axis=2),
                                   jnp.concatenate([zer, wv[:, 1::2]], axis=2)], axis=0)
            wv2 = jnp.transpose(wv2, (1, 0, 2)).astype(BF16)
            a_s = _unabsorb(ctx.reshape(bs * ts, n_heads * kv_lora), wv2, tr=min(bs * ts, 256))
            a_s = a_s.reshape(bs, ts, n_heads * vh)
            w_pre = attn_w_o[j].astype(BF16)
            ckv_p.append(c_p)
            kr_p.append(krb_p[:, :, :rope])
            ckv_s.append(c_s)
            kr_s.append(krb_s[:, :, :rope])
        else:
            n_blk, blk = rec_w_a.shape[1:3]
            assert n_blk % 2 == 0

            def pair(wm):
                z = jnp.zeros((n_blk // 2, blk, blk), F32)
                return jnp.concatenate([jnp.concatenate([wm[0::2], z], axis=2),
                                        jnp.concatenate([z, wm[1::2]], axis=2)], axis=1)

            w_gate = jnp.concatenate([pair(rec_w_a[j]), pair(rec_w_i[j])], axis=2).astype(BF16)
            rec_w = (gmix, rec_w_in[j].astype(BF16), rec_conv_w[j], row(rec_conv_b[j]), w_gate, row(rec_b_a[j]),
                     row(rec_b_i[j]), row(rec_lambda[j]))
            a_p, cv_p, h_p = _rec(xp, *rec_w, None, None, bb=1, tt=tt_rec)
            h0 = jnp.broadcast_to(state_lru_h[j][:, None, :], (bs, ts, lru_w))
            a_s, cv_s, h_s = _rec(xs, *rec_w, _pad_state(state_lru_conv[j]), h0, bb=bb_rec, tt=ts)
            w_pre = rec_w_out[j].astype(BF16)
            kw = rec_conv_w.shape[1] - 1
            lh_p.append(h_p[:, V7X_SUBLANES - 1])
            lh_s.append(h_s[:, V7X_SUBLANES - 1])
            lc_p.append(cv_p[:, V7X_SUBLANES - kw:])
            lc_s.append(cv_s[:, V7X_SUBLANES - kw:])
        wup = ffn_w_up[i]
        ffn_w = (w_pre, row(norm_ffn[i]), wup[:, :dff].astype(BF16), wup[:, dff:].astype(BF16), ffn_conv_w[i],
                 row(ffn_conv_b[i]), ffn_w_down[i].astype(BF16))
        xp, fb_p = _ffn(xp, a_p, *ffn_w, None, bb=1, tt=tt_ffn, fc=min(dff, 512))
        xs, fb_s = _ffn(xs, a_s, *ffn_w, _pad_state(state_ffn_conv[i]), bb=bb_ffn, tt=ts, fc=min(dff, 512))
        kf = ffn_conv_w.shape[1] - 1
        fc_p.append(fb_p[:, V7X_SUBLANES - kf:])
        fc_s.append(fb_s[:, V7X_SUBLANES - kf:])
    return (xp, xs, jnp.stack(ckv_p), jnp.stack(kr_p), jnp.stack(ckv_s), jnp.stack(kr_s),
            jnp.stack(lh_p), jnp.stack(lh_s), jnp.stack(lc_p), jnp.stack(lc_s), jnp.stack(fc_p), jnp.stack(fc_s))
```

```python
import functools
import math
from typing import NamedTuple

import jax
import jax.numpy as jnp
from jax import lax
from jax.experimental import pallas as pl
from jax.experimental.pallas import tpu as pltpu

EPS = 1e-6
ROPE_THETA = 10000.0
LRU_C = 8.0

V7X_LANES = 128
V7X_SUBLANES = 8
V7X_VMEM_BYTES = 64 * 1024 * 1024
VMEM_LIMIT = V7X_VMEM_BYTES * 7 // 8

F32 = jnp.float32
BF16 = jnp.bfloat16


def _params(*sem):
    return pltpu.CompilerParams(dimension_semantics=sem, vmem_limit_bytes=VMEM_LIMIT)


def _const_spec(a):
    nd = a.ndim
    return pl.BlockSpec(a.shape, lambda *_: (0,) * nd)


def _rms(x, g):
    ms = jnp.mean(x * x, axis=-1, keepdims=True)
    return x * lax.rsqrt(ms + EPS) * g


def _sigmoid(x):
    return 1.0 / (1.0 + jnp.exp(-x))


def _rope(y, cos, sin_lo, sin_hi, half):
    return y * cos + pltpu.roll(y, V7X_LANES - half, axis=1) * sin_lo + pltpu.roll(y, half, axis=1) * sin_hi


def _tile_rows(tab_ref, bb):
    t = tab_ref[...]
    if bb == 1:
        return t
    return jnp.broadcast_to(t[None], (bb,) + t.shape).reshape(bb * t.shape[0], t.shape[1])


def _key_norm_factor(cb, wk_ref, n_heads, nope):
    k = jnp.dot(cb, wk_ref[...], preferred_element_type=F32)
    ksq = k * k
    part = ksq[:, :V7X_LANES]
    for i in range(1, wk_ref.shape[1] // V7X_LANES):
        part = part + ksq[:, V7X_LANES * i:V7X_LANES * (i + 1)]
    shift = V7X_LANES // 2
    while shift >= n_heads:
        part = part + pltpu.roll(part, shift, axis=1)
        shift //= 2
    return lax.rsqrt(part / nope + EPS)


def _mla_proj_kernel(x_ref, gmix_ref, wd_ref, gql_ref, gkv_ref, gkr_ref, wuq_ref, gq_ref, wuk_ref, gk_ref, wuv_ref,
                     cos_ref, sinlo_ref, sinhi_ref, *out_refs, n_heads, q_lora, kv_lora, nope, rope, scale, with_kv):
    if with_kv:
        q_ref, ckv_ref, kr_ref, k_ref, v_ref = out_refs
    else:
        q_ref, ckv_ref, kr_ref, rs_ref = out_refs
    bb, tt, d = x_ref.shape
    rows = bb * tt
    half = rope // 2
    xn = _rms(x_ref[...].reshape(rows, d), gmix_ref[...])
    dd = jnp.dot(xn.astype(BF16), wd_ref[...], preferred_element_type=F32)
    cq = _rms(dd[:, :q_lora], gql_ref[...])
    ckv = _rms(dd[:, q_lora:q_lora + kv_lora], gkv_ref[...])
    kr = dd[:, q_lora + kv_lora:]
    cos = _tile_rows(cos_ref, bb)
    sin_lo = _tile_rows(sinlo_ref, bb)
    sin_hi = _tile_rows(sinhi_ref, bb)
    lane = lax.broadcasted_iota(jnp.int32, (1, V7X_LANES), 1)
    is_rope = lane < rope

    kr_ss = jnp.sum(kr * kr, axis=-1, keepdims=True)
    krr = _rope(kr * lax.rsqrt(kr_ss / rope + EPS) * gkr_ref[...], cos, sin_lo, sin_hi, half)
    kr_ref[...] = krr.reshape(bb, tt, V7X_LANES)
    ckv_ref[...] = ckv.reshape(bb, tt, kv_lora)

    q = jnp.dot(cq.astype(BF16), wuq_ref[...], preferred_element_type=F32)
    ckv_b = ckv.astype(BF16)
    if with_kv:
        kk = jnp.dot(ckv_b, wuk_ref[...], preferred_element_type=F32)
        v_ref[...] = jnp.dot(ckv_b, wuv_ref[...], preferred_element_type=F32).astype(BF16).reshape(v_ref.shape)
    else:
        rs_ref[...] = _key_norm_factor(ckv_b, wuk_ref, n_heads, nope).reshape(bb, tt, V7X_LANES)
    for h in range(n_heads):
        sl = slice(V7X_LANES * h, V7X_LANES * (h + 1))
        qb = q[:, sl]
        sq = qb * qb
        ss_r = jnp.sum(jnp.where(is_rope, sq, 0.0), axis=-1, keepdims=True)
        ss_n = jnp.sum(jnp.where(is_rope, 0.0, sq), axis=-1, keepdims=True)
        rs = jnp.where(is_rope, lax.rsqrt(ss_r / rope + EPS), lax.rsqrt(ss_n / nope + EPS))
        y = _rope(qb * rs * gq_ref[...], cos, sin_lo, sin_hi, half)
        q_ref[:, :, sl] = (y * scale).astype(BF16).reshape(bb, tt, V7X_LANES)
        if with_kv:
            kb = kk[:, sl]
            ss_k = jnp.sum(kb * kb, axis=-1, keepdims=True)
            kn = kb * lax.rsqrt(ss_k / nope + EPS) * gk_ref[...]
            k_ref[:, :, sl] = (kn + krr).astype(BF16).reshape(bb, tt, V7X_LANES)


def _mla_proj(x, gmix, wd, gql, gkv, gkr, wuq, gq, wuk, gk, wuv, cos, sin_lo, sin_hi, *, bb, tt, dims, with_kv):
    n_heads, q_lora, kv_lora, nope, rope, vh, scale = dims
    b, t, d = x.shape
    hl = n_heads * V7X_LANES
    tok = lambda c: pl.BlockSpec((bb, tt, c), lambda i, j: (i, j, 0))
    tab = pl.BlockSpec((tt, V7X_LANES), lambda i, j: (j, 0))
    out_shape = [jax.ShapeDtypeStruct((b, t, hl), BF16), jax.ShapeDtypeStruct((b, t, kv_lora), F32),
                 jax.ShapeDtypeStruct((b, t, V7X_LANES), F32)]
    out_specs = [tok(hl), tok(kv_lora), tok(V7X_LANES)]
    if with_kv:
        out_shape += [jax.ShapeDtypeStruct((b, t, hl), BF16), jax.ShapeDtypeStruct((b, t, n_heads * vh), BF16)]
        out_specs += [tok(hl), tok(n_heads * vh)]
    else:
        out_shape.append(jax.ShapeDtypeStruct((b, t, V7X_LANES), F32))
        out_specs.append(tok(V7X_LANES))
    consts = (gmix, wd, gql, gkv, gkr, wuq, gq, wuk, gk, wuv)
    return pl.pallas_call(
        functools.partial(_mla_proj_kernel, n_heads=n_heads, q_lora=q_lora, kv_lora=kv_lora, nope=nope, rope=rope,
                          scale=scale, with_kv=with_kv),
        grid=(b // bb, t // tt),
        in_specs=[tok(d)] + [_const_spec(a) for a in consts] + [tab, tab, tab],
        out_specs=out_specs, out_shape=out_shape,
        compiler_params=_params("parallel", "parallel"), name="mla_proj",
    )(x, *consts, cos, sin_lo, sin_hi)


def _attn_block(q, k_ref, v_ref, lanes, vlanes, past, tq):
    dn = (((1,), (1,)), ((), ()))
    row = lax.broadcasted_iota(jnp.int32, (tq, tq), 0)
    col = lax.broadcasted_iota(jnp.int32, (tq, tq), 1)
    s_d = lax.dot_general(q, k_ref[0, past:past + tq, lanes], dn, preferred_element_type=F32)
    s_d = jnp.where(col <= row, s_d, -jnp.inf)
    m = jnp.max(s_d, axis=1, keepdims=True)
    if past:
        s_p = lax.dot_general(q, k_ref[0, :past, lanes], dn, preferred_element_type=F32)
        m = jnp.maximum(m, jnp.max(s_p, axis=1, keepdims=True))
    p_d = jnp.exp2(s_d - m)
    l = jnp.sum(p_d, axis=1, keepdims=True)
    acc = jnp.dot(p_d.astype(BF16), v_ref[0, past:past + tq, vlanes], preferred_element_type=F32)
    if past:
        p_p = jnp.exp2(s_p - m)
        l = l + jnp.sum(p_p, axis=1, keepdims=True)
        acc = acc + jnp.dot(p_p.astype(BF16), v_ref[0, :past, vlanes], preferred_element_type=F32)
    return acc / l


def _flash_kernel(q_ref, k_ref, v_ref, o_ref, *, tq, vh, n_q, hps):
    qi = pl.program_id(2)
    for n in range(n_q):
        @pl.when(qi == n)
        def _(n=n):
            outs = []
            for hh in range(hps):
                lanes = slice(V7X_LANES * hh, V7X_LANES * (hh + 1))
                outs.append(_attn_block(q_ref[0, :, lanes], k_ref, v_ref, lanes, slice(vh * hh, vh * (hh + 1)),
                                        n * tq, tq))
            o_ref[0] = jnp.concatenate(outs, axis=1).astype(BF16)


def _flash(q, k, v, *, tq, n_heads, vh, hps):
    b, t, _ = q.shape
    return pl.pallas_call(
        functools.partial(_flash_kernel, tq=tq, vh=vh, n_q=t // tq, hps=hps),
        grid=(b, n_heads // hps, t // tq),
        in_specs=[pl.BlockSpec((1, tq, hps * V7X_LANES), lambda i, h, j: (i, j, h)),
                  pl.BlockSpec((1, t, hps * V7X_LANES), lambda i, h, j: (i, 0, h)),
                  pl.BlockSpec((1, t, hps * vh), lambda i, h, j: (i, 0, h))],
        out_specs=pl.BlockSpec((1, tq, hps * vh), lambda i, h, j: (i, j, h)),
        out_shape=jax.ShapeDtypeStruct((b, t, n_heads * vh), BF16),
        compiler_params=_params("parallel", "parallel", "arbitrary"), name="mla_prompt_attn",
    )(q, k, v)


def _absorb_kernel(q_ref, wa_ref, qa_ref, *, n_heads, kv_lora):
    for h in range(n_heads):
        qh = q_ref[:, V7X_LANES * h:V7X_LANES * (h + 1)]
        qa_ref[:, kv_lora * h:kv_lora * (h + 1)] = jnp.dot(qh, wa_ref[h], preferred_element_type=F32).astype(BF16)


def _absorb(q2d, wa, *, tr):
    r = q2d.shape[0]
    n_heads, _, kv_lora = wa.shape
    return pl.pallas_call(
        functools.partial(_absorb_kernel, n_heads=n_heads, kv_lora=kv_lora),
        grid=(r // tr,),
        in_specs=[pl.BlockSpec((tr, n_heads * V7X_LANES), lambda i: (i, 0)), _const_spec(wa)],
        out_specs=pl.BlockSpec((tr, n_heads * kv_lora), lambda i: (i, 0)),
        out_shape=jax.ShapeDtypeStruct((r, n_heads * kv_lora), BF16),
        compiler_params=_params("parallel"), name="mla_absorb_q",
    )(q2d, wa)


def _decode_attn_kernel(pt_ref, ckv_hbm, krt_hbm, wk_ref, qa_ref, qr_ref, cnew_ref, krnew_ref, rsnew_ref, o_ref,
                        cbuf, krbuf, cb_scr, s_scr, m_scr, l_scr, acc_scr, csem, ksem,
                        *, layer, pg, n_groups, n_heads, nope, rope):
    b = pl.program_id(0)
    page = ckv_hbm.shape[2]
    kv_lora = ckv_hbm.shape[3]
    dn_last = (((1,), (1,)), ((), ()))
    dn_first = (((0,), (0,)), ((), ()))
    qa = qa_ref[...]
    qr = qr_ref[:, :rope]

    def copies(seq, grp, slot):
        out = []
        for s in range(pg):
            pid = pt_ref[seq, grp * pg + s]
            rows = pl.ds(s * page, page)
            out.append(pltpu.make_async_copy(ckv_hbm.at[layer, pid], cbuf.at[slot, rows, :], csem.at[slot]))
            out.append(pltpu.make_async_copy(krt_hbm.at[layer, pid], krbuf.at[slot, :, rows], ksem.at[slot]))
        return out

    def fetch(seq, grp, slot):
        for cp in copies(seq, grp, slot):
            cp.start()

    def wait(seq, grp, slot):
        for cp in copies(seq, grp, slot):
            cp.wait()

    @pl.when(b == 0)
    def _():
        fetch(0, 0, 0)

    n_new = cnew_ref.shape[0]
    pad = 2 * V7X_SUBLANES - n_new
    c_new = jnp.concatenate([cnew_ref[...], jnp.zeros((pad, kv_lora), F32)], axis=0).astype(BF16)
    kr_new = jnp.concatenate([krnew_ref[:, :rope], jnp.zeros((pad, rope), F32)], axis=0).astype(BF16)
    rs_new = jnp.concatenate([rsnew_ref[...], jnp.ones((pad, V7X_LANES), F32)], axis=0)
    s = lax.dot_general(c_new, qa, dn_last, preferred_element_type=F32) * rs_new
    s = s + lax.dot_general(kr_new, qr, dn_last, preferred_element_type=F32)
    key = lax.broadcasted_iota(jnp.int32, s.shape, 0)
    qry = lax.broadcasted_iota(jnp.int32, s.shape, 1) // n_heads
    s = jnp.where(key <= qry, s, -jnp.inf)
    m0 = jnp.max(s, axis=0, keepdims=True)
    p = jnp.exp2(s - m0)
    m_scr[...] = m0
    l_scr[...] = jnp.sum(p, axis=0, keepdims=True)
    acc_scr[...] = lax.dot_general(c_new, p.astype(BF16), dn_first, preferred_element_type=F32)

    def stage_a(slot):
        cb = cbuf[slot].astype(BF16)
        cb_scr[slot] = cb
        rs = _key_norm_factor(cb, wk_ref, n_heads, nope)
        s = lax.dot_general(cb, qa, dn_last, preferred_element_type=F32) * rs
        krb = krbuf[slot].astype(BF16)
        s_scr[slot] = s + lax.dot_general(krb, qr, (((0,), (1,)), ((), ())), preferred_element_type=F32)

    def stage_b(slot):
        s = s_scr[slot]
        m_old = m_scr[...]
        m_new = jnp.maximum(m_old, jnp.max(s, axis=0, keepdims=True))
        p = jnp.exp2(s - m_new)
        alpha = jnp.exp2(m_old - m_new)
        l_scr[...] = alpha * l_scr[...] + jnp.sum(p, axis=0, keepdims=True)
        pv = lax.dot_general(cb_scr[slot], p.astype(BF16), dn_first, preferred_element_type=F32)
        acc_scr[...] = alpha * acc_scr[...] + pv
        m_scr[...] = m_new

    for i in range(n_groups + 1):
        slot = i % 2
        if i + 1 < n_groups:
            fetch(b, i + 1, 1 - slot)
        elif i + 1 == n_groups:
            @pl.when(b + 1 < pl.num_programs(0))
            def _():
                fetch(b + 1, 0, 0)
        if i < n_groups:
            wait(b, i, slot)
            stage_a(slot)
        if i >= 1:
            stage_b(1 - slot)
    o_ref[...] = (acc_scr[...] / l_scr[...]).T


def _decode_attn(page_table, cache_ckv, cache_krope, layer, wk_perm, qa, qr, c_new, kr_new, rs_new, *, pg, dims):
    n_heads, _, kv_lora, nope, rope, _, _ = dims
    b, n_pages = page_table.shape
    page = cache_ckv.shape[2]
    n_new = c_new.shape[1]
    rows = n_new * n_heads
    n_groups = n_pages // pg
    keys = pg * page
    assert rows == V7X_LANES and n_pages % pg == 0 and n_groups % 2 == 0 and n_new <= 2 * V7X_SUBLANES
    krope_t = jnp.swapaxes(cache_krope, 2, 3)
    per_seq = lambda r, c: pl.BlockSpec((None, r, c), lambda i, pt: (i, 0, 0))
    hbm = pl.BlockSpec(memory_space=pl.ANY)
    grid_spec = pltpu.PrefetchScalarGridSpec(
        num_scalar_prefetch=1, grid=(b,),
        in_specs=[hbm, hbm, pl.BlockSpec(wk_perm.shape, lambda i, pt: (0, 0)),
                  per_seq(rows, kv_lora), per_seq(rows, V7X_LANES),
                  per_seq(n_new, kv_lora), per_seq(n_new, V7X_LANES), per_seq(n_new, V7X_LANES)],
        out_specs=per_seq(rows, kv_lora),
        scratch_shapes=[pltpu.VMEM((2, keys, kv_lora), F32), pltpu.VMEM((2, rope, keys), F32),
                        pltpu.VMEM((2, keys, kv_lora), BF16), pltpu.VMEM((2, keys, V7X_LANES), F32),
                        pltpu.VMEM((1, V7X_LANES), F32), pltpu.VMEM((1, V7X_LANES), F32),
                        pltpu.VMEM((kv_lora, V7X_LANES), F32),
                        pltpu.SemaphoreType.DMA((2,)), pltpu.SemaphoreType.DMA((2,))])
    return pl.pallas_call(
        functools.partial(_decode_attn_kernel, layer=layer, pg=pg, n_groups=n_groups, n_heads=n_heads, nope=nope,
                          rope=rope),
        grid_spec=grid_spec, out_shape=jax.ShapeDtypeStruct((b, rows, kv_lora), F32),
        compiler_params=_params("arbitrary"), name="mla_decode_attn",
    )(page_table, cache_ckv, krope_t, wk_perm, qa, qr, c_new, kr_new, rs_new)


def _unabsorb_kernel(ctx_ref, wv_ref, o_ref, *, n_pairs, kv_lora):
    for p in range(n_pairs):
        x = ctx_ref[:, 2 * kv_lora * p:2 * kv_lora * (p + 1)].astype(BF16)
        o_ref[:, V7X_LANES * p:V7X_LANES * (p + 1)] = jnp.dot(x, wv_ref[p], preferred_element_type=F32).astype(BF16)


def _unabsorb(ctx2d, wv2, *, tr):
    r = ctx2d.shape[0]
    n_pairs, k2, _ = wv2.shape
    return pl.pallas_call(
        functools.partial(_unabsorb_kernel, n_pairs=n_pairs, kv_lora=k2 // 2),
        grid=(r // tr,),
        in_specs=[pl.BlockSpec((tr, n_pairs * k2), lambda i: (i, 0)), _const_spec(wv2)],
        out_specs=pl.BlockSpec((tr, n_pairs * V7X_LANES), lambda i: (i, 0)),
        out_shape=jax.ShapeDtypeStruct((r, n_pairs * V7X_LANES), BF16),
        compiler_params=_params("parallel"), name="mla_unabsorb_v",
    )(ctx2d, wv2)


def _conv_prompt(g, w, b, buf_ref, carry):
    rows = g.shape[0]
    width = w.shape[0]
    buf_ref[0:V7X_SUBLANES, :] = carry
    buf_ref[V7X_SUBLANES:, :] = g
    y = b + g * w[width - 1:width, :]
    for s in range(1, width):
        y = y + buf_ref[V7X_SUBLANES - s:V7X_SUBLANES - s + rows, :] * w[width - 1 - s:width - s, :]
    return y


def _conv_decode(g, w, b, prev, t_idx):
    width = w.shape[0]
    rows = g.shape[0]
    y = b + g * w[width - 1:width, :]
    for s in range(1, width):
        sh = jnp.where(t_idx >= s, pltpu.roll(g, s, axis=0), pltpu.roll(prev, rows - V7X_SUBLANES + s, axis=0))
        y = y + sh * w[width - 1 - s:width - s, :]
    return y


def _ffn_kernel(*refs, decode, fc):
    if decode:
        x_ref, a_ref, wpre_ref, gn_ref, wg_ref, wu_ref, cw_ref, cb_ref, wd_ref, st_ref, y_ref, so_ref = refs
    else:
        (x_ref, a_ref, wpre_ref, gn_ref, wg_ref, wu_ref, cw_ref, cb_ref, wd_ref,
         y_ref, so_ref, carry_scr, buf_scr) = refs
    bb, tt, d = x_ref.shape
    rows = bb * tt
    dff = wg_ref.shape[1]
    a = a_ref[...].reshape(rows, a_ref.shape[2])
    x1 = x_ref[...].reshape(rows, d) + jnp.dot(a, wpre_ref[...], preferred_element_type=F32)
    xn = _rms(x1, gn_ref[...]).astype(BF16)
    if decode:
        t_idx = lax.broadcasted_iota(jnp.int32, (rows, fc), 0) % tt
    else:
        @pl.when(pl.program_id(1) == 0)
        def _():
            carry_scr[...] = jnp.zeros(carry_scr.shape, F32)

    acc = x1
    for c in range(dff // fc):
        cols = slice(c * fc, (c + 1) * fc)
        g = jnp.dot(xn, wg_ref[:, cols], preferred_element_type=F32)
        u = jnp.dot(xn, wu_ref[:, cols], preferred_element_type=F32)
        if decode:
            gc = _conv_decode(g, cw_ref[:, cols], cb_ref[:, cols], st_ref[:, :, cols].reshape(rows, fc), t_idx)
            so_ref[:, :, cols] = g.reshape(bb, tt, fc)
        else:
            gc = _conv_prompt(g, cw_ref[:, cols], cb_ref[:, cols], buf_scr.at[c], carry_scr[:, cols])
            last = g[rows - V7X_SUBLANES:, :]
            carry_scr[:, cols] = last
            so_ref[0, :, cols] = last
        h = (gc * _sigmoid(gc) * u).astype(BF16)
        acc = acc + jnp.dot(h, wd_ref[cols, :], preferred_element_type=F32)
    y_ref[...] = acc.reshape(bb, tt, d)


def _resident_spec(a):
    nd = a.ndim
    return pl.BlockSpec(a.shape, lambda *_: (0,) * nd, pipeline_mode=pl.Buffered(1))


def _ffn(x, a, w_pre, gn, wg, wu, cw, cb, wd, state, *, bb, tt, fc):
    b, t, d = x.shape
    da = a.shape[2]
    dff = wg.shape[1]
    rows = bb * tt
    decode = state is not None
    tok = lambda c: pl.BlockSpec((bb, tt, c), lambda i, j: (i, j, 0))
    weights = [w_pre, gn, wg, wu, cw, cb, wd]
    in_specs = [tok(d), tok(da)] + [_resident_spec(w) for w in weights]
    args = [x, a] + weights
    scratch = []
    if decode:
        in_specs.append(pl.BlockSpec((bb, tt, dff), lambda i, j: (i, 0, 0)))
        args.append(state)
    else:
        assert bb == 1
        scratch = [pltpu.VMEM((V7X_SUBLANES, dff), F32), pltpu.VMEM((dff // fc, V7X_SUBLANES + rows, fc), F32)]
    so_spec = pl.BlockSpec((bb, None, V7X_SUBLANES, dff), lambda i, j: (i, j, 0, 0))
    y, so = pl.pallas_call(
        functools.partial(_ffn_kernel, decode=decode, fc=fc),
        grid=(b // bb, t // tt), in_specs=in_specs,
        out_specs=[tok(d), so_spec],
        out_shape=[jax.ShapeDtypeStruct((b, t, d), F32),
                   jax.ShapeDtypeStruct((b, t // tt, V7X_SUBLANES, dff), F32)],
        scratch_shapes=scratch,
        compiler_params=_params("parallel", "arbitrary"), name="conv_ffn",
    )(*args)
    return y, so[:, -1]


def _rec_kernel(*refs, decode):
    if decode:
        (x_ref, gn_ref, win_ref, cw_ref, cb_ref, wgate_ref, ba_ref, bi_ref, lam_ref, st_ref, h0_ref,
         a_ref, co_ref, ho_ref, a_scr, b_scr) = refs
    else:
        (x_ref, gn_ref, win_ref, cw_ref, cb_ref, wgate_ref, ba_ref, bi_ref, lam_ref,
         a_ref, co_ref, ho_ref, a_scr, b_scr, h_scr, carry_scr, hc_scr, buf_scr) = refs
    bb, tt, d = x_ref.shape
    rows = bb * tt
    w = cw_ref.shape[1]
    t = pl.program_id(1)
    xn = _rms(x_ref[...].reshape(rows, d), gn_ref[...]).astype(BF16)
    z = jnp.dot(xn, win_ref[...], preferred_element_type=F32)
    xr = z[:, :w]
    gin = z[:, w:]
    gate = 0.5 * gin * (1.0 + jnp.tanh(0.7978845608028654 * (gin + 0.044715 * (gin * gin * gin))))
    t_idx = lax.broadcasted_iota(jnp.int32, (rows, w), 0) % V7X_SUBLANES
    if decode:
        xc = _conv_decode(xr, cw_ref[...], cb_ref[...], st_ref[...].reshape(rows, w), t_idx)
        co_ref[...] = xr.reshape(bb, tt, w)
    else:
        @pl.when(t == 0)
        def _():
            carry_scr[...] = jnp.zeros(carry_scr.shape, F32)
            hc_scr[...] = jnp.zeros(hc_scr.shape, F32)

        xc = _conv_prompt(xr, cw_ref[...], cb_ref[...], buf_scr, carry_scr[...])
        last = xr[rows - V7X_SUBLANES:, :]
        carry_scr[...] = last
        co_ref[0] = last

    n_blk = wgate_ref.shape[0]
    kb = wgate_ref.shape[1]
    xcb = xc.astype(BF16)
    ra, ri = [], []
    for p in range(n_blk):
        zz = jnp.dot(xcb[:, kb * p:kb * (p + 1)], wgate_ref[p], preferred_element_type=F32)
        ra.append(zz[:, :kb])
        ri.append(zz[:, kb:])
    r = _sigmoid(jnp.concatenate(ra, axis=1) + ba_ref[...])
    gi = _sigmoid(jnp.concatenate(ri, axis=1) + bi_ref[...])
    neg_lam = -lam_ref[...]
    softplus = jnp.maximum(neg_lam, 0.0) + jnp.log(1.0 + jnp.exp(-jnp.abs(neg_lam)))
    av = jnp.exp(r * (-LRU_C * softplus))
    bv = jnp.sqrt(1.0 - av * av) * (gi * xc)

    for s in (1, 2, 4):
        keep = t_idx >= s
        bv = jnp.where(keep, av * pltpu.roll(bv, s, axis=0) + bv, bv)
        av = jnp.where(keep, av * pltpu.roll(av, s, axis=0), av)
    if decode:
        hs = av * h0_ref[...].reshape(rows, w) + bv
        ho_ref[...] = hs.reshape(bb, tt, w)
    else:
        a_scr[...] = av
        b_scr[...] = bv

        def group(i, h_in):
            sl = pl.ds(pl.multiple_of(i * V7X_SUBLANES, V7X_SUBLANES), V7X_SUBLANES)
            hg = a_scr[sl, :] * h_in + b_scr[sl, :]
            h_scr[sl, :] = hg
            return jnp.broadcast_to(hg[V7X_SUBLANES - 1:, :], (V7X_SUBLANES, w))

        h_last = lax.fori_loop(0, rows // V7X_SUBLANES, group, hc_scr[...])
        hc_scr[...] = h_last
        ho_ref[0] = h_last
        hs = h_scr[...]
    a_ref[...] = (hs * gate).astype(BF16).reshape(bb, tt, w)


def _rec(x, gn, w_in, cw, cb, w_gate, b_a, b_i, lam, state, h0, *, bb, tt):
    b, t, d = x.shape
    w = cw.shape[1]
    rows = bb * tt
    decode = state is not None
    tok = lambda c: pl.BlockSpec((bb, tt, c), lambda i, j: (i, j, 0))
    consts = (gn, w_in, cw, cb, w_gate, b_a, b_i, lam)
    in_specs = [tok(d)] + [_const_spec(a) for a in consts]
    args = [x, *consts]
    scratch = [pltpu.VMEM((rows, w), F32), pltpu.VMEM((rows, w), F32)]
    grp = pl.BlockSpec((bb, V7X_SUBLANES, w), lambda i, j: (i, 0, 0))
    if decode:
        in_specs += [grp, grp]
        args += [state, h0]
    else:
        assert bb == 1
        scratch += [pltpu.VMEM((rows, w), F32), pltpu.VMEM((V7X_SUBLANES, w), F32), pltpu.VMEM((V7X_SUBLANES, w), F32),
                    pltpu.VMEM((V7X_SUBLANES + rows, w), F32)]
    return pl.pallas_call(
        functools.partial(_rec_kernel, decode=decode),
        grid=(b // bb, t // tt), in_specs=in_specs,
        out_specs=[tok(w), grp, grp],
        out_shape=[jax.ShapeDtypeStruct((b, t, w), BF16), jax.ShapeDtypeStruct((b, V7X_SUBLANES, w), F32),
                   jax.ShapeDtypeStruct((b, V7X_SUBLANES, w), F32)],
        scratch_shapes=scratch,
        compiler_params=_params("parallel", "arbitrary"), name="rglru_mixer",
    )(*args)


def _pad_state(buf):
    return jnp.pad(buf, ((0, 0), (V7X_SUBLANES - buf.shape[1], 0), (0, 0)))


class _Tiles(NamedTuple):
    prompt_rows: int
    decode_seqs: int
    attn_q_rows: int
    attn_heads: int
    ffn_chunk: int
    pages_per_group: int


def _tiles(tp, bs, ts, n_heads, vh, dff, n_pages):
    heads = 4 if n_heads % 4 == 0 and (4 * vh) % V7X_LANES == 0 else 2
    return _Tiles(prompt_rows=min(tp, 512), decode_seqs=min(bs, 256 // ts), attn_q_rows=min(tp, 256),
                  attn_heads=heads, ffn_chunk=min(dff, 1024), pages_per_group=min(n_pages, 16))


def kernel(x_prompt, x_sample, cache_ckv, cache_krope, page_table, state_lru_h, state_lru_conv, state_ffn_conv,
           norm_mix, norm_ffn, attn_w_down, attn_g_q_lora, attn_w_uq, attn_g_kv_lora, attn_g_qn, attn_g_qr,
           attn_g_kn, attn_g_kr, attn_w_uk, attn_w_uv, attn_w_o, rec_w_in, rec_conv_w, rec_conv_b, rec_w_a,
           rec_b_a, rec_w_i, rec_b_i, rec_lambda, rec_w_out, ffn_w_up, ffn_conv_w, ffn_conv_b, ffn_w_down):
    bp, tp, d = x_prompt.shape
    bs, ts, _ = x_sample.shape
    depth = norm_mix.shape[0]
    n_mixers = 2
    kv_lora, n_heads, nope = attn_w_uk.shape[1:]
    vh = attn_w_uv.shape[3]
    rope = attn_g_qr.shape[1]
    q_lora = attn_g_q_lora.shape[1]
    half = rope // 2
    page = cache_ckv.shape[2]
    past = page_table.shape[1] * page
    dff = ffn_conv_w.shape[2]
    lru_w = rec_conv_w.shape[2]
    assert ts == V7X_SUBLANES and rope + nope <= V7X_LANES and n_heads % 2 == 0
    scale = float((nope + rope) ** -0.5) * math.log2(math.e)
    dims = (n_heads, q_lora, kv_lora, nope, rope, vh, scale)
    pad_l = V7X_LANES - rope - nope
    row = lambda v: v.reshape(1, -1).astype(F32)

    def rope_tables(pos):
        inv = ROPE_THETA ** (-jnp.arange(0, rope, 2, dtype=F32) / rope)
        ang = pos.astype(F32)[:, None] * inv[None, :]
        cos, sin = jnp.cos(ang), jnp.sin(ang)
        n = pos.shape[0]
        ones = jnp.ones((n, V7X_LANES - rope), F32)
        zeros = jnp.zeros((n, V7X_LANES - rope), F32)
        z_half = jnp.zeros((n, half), F32)
        return (jnp.concatenate([cos, cos, ones], axis=1), jnp.concatenate([-sin, z_half, zeros], axis=1),
                jnp.concatenate([z_half, sin, zeros], axis=1))

    tabs_p = rope_tables(jnp.arange(tp))
    tabs_s = rope_tables(past + jnp.arange(ts))

    xp, xs = x_prompt, x_sample
    a_p = a_s = w_pre = None
    ckv_p, kr_p, ckv_s, kr_s = [], [], [], []
    lh_p, lh_s, lc_p, lc_s = [], [], [], []
    fc_p, fc_s = [], []
    tiles = _tiles(tp, bs, ts, n_heads, vh, dff, page_table.shape[1])
    for i in range(depth):
        j = i // n_mixers
        gmix = row(norm_mix[i])
        if i % n_mixers == 0:
            wdn = attn_w_down[j]
            wd = jnp.concatenate([wdn, jnp.zeros((d, V7X_LANES - rope), F32)], axis=1).astype(BF16)
            gkr = jnp.concatenate([attn_g_kr[j], jnp.zeros((V7X_LANES - rope,), F32)]).reshape(1, -1)
            wq = attn_w_uq[j].reshape(q_lora, n_heads, nope + rope)
            wuq = jnp.concatenate([wq[:, :, nope:], wq[:, :, :nope], jnp.zeros((q_lora, n_heads, pad_l), F32)],
                                  axis=2).reshape(q_lora, n_heads * V7X_LANES).astype(BF16)
            gq = jnp.concatenate([attn_g_qr[j], attn_g_qn[j], jnp.zeros((pad_l,), F32)]).reshape(1, -1)
            wk = attn_w_uk[j]
            wuk = jnp.concatenate([jnp.zeros((kv_lora, n_heads, rope), F32), wk,
                                   jnp.zeros((kv_lora, n_heads, pad_l), F32)],
                                  axis=2).reshape(kv_lora, n_heads * V7X_LANES).astype(BF16)
            gk = jnp.concatenate([jnp.zeros((rope,), F32), attn_g_kn[j], jnp.zeros((pad_l,), F32)]).reshape(1, -1)
            wuv = attn_w_uv[j].reshape(kv_lora, n_heads * vh).astype(BF16)
            proj_w = (gmix, wd, row(attn_g_q_lora[j]), row(attn_g_kv_lora[j]), gkr, wuq, gq, wuk, gk, wuv)

            q_p, c_p, krb_p, k_p, v_p = _mla_proj(xp, *proj_w, *tabs_p, bb=1, tt=tiles.prompt_rows, dims=dims,
                                                  with_kv=True)
            wk_perm = jnp.transpose(wk, (0, 2, 1)).reshape(kv_lora, nope * n_heads).astype(BF16)
            proj_s = proj_w[:7] + (wk_perm,) + proj_w[8:]
            q_s, c_s, krb_s, rs_s = _mla_proj(xs, *proj_s, *tabs_s, bb=tiles.decode_seqs, tt=ts, dims=dims,
                                              with_kv=False)
            a_p = _flash(q_p, k_p, v_p, tq=tiles.attn_q_rows, n_heads=n_heads, vh=vh, hps=tiles.attn_heads)

            wa = jnp.transpose(wk * attn_g_kn[j][None, None, :], (1, 2, 0))
            wa = jnp.concatenate([jnp.zeros((n_heads, rope, kv_lora), F32), wa,
                                  jnp.zeros((n_heads, pad_l, kv_lora), F32)], axis=1).astype(BF16)
            qa = _absorb(q_s.reshape(bs * ts, n_heads * V7X_LANES), wa, tr=tiles.decode_seqs * ts)
            qa = qa.reshape(bs, ts * n_heads, kv_lora)
            qr = q_s.reshape(bs, ts * n_heads, V7X_LANES)
            ctx = _decode_attn(page_table, cache_ckv, cache_krope, j, wk_perm, qa, qr, c_s, krb_s, rs_s,
                               pg=tiles.pages_per_group, dims=dims)
            wv = attn_w_uv[j]
            zer = jnp.zeros((kv_lora, n_heads // 2, vh), F32)
            wv2 = jnp.concatenate([jnp.concatenate([wv[:, 0::2], zer], axis=2),
                                   jnp.concatenate([zer, wv[:, 1::2]], axis=2)], axis=0)
            wv2 = jnp.transpose(wv2, (1, 0, 2)).astype(BF16)
            a_s = _unabsorb(ctx.reshape(bs * ts, n_heads * kv_lora), wv2, tr=tiles.decode_seqs * ts)
            a_s = a_s.reshape(bs, ts, n_heads * vh)
            w_pre = attn_w_o[j].astype(BF16)
            ckv_p.append(c_p)
            kr_p.append(krb_p[:, :, :rope])
            ckv_s.append(c_s)
            kr_s.append(krb_s[:, :, :rope])
        else:
            n_blk, blk = rec_w_a.shape[1:3]
            assert n_blk % 2 == 0

            def pair(wm):
                z = jnp.zeros((n_blk // 2, blk, blk), F32)
                return jnp.concatenate([jnp.concatenate([wm[0::2], z], axis=2),
                                        jnp.concatenate([z, wm[1::2]], axis=2)], axis=1)

            w_gate = jnp.concatenate([pair(rec_w_a[j]), pair(rec_w_i[j])], axis=2).astype(BF16)
            rec_w = (gmix, rec_w_in[j].astype(BF16), rec_conv_w[j], row(rec_conv_b[j]), w_gate, row(rec_b_a[j]),
                     row(rec_b_i[j]), row(rec_lambda[j]))
            a_p, cv_p, h_p = _rec(xp, *rec_w, None, None, bb=1, tt=tiles.prompt_rows)
            h0 = jnp.broadcast_to(state_lru_h[j][:, None, :], (bs, ts, lru_w))
            a_s, cv_s, h_s = _rec(xs, *rec_w, _pad_state(state_lru_conv[j]), h0, bb=tiles.decode_seqs, tt=ts)
            w_pre = rec_w_out[j].astype(BF16)
            kw = rec_conv_w.shape[1] - 1
            lh_p.append(h_p[:, V7X_SUBLANES - 1])
            lh_s.append(h_s[:, V7X_SUBLANES - 1])
            lc_p.append(cv_p[:, V7X_SUBLANES - kw:])
            lc_s.append(cv_s[:, V7X_SUBLANES - kw:])
        wup = ffn_w_up[i]
        ffn_w = (w_pre, row(norm_ffn[i]), wup[:, :dff].astype(BF16), wup[:, dff:].astype(BF16), ffn_conv_w[i],
                 row(ffn_conv_b[i]), ffn_w_down[i].astype(BF16))
        xp, fb_p = _ffn(xp, a_p, *ffn_w, None, bb=1, tt=tiles.prompt_rows, fc=tiles.ffn_chunk)
        xs, fb_s = _ffn(xs, a_s, *ffn_w, _pad_state(state_ffn_conv[i]), bb=tiles.decode_seqs, tt=ts,
                        fc=tiles.ffn_chunk)
        kf = ffn_conv_w.shape[1] - 1
        fc_p.append(fb_p[:, V7X_SUBLANES - kf:])
        fc_s.append(fb_s[:, V7X_SUBLANES - kf:])
    return (xp, xs, jnp.stack(ckv_p), jnp.stack(kr_p), jnp.stack(ckv_s), jnp.stack(kr_s),
            jnp.stack(lh_p), jnp.stack(lh_s), jnp.stack(lc_p), jnp.stack(lc_s), jnp.stack(fc_p), jnp.stack(fc_s))
```

```python
import functools
import math
from typing import NamedTuple

import jax
import jax.numpy as jnp
from jax import lax
from jax.experimental import pallas as pl
from jax.experimental.pallas import tpu as pltpu

EPS = 1e-6
ROPE_THETA = 10000.0
LRU_C = 8.0

V7X_LANES = 128
V7X_SUBLANES = 8
V7X_VMEM_BYTES = 64 * 1024 * 1024
VMEM_LIMIT = V7X_VMEM_BYTES * 7 // 8

F32 = jnp.float32
BF16 = jnp.bfloat16


def _params(*sem):
    return pltpu.CompilerParams(dimension_semantics=sem, vmem_limit_bytes=VMEM_LIMIT)


def _const_spec(a):
    nd = a.ndim
    return pl.BlockSpec(a.shape, lambda *_: (0,) * nd)


def _rms(x, g):
    ms = jnp.mean(x * x, axis=-1, keepdims=True)
    return x * lax.rsqrt(ms + EPS) * g


def _sigmoid(x):
    return 1.0 / (1.0 + jnp.exp(-x))


def _rope(y, cos, sin, half):
    return y * cos + pltpu.roll(y, half, axis=1) * sin


def _tile_rows(tab_ref, bb):
    t = tab_ref[...]
    if bb == 1:
        return t
    return jnp.broadcast_to(t[None], (bb,) + t.shape).reshape(bb * t.shape[0], t.shape[1])


def _key_norm_factor(cb, wk_ref, n_heads, nope):
    k = jnp.dot(cb, wk_ref[...], preferred_element_type=F32)
    ksq = k * k
    part = ksq[:, :V7X_LANES]
    for i in range(1, wk_ref.shape[1] // V7X_LANES):
        part = part + ksq[:, V7X_LANES * i:V7X_LANES * (i + 1)]
    shift = V7X_LANES // 2
    while shift >= n_heads:
        part = part + pltpu.roll(part, shift, axis=1)
        shift //= 2
    return lax.rsqrt(part / nope + EPS)


def _mla_proj_kernel(x_ref, gmix_ref, wd_ref, gql_ref, gkv_ref, gkr_ref, wuq_ref, gq_ref, wuk_ref, gk_ref, wuv_ref,
                     vone_ref, cos_ref, sin_ref, *out_refs, n_heads, q_lora, kv_lora, nope, rope, scale, with_kv):
    if with_kv:
        q_ref, ckv_ref, kr_ref, k_ref, v_ref = out_refs
    else:
        q_ref, ckv_ref, kr_ref, rs_ref = out_refs
    bb, tt, d = x_ref.shape
    rows = bb * tt
    half = rope // 2
    xn = _rms(x_ref[...].reshape(rows, d), gmix_ref[...])
    dd = jnp.dot(xn.astype(BF16), wd_ref[...], preferred_element_type=F32)
    cq = _rms(dd[:, :q_lora], gql_ref[...])
    ckv = _rms(dd[:, q_lora:q_lora + kv_lora], gkv_ref[...])
    kr = dd[:, q_lora + kv_lora:]
    cos = _tile_rows(cos_ref, bb)
    sin = _tile_rows(sin_ref, bb)
    lane = lax.broadcasted_iota(jnp.int32, (1, V7X_LANES), 1)
    is_rope = lane < rope
    is_nope = jnp.logical_and(lane >= rope, lane < rope + nope)

    kr_ss = jnp.sum(jnp.where(is_rope, kr * kr, 0.0), axis=-1, keepdims=True)
    krr = _rope(kr * lax.rsqrt(kr_ss / rope + EPS) * gkr_ref[...], cos, sin, half)
    kr_ref[...] = krr.reshape(bb, tt, V7X_LANES)
    ckv_ref[...] = ckv.reshape(bb, tt, kv_lora)

    q = jnp.dot(cq.astype(BF16), wuq_ref[...], preferred_element_type=F32)
    ckv_b = ckv.astype(BF16)
    if with_kv:
        kk = jnp.dot(ckv_b, wuk_ref[...], preferred_element_type=F32)
        vv = jnp.dot(ckv_b, wuv_ref[...], preferred_element_type=F32) + vone_ref[...]
        v_ref[...] = vv.astype(BF16).reshape(v_ref.shape)
    else:
        rs_ref[...] = _key_norm_factor(ckv_b, wuk_ref, n_heads, nope).reshape(bb, tt, V7X_LANES)
    for h in range(n_heads):
        sl = slice(V7X_LANES * h, V7X_LANES * (h + 1))
        qb = q[:, sl]
        sq = qb * qb
        ss_r = jnp.sum(jnp.where(is_rope, sq, 0.0), axis=-1, keepdims=True)
        ss_n = jnp.sum(jnp.where(is_nope, sq, 0.0), axis=-1, keepdims=True)
        rs = jnp.where(is_nope, lax.rsqrt(ss_n / nope + EPS), lax.rsqrt(ss_r / rope + EPS))
        y = _rope(qb * rs * gq_ref[...], cos, sin, half)
        q_ref[:, :, sl] = (y * scale).astype(BF16).reshape(bb, tt, V7X_LANES)
        if with_kv:
            kb = kk[:, sl]
            ss_k = jnp.sum(kb * kb, axis=-1, keepdims=True)
            kn = kb * lax.rsqrt(ss_k / nope + EPS) * gk_ref[...]
            k_ref[:, :, sl] = (kn + krr).astype(BF16).reshape(bb, tt, V7X_LANES)


def _mla_proj(x, gmix, wd, gql, gkv, gkr, wuq, gq, wuk, gk, wuv, vone, cos, sin, *, bb, tt, dims, with_kv):
    n_heads, q_lora, kv_lora, nope, rope, vh, scale = dims
    b, t, d = x.shape
    hl = n_heads * V7X_LANES
    tok = lambda c: pl.BlockSpec((bb, tt, c), lambda i, j: (i, j, 0))
    tab = pl.BlockSpec((tt, V7X_LANES), lambda i, j: (j, 0))
    out_shape = [jax.ShapeDtypeStruct((b, t, hl), BF16), jax.ShapeDtypeStruct((b, t, kv_lora), F32),
                 jax.ShapeDtypeStruct((b, t, V7X_LANES), F32)]
    out_specs = [tok(hl), tok(kv_lora), tok(V7X_LANES)]
    if with_kv:
        out_shape += [jax.ShapeDtypeStruct((b, t, hl), BF16), jax.ShapeDtypeStruct((b, t, hl), BF16)]
        out_specs += [tok(hl), tok(hl)]
    else:
        out_shape.append(jax.ShapeDtypeStruct((b, t, V7X_LANES), F32))
        out_specs.append(tok(V7X_LANES))
    consts = (gmix, wd, gql, gkv, gkr, wuq, gq, wuk, gk, wuv, vone)
    return pl.pallas_call(
        functools.partial(_mla_proj_kernel, n_heads=n_heads, q_lora=q_lora, kv_lora=kv_lora, nope=nope, rope=rope,
                          scale=scale, with_kv=with_kv),
        grid=(b // bb, t // tt),
        in_specs=[tok(d)] + [_const_spec(a) for a in consts] + [tab, tab],
        out_specs=out_specs, out_shape=out_shape,
        compiler_params=_params("parallel", "parallel"), name="mla_proj",
    )(x, *consts, cos, sin)


def _attn_block(q, k_ref, v_ref, lanes, vh, past, tq):
    dn = (((1,), (1,)), ((), ()))
    row = lax.broadcasted_iota(jnp.int32, (tq, tq), 0)
    col = lax.broadcasted_iota(jnp.int32, (tq, tq), 1)
    s_d = lax.dot_general(q, k_ref[0, past:past + tq, lanes], dn, preferred_element_type=F32)
    s_d = jnp.where(col <= row, s_d, -jnp.inf)
    m = jnp.max(s_d, axis=1, keepdims=True)
    if past:
        s_p = lax.dot_general(q, k_ref[0, :past, lanes], dn, preferred_element_type=F32)
        m = jnp.maximum(m, jnp.max(s_p, axis=1, keepdims=True))
    p_d = jnp.exp2(s_d - m)
    acc = jnp.dot(p_d.astype(BF16), v_ref[0, past:past + tq, lanes], preferred_element_type=F32)
    if past:
        p_p = jnp.exp2(s_p - m)
        acc = acc + jnp.dot(p_p.astype(BF16), v_ref[0, :past, lanes], preferred_element_type=F32)
    return acc[:, :vh] / acc[:, vh:vh + 1]


def _flash_kernel(q_ref, k_ref, v_ref, o_ref, *, tq, vh, n_q, hps):
    qi = pl.program_id(2)
    for n in range(n_q):
        @pl.when(qi == n)
        def _(n=n):
            outs = []
            for hh in range(hps):
                lanes = slice(V7X_LANES * hh, V7X_LANES * (hh + 1))
                outs.append(_attn_block(q_ref[0, :, lanes], k_ref, v_ref, lanes, vh, n * tq, tq))
            o_ref[0] = jnp.concatenate(outs, axis=1).astype(BF16)


def _flash(q, k, v, *, tq, n_heads, vh, hps):
    b, t, _ = q.shape
    return pl.pallas_call(
        functools.partial(_flash_kernel, tq=tq, vh=vh, n_q=t // tq, hps=hps),
        grid=(b, n_heads // hps, t // tq),
        in_specs=[pl.BlockSpec((1, tq, hps * V7X_LANES), lambda i, h, j: (i, j, h)),
                  pl.BlockSpec((1, t, hps * V7X_LANES), lambda i, h, j: (i, 0, h)),
                  pl.BlockSpec((1, t, hps * V7X_LANES), lambda i, h, j: (i, 0, h))],
        out_specs=pl.BlockSpec((1, tq, hps * vh), lambda i, h, j: (i, j, h)),
        out_shape=jax.ShapeDtypeStruct((b, t, n_heads * vh), BF16),
        compiler_params=_params("parallel", "parallel", "arbitrary"), name="mla_prompt_attn",
    )(q, k, v)


def _absorb_kernel(q_ref, wa_ref, qa_ref, *, n_heads, kv_lora):
    for h in range(n_heads):
        qh = q_ref[:, V7X_LANES * h:V7X_LANES * (h + 1)]
        qa_ref[:, kv_lora * h:kv_lora * (h + 1)] = jnp.dot(qh, wa_ref[h], preferred_element_type=F32).astype(BF16)


def _absorb(q2d, wa, *, tr):
    r = q2d.shape[0]
    n_heads, _, kv_lora = wa.shape
    return pl.pallas_call(
        functools.partial(_absorb_kernel, n_heads=n_heads, kv_lora=kv_lora),
        grid=(r // tr,),
        in_specs=[pl.BlockSpec((tr, n_heads * V7X_LANES), lambda i: (i, 0)), _const_spec(wa)],
        out_specs=pl.BlockSpec((tr, n_heads * kv_lora), lambda i: (i, 0)),
        out_shape=jax.ShapeDtypeStruct((r, n_heads * kv_lora), BF16),
        compiler_params=_params("parallel"), name="mla_absorb_q",
    )(q2d, wa)


def _decode_attn_kernel(pt_ref, ckv_hbm, krt_hbm, wk_ref, qa_ref, qr_ref, cnew_ref, krnew_ref, rsnew_ref, o_ref,
                        cbuf, krbuf, cb_scr, s_scr, m_scr, l_scr, acc_scr, csem, ksem,
                        *, layer, pg, n_groups, n_heads, nope, rope):
    b = pl.program_id(0)
    page = ckv_hbm.shape[2]
    kv_lora = ckv_hbm.shape[3]
    dn_last = (((1,), (1,)), ((), ()))
    dn_first = (((0,), (0,)), ((), ()))
    qa = qa_ref[...]
    qr = qr_ref[:, :rope]

    def copies(seq, grp, slot):
        out = []
        for s in range(pg):
            pid = pt_ref[seq, grp * pg + s]
            rows = pl.ds(s * page, page)
            out.append(pltpu.make_async_copy(ckv_hbm.at[layer, pid], cbuf.at[slot, rows, :], csem.at[slot]))
            out.append(pltpu.make_async_copy(krt_hbm.at[layer, pid], krbuf.at[slot, :, rows], ksem.at[slot]))
        return out

    def fetch(seq, grp, slot):
        for cp in copies(seq, grp, slot):
            cp.start()

    def wait(seq, grp, slot):
        for cp in copies(seq, grp, slot):
            cp.wait()

    @pl.when(b == 0)
    def _():
        fetch(0, 0, 0)

    n_new = cnew_ref.shape[0]
    pad = 2 * V7X_SUBLANES - n_new
    c_new = jnp.concatenate([cnew_ref[...], jnp.zeros((pad, kv_lora), F32)], axis=0).astype(BF16)
    kr_new = jnp.concatenate([krnew_ref[:, :rope], jnp.zeros((pad, rope), F32)], axis=0).astype(BF16)
    rs_new = jnp.concatenate([rsnew_ref[...], jnp.ones((pad, V7X_LANES), F32)], axis=0)
    s = lax.dot_general(c_new, qa, dn_last, preferred_element_type=F32) * rs_new
    s = s + lax.dot_general(kr_new, qr, dn_last, preferred_element_type=F32)
    key = lax.broadcasted_iota(jnp.int32, s.shape, 0)
    qry = lax.broadcasted_iota(jnp.int32, s.shape, 1) // n_heads
    s = jnp.where(key <= qry, s, -jnp.inf)
    m0 = jnp.max(s, axis=0, keepdims=True)
    p = jnp.exp2(s - m0)
    m_scr[...] = m0
    l_scr[...] = jnp.sum(p, axis=0, keepdims=True)
    acc_scr[...] = lax.dot_general(c_new, p.astype(BF16), dn_first, preferred_element_type=F32)

    def stage_a(slot):
        cb = cbuf[slot].astype(BF16)
        cb_scr[slot] = cb
        rs = _key_norm_factor(cb, wk_ref, n_heads, nope)
        s = lax.dot_general(cb, qa, dn_last, preferred_element_type=F32) * rs
        krb = krbuf[slot].astype(BF16)
        s_scr[slot] = s + lax.dot_general(krb, qr, (((0,), (1,)), ((), ())), preferred_element_type=F32)

    def stage_b(slot):
        s = s_scr[slot]
        m_old = m_scr[...]
        m_new = jnp.maximum(m_old, jnp.max(s, axis=0, keepdims=True))
        p = jnp.exp2(s - m_new)
        alpha = jnp.exp2(m_old - m_new)
        l_scr[...] = alpha * l_scr[...] + jnp.sum(p, axis=0, keepdims=True)
        pv = lax.dot_general(cb_scr[slot], p.astype(BF16), dn_first, preferred_element_type=F32)
        acc_scr[...] = alpha * acc_scr[...] + pv
        m_scr[...] = m_new

    for i in range(n_groups + 1):
        slot = i % 2
        if i + 1 < n_groups:
            fetch(b, i + 1, 1 - slot)
        elif i + 1 == n_groups:
            @pl.when(b + 1 < pl.num_programs(0))
            def _():
                fetch(b + 1, 0, 0)
        if i < n_groups:
            wait(b, i, slot)
            stage_a(slot)
        if i >= 1:
            stage_b(1 - slot)
    o_ref[...] = (acc_scr[...] / l_scr[...]).T


def _decode_attn(page_table, cache_ckv, cache_krope, layer, wk_perm, qa, qr, c_new, kr_new, rs_new, *, pg, dims):
    n_heads, _, kv_lora, nope, rope, _, _ = dims
    b, n_pages = page_table.shape
    page = cache_ckv.shape[2]
    n_new = c_new.shape[1]
    rows = n_new * n_heads
    n_groups = n_pages // pg
    keys = pg * page
    assert rows == V7X_LANES and n_pages % pg == 0 and n_groups % 2 == 0 and n_new <= 2 * V7X_SUBLANES
    krope_t = jnp.swapaxes(cache_krope, 2, 3)
    per_seq = lambda r, c: pl.BlockSpec((None, r, c), lambda i, pt: (i, 0, 0))
    hbm = pl.BlockSpec(memory_space=pl.ANY)
    grid_spec = pltpu.PrefetchScalarGridSpec(
        num_scalar_prefetch=1, grid=(b,),
        in_specs=[hbm, hbm, pl.BlockSpec(wk_perm.shape, lambda i, pt: (0, 0)),
                  per_seq(rows, kv_lora), per_seq(rows, V7X_LANES),
                  per_seq(n_new, kv_lora), per_seq(n_new, V7X_LANES), per_seq(n_new, V7X_LANES)],
        out_specs=per_seq(rows, kv_lora),
        scratch_shapes=[pltpu.VMEM((2, keys, kv_lora), F32), pltpu.VMEM((2, rope, keys), F32),
                        pltpu.VMEM((2, keys, kv_lora), BF16), pltpu.VMEM((2, keys, V7X_LANES), F32),
                        pltpu.VMEM((1, V7X_LANES), F32), pltpu.VMEM((1, V7X_LANES), F32),
                        pltpu.VMEM((kv_lora, V7X_LANES), F32),
                        pltpu.SemaphoreType.DMA((2,)), pltpu.SemaphoreType.DMA((2,))])
    return pl.pallas_call(
        functools.partial(_decode_attn_kernel, layer=layer, pg=pg, n_groups=n_groups, n_heads=n_heads, nope=nope,
                          rope=rope),
        grid_spec=grid_spec, out_shape=jax.ShapeDtypeStruct((b, rows, kv_lora), F32),
        compiler_params=_params("arbitrary"), name="mla_decode_attn",
    )(page_table, cache_ckv, krope_t, wk_perm, qa, qr, c_new, kr_new, rs_new)


def _unabsorb_kernel(ctx_ref, wv_ref, o_ref, *, n_pairs, kv_lora):
    for p in range(n_pairs):
        x = ctx_ref[:, 2 * kv_lora * p:2 * kv_lora * (p + 1)].astype(BF16)
        o_ref[:, V7X_LANES * p:V7X_LANES * (p + 1)] = jnp.dot(x, wv_ref[p], preferred_element_type=F32).astype(BF16)


def _unabsorb(ctx2d, wv2, *, tr):
    r = ctx2d.shape[0]
    n_pairs, k2, _ = wv2.shape
    return pl.pallas_call(
        functools.partial(_unabsorb_kernel, n_pairs=n_pairs, kv_lora=k2 // 2),
        grid=(r // tr,),
        in_specs=[pl.BlockSpec((tr, n_pairs * k2), lambda i: (i, 0)), _const_spec(wv2)],
        out_specs=pl.BlockSpec((tr, n_pairs * V7X_LANES), lambda i: (i, 0)),
        out_shape=jax.ShapeDtypeStruct((r, n_pairs * V7X_LANES), BF16),
        compiler_params=_params("parallel"), name="mla_unabsorb_v",
    )(ctx2d, wv2)


def _conv_prompt(g, w, b, buf_ref, carry):
    rows = g.shape[0]
    width = w.shape[0]
    buf_ref[0:V7X_SUBLANES, :] = carry
    buf_ref[V7X_SUBLANES:, :] = g
    y = b + g * w[width - 1:width, :]
    for s in range(1, width):
        y = y + buf_ref[V7X_SUBLANES - s:V7X_SUBLANES - s + rows, :] * w[width - 1 - s:width - s, :]
    return y


def _conv_decode(g, w, b, prev, t_idx):
    width = w.shape[0]
    rows = g.shape[0]
    y = b + g * w[width - 1:width, :]
    for s in range(1, width):
        sh = jnp.where(t_idx >= s, pltpu.roll(g, s, axis=0), pltpu.roll(prev, rows - V7X_SUBLANES + s, axis=0))
        y = y + sh * w[width - 1 - s:width - s, :]
    return y


def _ffn_kernel(*refs, decode, fc):
    if decode:
        x_ref, a_ref, wpre_ref, gn_ref, wup_ref, cw_ref, cb_ref, wd_ref, st_ref, y_ref, so_ref = refs
    else:
        (x_ref, a_ref, wpre_ref, gn_ref, wup_ref, cw_ref, cb_ref, wd_ref,
         y_ref, so_ref, carry_scr, buf_scr) = refs
    bb, tt, d = x_ref.shape
    rows = bb * tt
    dff = wd_ref.shape[0]
    a = a_ref[...].reshape(rows, a_ref.shape[2])
    x1 = x_ref[...].reshape(rows, d) + jnp.dot(a, wpre_ref[...], preferred_element_type=F32)
    xn = _rms(x1, gn_ref[...]).astype(BF16)
    if decode:
        t_idx = lax.broadcasted_iota(jnp.int32, (rows, fc), 0) % tt
    else:
        @pl.when(pl.program_id(1) == 0)
        def _():
            carry_scr[...] = jnp.zeros(carry_scr.shape, F32)

    acc = x1
    for c in range(dff // fc):
        cols = slice(c * fc, (c + 1) * fc)
        g = jnp.dot(xn, wup_ref[:, cols], preferred_element_type=F32)
        u = jnp.dot(xn, wup_ref[:, dff + c * fc:dff + (c + 1) * fc], preferred_element_type=F32)
        if decode:
            gc = _conv_decode(g, cw_ref[:, cols], cb_ref[:, cols], st_ref[:, :, cols].reshape(rows, fc), t_idx)
            so_ref[:, :, cols] = g.reshape(bb, tt, fc)
        else:
            gc = _conv_prompt(g, cw_ref[:, cols], cb_ref[:, cols], buf_scr.at[c], carry_scr[:, cols])
            last = g[rows - V7X_SUBLANES:, :]
            carry_scr[:, cols] = last
            so_ref[0, :, cols] = last
        h = (gc * _sigmoid(gc) * u).astype(BF16)
        acc = acc + jnp.dot(h, wd_ref[cols, :], preferred_element_type=F32)
    y_ref[...] = acc.reshape(bb, tt, d)


def _resident_spec(a):
    nd = a.ndim
    return pl.BlockSpec(a.shape, lambda *_: (0,) * nd, pipeline_mode=pl.Buffered(1))


def _ffn(x, a, w_pre, gn, wup, cw, cb, wd, state, *, bb, tt, fc):
    b, t, d = x.shape
    da = a.shape[2]
    dff = wd.shape[0]
    rows = bb * tt
    decode = state is not None
    tok = lambda c: pl.BlockSpec((bb, tt, c), lambda i, j: (i, j, 0))
    weights = [w_pre, gn, wup, cw, cb, wd]
    in_specs = [tok(d), tok(da)] + [_resident_spec(w) for w in weights]
    args = [x, a] + weights
    scratch = []
    if decode:
        in_specs.append(pl.BlockSpec((bb, tt, dff), lambda i, j: (i, 0, 0)))
        args.append(state)
    else:
        assert bb == 1
        scratch = [pltpu.VMEM((V7X_SUBLANES, dff), F32), pltpu.VMEM((dff // fc, V7X_SUBLANES + rows, fc), F32)]
    so_spec = pl.BlockSpec((bb, None, V7X_SUBLANES, dff), lambda i, j: (i, j, 0, 0))
    y, so = pl.pallas_call(
        functools.partial(_ffn_kernel, decode=decode, fc=fc),
        grid=(b // bb, t // tt), in_specs=in_specs,
        out_specs=[tok(d), so_spec],
        out_shape=[jax.ShapeDtypeStruct((b, t, d), F32),
                   jax.ShapeDtypeStruct((b, t // tt, V7X_SUBLANES, dff), F32)],
        scratch_shapes=scratch,
        compiler_params=_params("parallel", "arbitrary"), name="conv_ffn",
    )(*args)
    return y, so[:, -1]


def _rec_kernel(*refs, decode):
    if decode:
        (x_ref, gn_ref, win_ref, cw_ref, cb_ref, wgate_ref, ba_ref, bi_ref, lam_ref, st_ref, h0_ref,
         a_ref, co_ref, ho_ref, a_scr, b_scr) = refs
    else:
        (x_ref, gn_ref, win_ref, cw_ref, cb_ref, wgate_ref, ba_ref, bi_ref, lam_ref,
         a_ref, co_ref, ho_ref, a_scr, b_scr, h_scr, carry_scr, hc_scr, buf_scr) = refs
    bb, tt, d = x_ref.shape
    rows = bb * tt
    w = cw_ref.shape[1]
    t = pl.program_id(1)
    xn = _rms(x_ref[...].reshape(rows, d), gn_ref[...]).astype(BF16)
    z = jnp.dot(xn, win_ref[...], preferred_element_type=F32)
    xr = z[:, :w]
    gin = z[:, w:]
    gate = 0.5 * gin * (1.0 + jnp.tanh(0.7978845608028654 * (gin + 0.044715 * (gin * gin * gin))))
    t_idx = lax.broadcasted_iota(jnp.int32, (rows, w), 0) % V7X_SUBLANES
    if decode:
        xc = _conv_decode(xr, cw_ref[...], cb_ref[...], st_ref[...].reshape(rows, w), t_idx)
        co_ref[...] = xr.reshape(bb, tt, w)
    else:
        @pl.when(t == 0)
        def _():
            carry_scr[...] = jnp.zeros(carry_scr.shape, F32)
            hc_scr[...] = jnp.zeros(hc_scr.shape, F32)

        xc = _conv_prompt(xr, cw_ref[...], cb_ref[...], buf_scr, carry_scr[...])
        last = xr[rows - V7X_SUBLANES:, :]
        carry_scr[...] = last
        co_ref[0] = last

    n_blk = wgate_ref.shape[0]
    kb = wgate_ref.shape[1]
    xcb = xc.astype(BF16)
    ra, ri = [], []
    for p in range(n_blk):
        zz = jnp.dot(xcb[:, kb * p:kb * (p + 1)], wgate_ref[p], preferred_element_type=F32)
        ra.append(zz[:, :kb])
        ri.append(zz[:, kb:])
    r = _sigmoid(jnp.concatenate(ra, axis=1) + ba_ref[...])
    gi = _sigmoid(jnp.concatenate(ri, axis=1) + bi_ref[...])
    neg_lam = -lam_ref[...]
    softplus = jnp.maximum(neg_lam, 0.0) + jnp.log(1.0 + jnp.exp(-jnp.abs(neg_lam)))
    av = jnp.exp(r * (-LRU_C * softplus))
    bv = jnp.sqrt(1.0 - av * av) * (gi * xc)

    grp3 = (rows // V7X_SUBLANES, V7X_SUBLANES, w)
    av, bv = av.reshape(grp3), bv.reshape(grp3)
    t3 = lax.broadcasted_iota(jnp.int32, grp3, 1)
    for s in (1, 2, 4):
        keep = t3 >= s
        bv = jnp.where(keep, av * pltpu.roll(bv, s, axis=1) + bv, bv)
        av = jnp.where(keep, av * pltpu.roll(av, s, axis=1), av)
    av, bv = av.reshape(rows, w), bv.reshape(rows, w)
    if decode:
        hs = av * h0_ref[...].reshape(rows, w) + bv
        ho_ref[...] = hs.reshape(bb, tt, w)
    else:
        a_scr[...] = av
        b_scr[...] = bv

        def group(i, h_in):
            sl = pl.ds(pl.multiple_of(i * V7X_SUBLANES, V7X_SUBLANES), V7X_SUBLANES)
            hg = a_scr[sl, :] * h_in + b_scr[sl, :]
            h_scr[sl, :] = hg
            return jnp.broadcast_to(hg[V7X_SUBLANES - 1:, :], (V7X_SUBLANES, w))

        h_last = lax.fori_loop(0, rows // V7X_SUBLANES, group, hc_scr[...])
        hc_scr[...] = h_last
        ho_ref[0] = h_last
        hs = h_scr[...]
    a_ref[...] = (hs * gate).astype(BF16).reshape(bb, tt, w)


def _rec(x, gn, w_in, cw, cb, w_gate, b_a, b_i, lam, state, h0, *, bb, tt):
    b, t, d = x.shape
    w = cw.shape[1]
    rows = bb * tt
    decode = state is not None
    tok = lambda c: pl.BlockSpec((bb, tt, c), lambda i, j: (i, j, 0))
    consts = (gn, w_in, cw, cb, w_gate, b_a, b_i, lam)
    in_specs = [tok(d)] + [_const_spec(a) for a in consts]
    args = [x, *consts]
    scratch = [pltpu.VMEM((rows, w), F32), pltpu.VMEM((rows, w), F32)]
    grp = pl.BlockSpec((bb, V7X_SUBLANES, w), lambda i, j: (i, 0, 0))
    if decode:
        in_specs += [grp, grp]
        args += [state, h0]
    else:
        assert bb == 1
        scratch += [pltpu.VMEM((rows, w), F32), pltpu.VMEM((V7X_SUBLANES, w), F32), pltpu.VMEM((V7X_SUBLANES, w), F32),
                    pltpu.VMEM((V7X_SUBLANES + rows, w), F32)]
    return pl.pallas_call(
        functools.partial(_rec_kernel, decode=decode),
        grid=(b // bb, t // tt), in_specs=in_specs,
        out_specs=[tok(w), grp, grp],
        out_shape=[jax.ShapeDtypeStruct((b, t, w), BF16), jax.ShapeDtypeStruct((b, V7X_SUBLANES, w), F32),
                   jax.ShapeDtypeStruct((b, V7X_SUBLANES, w), F32)],
        scratch_shapes=scratch,
        compiler_params=_params("parallel", "arbitrary"), name="rglru_mixer",
    )(*args)


def _pad_state(buf):
    return jnp.pad(buf, ((0, 0), (V7X_SUBLANES - buf.shape[1], 0), (0, 0)))


class _Tiles(NamedTuple):
    prompt_rows: int
    decode_seqs: int
    attn_q_rows: int
    attn_heads: int
    ffn_chunk: int
    pages_per_group: int


def _tiles(tp, bs, ts, n_heads, vh, dff, n_pages):
    heads = 4 if n_heads % 4 == 0 and (4 * vh) % V7X_LANES == 0 else 2
    return _Tiles(prompt_rows=min(tp, 512), decode_seqs=min(bs, 256 // ts), attn_q_rows=min(tp, 256),
                  attn_heads=heads, ffn_chunk=min(dff, 1024), pages_per_group=min(n_pages, 16))


def kernel(x_prompt, x_sample, cache_ckv, cache_krope, page_table, state_lru_h, state_lru_conv, state_ffn_conv,
           norm_mix, norm_ffn, attn_w_down, attn_g_q_lora, attn_w_uq, attn_g_kv_lora, attn_g_qn, attn_g_qr,
           attn_g_kn, attn_g_kr, attn_w_uk, attn_w_uv, attn_w_o, rec_w_in, rec_conv_w, rec_conv_b, rec_w_a,
           rec_b_a, rec_w_i, rec_b_i, rec_lambda, rec_w_out, ffn_w_up, ffn_conv_w, ffn_conv_b, ffn_w_down):
    bp, tp, d = x_prompt.shape
    bs, ts, _ = x_sample.shape
    depth = norm_mix.shape[0]
    n_mixers = 2
    kv_lora, n_heads, nope = attn_w_uk.shape[1:]
    vh = attn_w_uv.shape[3]
    rope = attn_g_qr.shape[1]
    q_lora = attn_g_q_lora.shape[1]
    half = rope // 2
    page = cache_ckv.shape[2]
    past = page_table.shape[1] * page
    dff = ffn_conv_w.shape[2]
    lru_w = rec_conv_w.shape[2]
    assert ts == V7X_SUBLANES and rope + nope + half <= V7X_LANES and vh < V7X_LANES and n_heads % 2 == 0
    scale = float((nope + rope) ** -0.5) * math.log2(math.e)
    dims = (n_heads, q_lora, kv_lora, nope, rope, vh, scale)
    pad_l = V7X_LANES - rope - nope
    row = lambda v: v.reshape(1, -1).astype(F32)

    def rope_tables(pos):
        inv = ROPE_THETA ** (-jnp.arange(0, rope, 2, dtype=F32) / rope)
        ang = pos.astype(F32)[:, None] * inv[None, :]
        cos, sin = jnp.cos(ang), jnp.sin(ang)
        n = pos.shape[0]
        ones = jnp.ones((n, V7X_LANES - rope - half), F32)
        return (jnp.concatenate([cos, cos, ones, jnp.zeros((n, half), F32)], axis=1),
                jnp.concatenate([-sin, sin, jnp.zeros((n, V7X_LANES - rope), F32)], axis=1))

    tabs_p = rope_tables(jnp.arange(tp))
    tabs_s = rope_tables(past + jnp.arange(ts))

    xp, xs = x_prompt, x_sample
    a_p = a_s = w_pre = None
    ckv_p, kr_p, ckv_s, kr_s = [], [], [], []
    lh_p, lh_s, lc_p, lc_s = [], [], [], []
    fc_p, fc_s = [], []
    tiles = _tiles(tp, bs, ts, n_heads, vh, dff, page_table.shape[1])
    for i in range(depth):
        j = i // n_mixers
        gmix = row(norm_mix[i])
        if i % n_mixers == 0:
            wdn = attn_w_down[j]
            wd = jnp.concatenate([wdn, jnp.zeros((d, V7X_LANES - rope - half), F32), wdn[:, -half:]],
                                 axis=1).astype(BF16)
            gkr = jnp.concatenate([attn_g_kr[j], jnp.zeros((V7X_LANES - rope - half,), F32),
                                   attn_g_kr[j][half:]]).reshape(1, -1)
            wq = attn_w_uq[j].reshape(q_lora, n_heads, nope + rope)
            wuq = jnp.concatenate([wq[:, :, nope:], wq[:, :, :nope], jnp.zeros((q_lora, n_heads, pad_l - half), F32),
                                   wq[:, :, nope + half:]],
                                  axis=2).reshape(q_lora, n_heads * V7X_LANES).astype(BF16)
            gq = jnp.concatenate([attn_g_qr[j], attn_g_qn[j], jnp.zeros((pad_l - half,), F32),
                                  attn_g_qr[j][half:]]).reshape(1, -1)
            wk = attn_w_uk[j]
            wuk = jnp.concatenate([jnp.zeros((kv_lora, n_heads, rope), F32), wk,
                                   jnp.zeros((kv_lora, n_heads, pad_l), F32)],
                                  axis=2).reshape(kv_lora, n_heads * V7X_LANES).astype(BF16)
            gk = jnp.concatenate([jnp.zeros((rope,), F32), attn_g_kn[j], jnp.zeros((pad_l,), F32)]).reshape(1, -1)
            wuv = jnp.concatenate([attn_w_uv[j], jnp.zeros((kv_lora, n_heads, V7X_LANES - vh), F32)],
                                  axis=2).reshape(kv_lora, n_heads * V7X_LANES).astype(BF16)
            vone = jnp.tile((jnp.arange(V7X_LANES) == vh).astype(F32), n_heads).reshape(1, -1)
            proj_w = (gmix, wd, row(attn_g_q_lora[j]), row(attn_g_kv_lora[j]), gkr, wuq, gq, wuk, gk, wuv, vone)

            q_p, c_p, krb_p, k_p, v_p = _mla_proj(xp, *proj_w, *tabs_p, bb=1, tt=tiles.prompt_rows, dims=dims,
                                                  with_kv=True)
            wk_perm = jnp.transpose(wk, (0, 2, 1)).reshape(kv_lora, nope * n_heads).astype(BF16)
            proj_s = proj_w[:7] + (wk_perm,) + proj_w[8:]
            q_s, c_s, krb_s, rs_s = _mla_proj(xs, *proj_s, *tabs_s, bb=tiles.decode_seqs, tt=ts, dims=dims,
                                              with_kv=False)
            a_p = _flash(q_p, k_p, v_p, tq=tiles.attn_q_rows, n_heads=n_heads, vh=vh, hps=tiles.attn_heads)

            wa = jnp.transpose(wk * attn_g_kn[j][None, None, :], (1, 2, 0))
            wa = jnp.concatenate([jnp.zeros((n_heads, rope, kv_lora), F32), wa,
                                  jnp.zeros((n_heads, pad_l, kv_lora), F32)], axis=1).astype(BF16)
            qa = _absorb(q_s.reshape(bs * ts, n_heads * V7X_LANES), wa, tr=tiles.decode_seqs * ts)
            qa = qa.reshape(bs, ts * n_heads, kv_lora)
            qr = q_s.reshape(bs, ts * n_heads, V7X_LANES)
            ctx = _decode_attn(page_table, cache_ckv, cache_krope, j, wk_perm, qa, qr, c_s, krb_s, rs_s,
                               pg=tiles.pages_per_group, dims=dims)
            wv = attn_w_uv[j]
            zer = jnp.zeros((kv_lora, n_heads // 2, vh), F32)
            wv2 = jnp.concatenate([jnp.concatenate([wv[:, 0::2], zer], axis=2),
                                   jnp.concatenate([zer, wv[:, 1::2]], axis=2)], axis=0)
            wv2 = jnp.transpose(wv2, (1, 0, 2)).astype(BF16)
            a_s = _unabsorb(ctx.reshape(bs * ts, n_heads * kv_lora), wv2, tr=tiles.decode_seqs * ts)
            a_s = a_s.reshape(bs, ts, n_heads * vh)
            w_pre = attn_w_o[j].astype(BF16)
            ckv_p.append(c_p)
            kr_p.append(krb_p[:, :, :rope])
            ckv_s.append(c_s)
            kr_s.append(krb_s[:, :, :rope])
        else:
            n_blk, blk = rec_w_a.shape[1:3]
            assert n_blk % 2 == 0

            def pair(wm):
                z = jnp.zeros((n_blk // 2, blk, blk), F32)
                return jnp.concatenate([jnp.concatenate([wm[0::2], z], axis=2),
                                        jnp.concatenate([z, wm[1::2]], axis=2)], axis=1)

            w_gate = jnp.concatenate([pair(rec_w_a[j]), pair(rec_w_i[j])], axis=2).astype(BF16)
            rec_w = (gmix, rec_w_in[j].astype(BF16), rec_conv_w[j], row(rec_conv_b[j]), w_gate, row(rec_b_a[j]),
                     row(rec_b_i[j]), row(rec_lambda[j]))
            a_p, cv_p, h_p = _rec(xp, *rec_w, None, None, bb=1, tt=tiles.prompt_rows)
            h0 = jnp.broadcast_to(state_lru_h[j][:, None, :], (bs, ts, lru_w))
            a_s, cv_s, h_s = _rec(xs, *rec_w, _pad_state(state_lru_conv[j]), h0, bb=tiles.decode_seqs, tt=ts)
            w_pre = rec_w_out[j].astype(BF16)
            kw = rec_conv_w.shape[1] - 1
            lh_p.append(h_p[:, V7X_SUBLANES - 1])
            lh_s.append(h_s[:, V7X_SUBLANES - 1])
            lc_p.append(cv_p[:, V7X_SUBLANES - kw:])
            lc_s.append(cv_s[:, V7X_SUBLANES - kw:])
        ffn_w = (w_pre, row(norm_ffn[i]), ffn_w_up[i].astype(BF16), ffn_conv_w[i], row(ffn_conv_b[i]),
                 ffn_w_down[i].astype(BF16))
        xp, fb_p = _ffn(xp, a_p, *ffn_w, None, bb=1, tt=tiles.prompt_rows, fc=tiles.ffn_chunk)
        xs, fb_s = _ffn(xs, a_s, *ffn_w, _pad_state(state_ffn_conv[i]), bb=tiles.decode_seqs, tt=ts,
                        fc=tiles.ffn_chunk)
        kf = ffn_conv_w.shape[1] - 1
        fc_p.append(fb_p[:, V7X_SUBLANES - kf:])
        fc_s.append(fb_s[:, V7X_SUBLANES - kf:])
    return (xp, xs, jnp.stack(ckv_p), jnp.stack(kr_p), jnp.stack(ckv_s), jnp.stack(kr_s),
            jnp.stack(lh_p), jnp.stack(lh_s), jnp.stack(lc_p), jnp.stack(lc_s), jnp.stack(fc_p), jnp.stack(fc_s))
```

```python
import functools
import math
from typing import NamedTuple

import jax
import jax.numpy as jnp
from jax import lax
from jax.experimental import pallas as pl
from jax.experimental.pallas import tpu as pltpu

EPS = 1e-6
ROPE_THETA = 10000.0
LRU_C = 8.0
SCORE_EXP2_LIMIT = 64.0

V7X_LANES = 128
V7X_SUBLANES = 8
V7X_VMEM_BYTES = 64 * 1024 * 1024
VMEM_LIMIT = V7X_VMEM_BYTES * 7 // 8

F32 = jnp.float32
BF16 = jnp.bfloat16


def _params(*sem):
    return pltpu.CompilerParams(dimension_semantics=sem, vmem_limit_bytes=VMEM_LIMIT)


def _const_spec(a):
    nd = a.ndim
    return pl.BlockSpec(a.shape, lambda *_: (0,) * nd)


def _rms(x, g):
    ms = jnp.mean(x * x, axis=-1, keepdims=True)
    return x * lax.rsqrt(ms + EPS) * g


def _sigmoid(x):
    return 1.0 / (1.0 + jnp.exp(-x))


def _rope(y, cos, sin, half):
    return y * cos + pltpu.roll(y, half, axis=1) * sin


def _tile_rows(tab_ref, bb):
    t = tab_ref[...]
    if bb == 1:
        return t
    return jnp.broadcast_to(t[None], (bb,) + t.shape).reshape(bb * t.shape[0], t.shape[1])


def _key_norm_factor(cb, wk_ref, n_heads, nope):
    k = jnp.dot(cb, wk_ref[...], preferred_element_type=F32)
    ksq = k * k
    part = ksq[:, :V7X_LANES]
    for i in range(1, wk_ref.shape[1] // V7X_LANES):
        part = part + ksq[:, V7X_LANES * i:V7X_LANES * (i + 1)]
    shift = V7X_LANES // 2
    while shift >= n_heads:
        part = part + pltpu.roll(part, shift, axis=1)
        shift //= 2
    return lax.rsqrt(part / nope + EPS)


def _mla_proj_kernel(x_ref, gmix_ref, wd_ref, gql_ref, gkv_ref, gkr_ref, wuq_ref, gq_ref, wuk_ref, gk_ref, wuv_ref,
                     vone_ref, cos_ref, sin_ref, *out_refs, n_heads, q_lora, kv_lora, nope, rope, scale, with_kv):
    if with_kv:
        q_ref, ckv_ref, kr_ref, k_ref, v_ref = out_refs
    else:
        q_ref, ckv_ref, kr_ref, rs_ref = out_refs
    bb, tt, d = x_ref.shape
    rows = bb * tt
    half = rope // 2
    xn = _rms(x_ref[...].reshape(rows, d), gmix_ref[...])
    dd = jnp.dot(xn.astype(BF16), wd_ref[...], preferred_element_type=F32)
    cq = _rms(dd[:, :q_lora], gql_ref[...])
    ckv = _rms(dd[:, q_lora:q_lora + kv_lora], gkv_ref[...])
    kr = dd[:, q_lora + kv_lora:]
    cos = _tile_rows(cos_ref, bb)
    sin = _tile_rows(sin_ref, bb)
    lane = lax.broadcasted_iota(jnp.int32, (1, V7X_LANES), 1)
    is_rope = lane < rope
    is_nope = jnp.logical_and(lane >= rope, lane < rope + nope)

    kr_ss = jnp.sum(jnp.where(is_rope, kr * kr, 0.0), axis=-1, keepdims=True)
    krr = _rope(kr * lax.rsqrt(kr_ss / rope + EPS) * gkr_ref[...], cos, sin, half)
    kr_ref[...] = krr.reshape(bb, tt, V7X_LANES)
    ckv_ref[...] = ckv.reshape(bb, tt, kv_lora)

    q = jnp.dot(cq.astype(BF16), wuq_ref[...], preferred_element_type=F32)
    ckv_b = ckv.astype(BF16)
    if with_kv:
        kk = jnp.dot(ckv_b, wuk_ref[...], preferred_element_type=F32)
        vv = jnp.dot(ckv_b, wuv_ref[...], preferred_element_type=F32) + vone_ref[...]
        v_ref[...] = vv.astype(BF16).reshape(v_ref.shape)
    else:
        rs_ref[...] = _key_norm_factor(ckv_b, wuk_ref, n_heads, nope).reshape(bb, tt, V7X_LANES)
    for h in range(n_heads):
        sl = slice(V7X_LANES * h, V7X_LANES * (h + 1))
        qb = q[:, sl]
        sq = qb * qb
        ss_r = jnp.sum(jnp.where(is_rope, sq, 0.0), axis=-1, keepdims=True)
        ss_n = jnp.sum(jnp.where(is_nope, sq, 0.0), axis=-1, keepdims=True)
        rs = jnp.where(is_nope, lax.rsqrt(ss_n / nope + EPS), lax.rsqrt(ss_r / rope + EPS))
        y = _rope(qb * rs * gq_ref[...], cos, sin, half)
        q_ref[:, :, sl] = (y * scale).astype(BF16).reshape(bb, tt, V7X_LANES)
        if with_kv:
            kb = kk[:, sl]
            ss_k = jnp.sum(kb * kb, axis=-1, keepdims=True)
            kn = kb * lax.rsqrt(ss_k / nope + EPS) * gk_ref[...]
            k_ref[:, :, sl] = (kn + krr).astype(BF16).reshape(bb, tt, V7X_LANES)


def _mla_proj(x, gmix, wd, gql, gkv, gkr, wuq, gq, wuk, gk, wuv, vone, cos, sin, *, bb, tt, dims, with_kv):
    n_heads, q_lora, kv_lora, nope, rope, vh, scale = dims
    b, t, d = x.shape
    hl = n_heads * V7X_LANES
    tok = lambda c: pl.BlockSpec((bb, tt, c), lambda i, j: (i, j, 0))
    tab = pl.BlockSpec((tt, V7X_LANES), lambda i, j: (j, 0))
    out_shape = [jax.ShapeDtypeStruct((b, t, hl), BF16), jax.ShapeDtypeStruct((b, t, kv_lora), F32),
                 jax.ShapeDtypeStruct((b, t, V7X_LANES), F32)]
    out_specs = [tok(hl), tok(kv_lora), tok(V7X_LANES)]
    if with_kv:
        out_shape += [jax.ShapeDtypeStruct((b, t, hl), BF16), jax.ShapeDtypeStruct((b, t, hl), BF16)]
        out_specs += [tok(hl), tok(hl)]
    else:
        out_shape.append(jax.ShapeDtypeStruct((b, t, V7X_LANES), F32))
        out_specs.append(tok(V7X_LANES))
    consts = (gmix, wd, gql, gkv, gkr, wuq, gq, wuk, gk, wuv, vone)
    return pl.pallas_call(
        functools.partial(_mla_proj_kernel, n_heads=n_heads, q_lora=q_lora, kv_lora=kv_lora, nope=nope, rope=rope,
                          scale=scale, with_kv=with_kv),
        grid=(b // bb, t // tt),
        in_specs=[tok(d)] + [_const_spec(a) for a in consts] + [tab, tab],
        out_specs=out_specs, out_shape=out_shape,
        compiler_params=_params("parallel", "parallel"), name="mla_proj",
    )(x, *consts, cos, sin)


def _attn_block(q, k_ref, v_ref, lanes, vh, past, tq, bounded):
    dn = (((1,), (1,)), ((), ()))
    row = lax.broadcasted_iota(jnp.int32, (tq, tq), 0)
    col = lax.broadcasted_iota(jnp.int32, (tq, tq), 1)
    s_d = lax.dot_general(q, k_ref[0, past:past + tq, lanes], dn, preferred_element_type=F32)
    s_d = jnp.where(col <= row, s_d, -jnp.inf)
    if past:
        s_p = lax.dot_general(q, k_ref[0, :past, lanes], dn, preferred_element_type=F32)
    if not bounded:
        m = jnp.max(s_d, axis=1, keepdims=True)
        if past:
            m = jnp.maximum(m, jnp.max(s_p, axis=1, keepdims=True))
            s_p = s_p - m
        s_d = s_d - m
    acc = jnp.dot(jnp.exp2(s_d).astype(BF16), v_ref[0, past:past + tq, lanes], preferred_element_type=F32)
    if past:
        acc = acc + jnp.dot(jnp.exp2(s_p).astype(BF16), v_ref[0, :past, lanes], preferred_element_type=F32)
    return acc[:, :vh] / acc[:, vh:vh + 1]


def _flash_kernel(bounded_ref, q_ref, k_ref, v_ref, o_ref, *, tq, vh, n_q, hps):
    qi = pl.program_id(2)
    is_bounded = bounded_ref[0] != 0
    for n in range(n_q):
        for bounded in (True, False):
            @pl.when(jnp.logical_and(qi == n, is_bounded == bounded))
            def _(n=n, bounded=bounded):
                outs = []
                for hh in range(hps):
                    lanes = slice(V7X_LANES * hh, V7X_LANES * (hh + 1))
                    outs.append(_attn_block(q_ref[0, :, lanes], k_ref, v_ref, lanes, vh, n * tq, tq, bounded))
                o_ref[0] = jnp.concatenate(outs, axis=1).astype(BF16)


def _flash(bounded, q, k, v, *, tq, n_heads, vh, hps):
    b, t, _ = q.shape
    grid_spec = pltpu.PrefetchScalarGridSpec(
        num_scalar_prefetch=1, grid=(b, n_heads // hps, t // tq),
        in_specs=[pl.BlockSpec((1, tq, hps * V7X_LANES), lambda i, h, j, f: (i, j, h)),
                  pl.BlockSpec((1, t, hps * V7X_LANES), lambda i, h, j, f: (i, 0, h)),
                  pl.BlockSpec((1, t, hps * V7X_LANES), lambda i, h, j, f: (i, 0, h))],
        out_specs=pl.BlockSpec((1, tq, hps * vh), lambda i, h, j, f: (i, j, h)))
    return pl.pallas_call(
        functools.partial(_flash_kernel, tq=tq, vh=vh, n_q=t // tq, hps=hps),
        grid_spec=grid_spec, out_shape=jax.ShapeDtypeStruct((b, t, n_heads * vh), BF16),
        compiler_params=_params("parallel", "parallel", "arbitrary"), name="mla_prompt_attn",
    )(bounded, q, k, v)


def _absorb_kernel(q_ref, wa_ref, qa_ref, *, n_heads, kv_lora):
    for h in range(n_heads):
        qh = q_ref[:, V7X_LANES * h:V7X_LANES * (h + 1)]
        qa_ref[:, kv_lora * h:kv_lora * (h + 1)] = jnp.dot(qh, wa_ref[h], preferred_element_type=F32).astype(BF16)


def _absorb(q2d, wa, *, tr):
    r = q2d.shape[0]
    n_heads, _, kv_lora = wa.shape
    return pl.pallas_call(
        functools.partial(_absorb_kernel, n_heads=n_heads, kv_lora=kv_lora),
        grid=(r // tr,),
        in_specs=[pl.BlockSpec((tr, n_heads * V7X_LANES), lambda i: (i, 0)), _const_spec(wa)],
        out_specs=pl.BlockSpec((tr, n_heads * kv_lora), lambda i: (i, 0)),
        out_shape=jax.ShapeDtypeStruct((r, n_heads * kv_lora), BF16),
        compiler_params=_params("parallel"), name="mla_absorb_q",
    )(q2d, wa)


def _decode_attn_kernel(pt_ref, ckv_hbm, krt_hbm, wk_ref, qa_ref, qr_ref, cnew_ref, krnew_ref, rsnew_ref, o_ref,
                        cbuf, krbuf, cb_scr, s_scr, m_scr, l_scr, acc_scr, csem, ksem,
                        *, layer, pg, n_groups, n_heads, nope, rope):
    b = pl.program_id(0)
    page = ckv_hbm.shape[2]
    kv_lora = ckv_hbm.shape[3]
    dn_last = (((1,), (1,)), ((), ()))
    dn_first = (((0,), (0,)), ((), ()))
    qa = qa_ref[...]
    qr = qr_ref[:, :rope]

    def copies(seq, grp, slot):
        out = []
        for s in range(pg):
            pid = pt_ref[seq, grp * pg + s]
            rows = pl.ds(s * page, page)
            out.append(pltpu.make_async_copy(ckv_hbm.at[layer, pid], cbuf.at[slot, rows, :], csem.at[slot]))
            out.append(pltpu.make_async_copy(krt_hbm.at[layer, pid], krbuf.at[slot, :, rows], ksem.at[slot]))
        return out

    def fetch(seq, grp, slot):
        for cp in copies(seq, grp, slot):
            cp.start()

    def wait(seq, grp, slot):
        for cp in copies(seq, grp, slot):
            cp.wait()

    @pl.when(b == 0)
    def _():
        fetch(0, 0, 0)

    n_new = cnew_ref.shape[0]
    pad = 2 * V7X_SUBLANES - n_new
    c_new = jnp.concatenate([cnew_ref[...], jnp.zeros((pad, kv_lora), F32)], axis=0).astype(BF16)
    kr_new = jnp.concatenate([krnew_ref[:, :rope], jnp.zeros((pad, rope), F32)], axis=0).astype(BF16)
    rs_new = jnp.concatenate([rsnew_ref[...], jnp.ones((pad, V7X_LANES), F32)], axis=0)
    s = lax.dot_general(c_new, qa, dn_last, preferred_element_type=F32) * rs_new
    s = s + lax.dot_general(kr_new, qr, dn_last, preferred_element_type=F32)
    key = lax.broadcasted_iota(jnp.int32, s.shape, 0)
    qry = lax.broadcasted_iota(jnp.int32, s.shape, 1) // n_heads
    s = jnp.where(key <= qry, s, -jnp.inf)
    m0 = jnp.max(s, axis=0, keepdims=True)
    p = jnp.exp2(s - m0)
    m_scr[...] = m0
    l_scr[...] = jnp.sum(p, axis=0, keepdims=True)
    acc_scr[...] = lax.dot_general(c_new, p.astype(BF16), dn_first, preferred_element_type=F32)

    def stage_a(slot):
        cb = cbuf[slot].astype(BF16)
        cb_scr[slot] = cb
        rs = _key_norm_factor(cb, wk_ref, n_heads, nope)
        s = lax.dot_general(cb, qa, dn_last, preferred_element_type=F32) * rs
        krb = krbuf[slot].astype(BF16)
        s_scr[slot] = s + lax.dot_general(krb, qr, (((0,), (1,)), ((), ())), preferred_element_type=F32)

    def stage_b(slot):
        s = s_scr[slot]
        m_old = m_scr[...]
        m_new = jnp.maximum(m_old, jnp.max(s, axis=0, keepdims=True))
        p = jnp.exp2(s - m_new)
        alpha = jnp.exp2(m_old - m_new)
        l_scr[...] = alpha * l_scr[...] + jnp.sum(p, axis=0, keepdims=True)
        pv = lax.dot_general(cb_scr[slot], p.astype(BF16), dn_first, preferred_element_type=F32)
        acc_scr[...] = alpha * acc_scr[...] + pv
        m_scr[...] = m_new

    for i in range(n_groups + 1):
        slot = i % 2
        if i + 1 < n_groups:
            fetch(b, i + 1, 1 - slot)
        elif i + 1 == n_groups:
            @pl.when(b + 1 < pl.num_programs(0))
            def _():
                fetch(b + 1, 0, 0)
        if i < n_groups:
            wait(b, i, slot)
            stage_a(slot)
        if i >= 1:
            stage_b(1 - slot)
    o_ref[...] = (acc_scr[...] / l_scr[...]).T


def _decode_attn(page_table, cache_ckv, cache_krope, layer, wk_perm, qa, qr, c_new, kr_new, rs_new, *, pg, dims):
    n_heads, _, kv_lora, nope, rope, _, _ = dims
    b, n_pages = page_table.shape
    page = cache_ckv.shape[2]
    n_new = c_new.shape[1]
    rows = n_new * n_heads
    n_groups = n_pages // pg
    keys = pg * page
    assert rows == V7X_LANES and n_pages % pg == 0 and n_groups % 2 == 0 and n_new <= 2 * V7X_SUBLANES
    krope_t = jnp.swapaxes(cache_krope, 2, 3)
    per_seq = lambda r, c: pl.BlockSpec((None, r, c), lambda i, pt: (i, 0, 0))
    hbm = pl.BlockSpec(memory_space=pl.ANY)
    grid_spec = pltpu.PrefetchScalarGridSpec(
        num_scalar_prefetch=1, grid=(b,),
        in_specs=[hbm, hbm, pl.BlockSpec(wk_perm.shape, lambda i, pt: (0, 0)),
                  per_seq(rows, kv_lora), per_seq(rows, V7X_LANES),
                  per_seq(n_new, kv_lora), per_seq(n_new, V7X_LANES), per_seq(n_new, V7X_LANES)],
        out_specs=per_seq(rows, kv_lora),
        scratch_shapes=[pltpu.VMEM((2, keys, kv_lora), F32), pltpu.VMEM((2, rope, keys), F32),
                        pltpu.VMEM((2, keys, kv_lora), BF16), pltpu.VMEM((2, keys, V7X_LANES), F32),
                        pltpu.VMEM((1, V7X_LANES), F32), pltpu.VMEM((1, V7X_LANES), F32),
                        pltpu.VMEM((kv_lora, V7X_LANES), F32),
                        pltpu.SemaphoreType.DMA((2,)), pltpu.SemaphoreType.DMA((2,))])
    return pl.pallas_call(
        functools.partial(_decode_attn_kernel, layer=layer, pg=pg, n_groups=n_groups, n_heads=n_heads, nope=nope,
                          rope=rope),
        grid_spec=grid_spec, out_shape=jax.ShapeDtypeStruct((b, rows, kv_lora), F32),
        compiler_params=_params("arbitrary"), name="mla_decode_attn",
    )(page_table, cache_ckv, krope_t, wk_perm, qa, qr, c_new, kr_new, rs_new)


def _unabsorb_kernel(ctx_ref, wv_ref, o_ref, *, n_pairs, kv_lora):
    for p in range(n_pairs):
        x = ctx_ref[:, 2 * kv_lora * p:2 * kv_lora * (p + 1)].astype(BF16)
        o_ref[:, V7X_LANES * p:V7X_LANES * (p + 1)] = jnp.dot(x, wv_ref[p], preferred_element_type=F32).astype(BF16)


def _unabsorb(ctx2d, wv2, *, tr):
    r = ctx2d.shape[0]
    n_pairs, k2, _ = wv2.shape
    return pl.pallas_call(
        functools.partial(_unabsorb_kernel, n_pairs=n_pairs, kv_lora=k2 // 2),
        grid=(r // tr,),
        in_specs=[pl.BlockSpec((tr, n_pairs * k2), lambda i: (i, 0)), _const_spec(wv2)],
        out_specs=pl.BlockSpec((tr, n_pairs * V7X_LANES), lambda i: (i, 0)),
        out_shape=jax.ShapeDtypeStruct((r, n_pairs * V7X_LANES), BF16),
        compiler_params=_params("parallel"), name="mla_unabsorb_v",
    )(ctx2d, wv2)


def _conv_prompt(g, w, b, buf_ref, carry):
    rows = g.shape[0]
    width = w.shape[0]
    buf_ref[0:V7X_SUBLANES, :] = carry
    buf_ref[V7X_SUBLANES:, :] = g
    y = b + g * w[width - 1:width, :]
    for s in range(1, width):
        y = y + buf_ref[V7X_SUBLANES - s:V7X_SUBLANES - s + rows, :] * w[width - 1 - s:width - s, :]
    return y


def _conv_decode(g, w, b, prev, t_idx):
    width = w.shape[0]
    rows = g.shape[0]
    y = b + g * w[width - 1:width, :]
    for s in range(1, width):
        sh = jnp.where(t_idx >= s, pltpu.roll(g, s, axis=0), pltpu.roll(prev, rows - V7X_SUBLANES + s, axis=0))
        y = y + sh * w[width - 1 - s:width - s, :]
    return y


def _ffn_kernel(*refs, decode, fc):
    if decode:
        x_ref, a_ref, wpre_ref, gn_ref, wup_ref, cw_ref, cb_ref, wd_ref, st_ref, y_ref, so_ref = refs
    else:
        (x_ref, a_ref, wpre_ref, gn_ref, wup_ref, cw_ref, cb_ref, wd_ref,
         y_ref, so_ref, carry_scr, buf_scr) = refs
    bb, tt, d = x_ref.shape
    rows = bb * tt
    dff = wd_ref.shape[0]
    a = a_ref[...].reshape(rows, a_ref.shape[2])
    x1 = x_ref[...].reshape(rows, d) + jnp.dot(a, wpre_ref[...], preferred_element_type=F32)
    xn = _rms(x1, gn_ref[...]).astype(BF16)
    if decode:
        t_idx = lax.broadcasted_iota(jnp.int32, (rows, fc), 0) % tt
    else:
        @pl.when(pl.program_id(1) == 0)
        def _():
            carry_scr[...] = jnp.zeros(carry_scr.shape, F32)

    acc = x1
    for c in range(dff // fc):
        cols = slice(c * fc, (c + 1) * fc)
        g = jnp.dot(xn, wup_ref[:, cols], preferred_element_type=F32)
        u = jnp.dot(xn, wup_ref[:, dff + c * fc:dff + (c + 1) * fc], preferred_element_type=F32)
        if decode:
            gc = _conv_decode(g, cw_ref[:, cols], cb_ref[:, cols], st_ref[:, :, cols].reshape(rows, fc), t_idx)
            so_ref[:, :, cols] = g.reshape(bb, tt, fc)
        else:
            gc = _conv_prompt(g, cw_ref[:, cols], cb_ref[:, cols], buf_scr.at[c], carry_scr[:, cols])
            last = g[rows - V7X_SUBLANES:, :]
            carry_scr[:, cols] = last
            so_ref[0, :, cols] = last
        h = (gc * _sigmoid(gc) * u).astype(BF16)
        acc = acc + jnp.dot(h, wd_ref[cols, :], preferred_element_type=F32)
    y_ref[...] = acc.reshape(bb, tt, d)


def _resident_spec(a):
    nd = a.ndim
    return pl.BlockSpec(a.shape, lambda *_: (0,) * nd, pipeline_mode=pl.Buffered(1))


def _ffn(x, a, w_pre, gn, wup, cw, cb, wd, state, *, bb, tt, fc):
    b, t, d = x.shape
    da = a.shape[2]
    dff = wd.shape[0]
    rows = bb * tt
    decode = state is not None
    tok = lambda c: pl.BlockSpec((bb, tt, c), lambda i, j: (i, j, 0))
    weights = [w_pre, gn, wup, cw, cb, wd]
    in_specs = [tok(d), tok(da)] + [_resident_spec(w) for w in weights]
    args = [x, a] + weights
    scratch = []
    if decode:
        in_specs.append(pl.BlockSpec((bb, tt, dff), lambda i, j: (i, 0, 0)))
        args.append(state)
    else:
        assert bb == 1
        scratch = [pltpu.VMEM((V7X_SUBLANES, dff), F32), pltpu.VMEM((dff // fc, V7X_SUBLANES + rows, fc), F32)]
    so_spec = pl.BlockSpec((bb, None, V7X_SUBLANES, dff), lambda i, j: (i, j, 0, 0))
    y, so = pl.pallas_call(
        functools.partial(_ffn_kernel, decode=decode, fc=fc),
        grid=(b // bb, t // tt), in_specs=in_specs,
        out_specs=[tok(d), so_spec],
        out_shape=[jax.ShapeDtypeStruct((b, t, d), F32),
                   jax.ShapeDtypeStruct((b, t // tt, V7X_SUBLANES, dff), F32)],
        scratch_shapes=scratch,
        compiler_params=_params("parallel", "arbitrary"), name="conv_ffn",
    )(*args)
    return y, so[:, -1]


def _rec_kernel(*refs, decode):
    if decode:
        (x_ref, gn_ref, win_ref, cw_ref, cb_ref, wgate_ref, ba_ref, bi_ref, lam_ref, st_ref, h0_ref,
         a_ref, co_ref, ho_ref, a_scr, b_scr) = refs
    else:
        (x_ref, gn_ref, win_ref, cw_ref, cb_ref, wgate_ref, ba_ref, bi_ref, lam_ref,
         a_ref, co_ref, ho_ref, a_scr, b_scr, h_scr, carry_scr, hc_scr, buf_scr) = refs
    bb, tt, d = x_ref.shape
    rows = bb * tt
    w = cw_ref.shape[1]
    t = pl.program_id(1)
    xn = _rms(x_ref[...].reshape(rows, d), gn_ref[...]).astype(BF16)
    z = jnp.dot(xn, win_ref[...], preferred_element_type=F32)
    xr = z[:, :w]
    gin = z[:, w:]
    gate = 0.5 * gin * (1.0 + jnp.tanh(0.7978845608028654 * (gin + 0.044715 * (gin * gin * gin))))
    t_idx = lax.broadcasted_iota(jnp.int32, (rows, w), 0) % V7X_SUBLANES
    if decode:
        xc = _conv_decode(xr, cw_ref[...], cb_ref[...], st_ref[...].reshape(rows, w), t_idx)
        co_ref[...] = xr.reshape(bb, tt, w)
    else:
        @pl.when(t == 0)
        def _():
            carry_scr[...] = jnp.zeros(carry_scr.shape, F32)
            hc_scr[...] = jnp.zeros(hc_scr.shape, F32)

        xc = _conv_prompt(xr, cw_ref[...], cb_ref[...], buf_scr, carry_scr[...])
        last = xr[rows - V7X_SUBLANES:, :]
        carry_scr[...] = last
        co_ref[0] = last

    n_blk = wgate_ref.shape[0]
    kb = wgate_ref.shape[1]
    xcb = xc.astype(BF16)
    ra, ri = [], []
    for p in range(n_blk):
        zz = jnp.dot(xcb[:, kb * p:kb * (p + 1)], wgate_ref[p], preferred_element_type=F32)
        ra.append(zz[:, :kb])
        ri.append(zz[:, kb:])
    r = _sigmoid(jnp.concatenate(ra, axis=1) + ba_ref[...])
    gi = _sigmoid(jnp.concatenate(ri, axis=1) + bi_ref[...])
    neg_lam = -lam_ref[...]
    softplus = jnp.maximum(neg_lam, 0.0) + jnp.log(1.0 + jnp.exp(-jnp.abs(neg_lam)))
    av = jnp.exp(r * (-LRU_C * softplus))
    bv = jnp.sqrt(1.0 - av * av) * (gi * xc)

    grp3 = (rows // V7X_SUBLANES, V7X_SUBLANES, w)
    av, bv = av.reshape(grp3), bv.reshape(grp3)
    t3 = lax.broadcasted_iota(jnp.int32, grp3, 1)
    for s in (1, 2, 4):
        keep = t3 >= s
        bv = jnp.where(keep, av * pltpu.roll(bv, s, axis=1) + bv, bv)
        av = jnp.where(keep, av * pltpu.roll(av, s, axis=1), av)
    av, bv = av.reshape(rows, w), bv.reshape(rows, w)
    if decode:
        hs = av * h0_ref[...].reshape(rows, w) + bv
        ho_ref[...] = hs.reshape(bb, tt, w)
    else:
        a_scr[...] = av
        b_scr[...] = bv

        def group(i, h_in):
            sl = pl.ds(pl.multiple_of(i * V7X_SUBLANES, V7X_SUBLANES), V7X_SUBLANES)
            hg = a_scr[sl, :] * h_in + b_scr[sl, :]
            h_scr[sl, :] = hg
            return jnp.broadcast_to(hg[V7X_SUBLANES - 1:, :], (V7X_SUBLANES, w))

        h_last = lax.fori_loop(0, rows // V7X_SUBLANES, group, hc_scr[...])
        hc_scr[...] = h_last
        ho_ref[0] = h_last
        hs = h_scr[...]
    a_ref[...] = (hs * gate).astype(BF16).reshape(bb, tt, w)


def _rec(x, gn, w_in, cw, cb, w_gate, b_a, b_i, lam, state, h0, *, bb, tt):
    b, t, d = x.shape
    w = cw.shape[1]
    rows = bb * tt
    decode = state is not None
    tok = lambda c: pl.BlockSpec((bb, tt, c), lambda i, j: (i, j, 0))
    consts = (gn, w_in, cw, cb, w_gate, b_a, b_i, lam)
    in_specs = [tok(d)] + [_const_spec(a) for a in consts]
    args = [x, *consts]
    scratch = [pltpu.VMEM((rows, w), F32), pltpu.VMEM((rows, w), F32)]
    grp = pl.BlockSpec((bb, V7X_SUBLANES, w), lambda i, j: (i, 0, 0))
    if decode:
        in_specs += [grp, grp]
        args += [state, h0]
    else:
        assert bb == 1
        scratch += [pltpu.VMEM((rows, w), F32), pltpu.VMEM((V7X_SUBLANES, w), F32), pltpu.VMEM((V7X_SUBLANES, w), F32),
                    pltpu.VMEM((V7X_SUBLANES + rows, w), F32)]
    return pl.pallas_call(
        functools.partial(_rec_kernel, decode=decode),
        grid=(b // bb, t // tt), in_specs=in_specs,
        out_specs=[tok(w), grp, grp],
        out_shape=[jax.ShapeDtypeStruct((b, t, w), BF16), jax.ShapeDtypeStruct((b, V7X_SUBLANES, w), F32),
                   jax.ShapeDtypeStruct((b, V7X_SUBLANES, w), F32)],
        scratch_shapes=scratch,
        compiler_params=_params("parallel", "arbitrary"), name="rglru_mixer",
    )(*args)


def _pad_state(buf):
    return jnp.pad(buf, ((0, 0), (V7X_SUBLANES - buf.shape[1], 0), (0, 0)))


class _Tiles(NamedTuple):
    prompt_rows: int
    decode_seqs: int
    attn_q_rows: int
    attn_heads: int
    ffn_chunk: int
    pages_per_group: int


def _tiles(tp, bs, ts, n_heads, vh, dff, n_pages):
    heads = 4 if n_heads % 4 == 0 and (4 * vh) % V7X_LANES == 0 else 2
    return _Tiles(prompt_rows=min(tp, 512), decode_seqs=min(bs, 256 // ts), attn_q_rows=min(tp, 256),
                  attn_heads=heads, ffn_chunk=min(dff, 1024), pages_per_group=min(n_pages, 16))


def kernel(x_prompt, x_sample, cache_ckv, cache_krope, page_table, state_lru_h, state_lru_conv, state_ffn_conv,
           norm_mix, norm_ffn, attn_w_down, attn_g_q_lora, attn_w_uq, attn_g_kv_lora, attn_g_qn, attn_g_qr,
           attn_g_kn, attn_g_kr, attn_w_uk, attn_w_uv, attn_w_o, rec_w_in, rec_conv_w, rec_conv_b, rec_w_a,
           rec_b_a, rec_w_i, rec_b_i, rec_lambda, rec_w_out, ffn_w_up, ffn_conv_w, ffn_conv_b, ffn_w_down):
    bp, tp, d = x_prompt.shape
    bs, ts, _ = x_sample.shape
    depth = norm_mix.shape[0]
    n_mixers = 2
    kv_lora, n_heads, nope = attn_w_uk.shape[1:]
    vh = attn_w_uv.shape[3]
    rope = attn_g_qr.shape[1]
    q_lora = attn_g_q_lora.shape[1]
    half = rope // 2
    page = cache_ckv.shape[2]
    past = page_table.shape[1] * page
    dff = ffn_conv_w.shape[2]
    lru_w = rec_conv_w.shape[2]
    assert ts == V7X_SUBLANES and rope + nope + half <= V7X_LANES and vh < V7X_LANES and n_heads % 2 == 0
    scale = float((nope + rope) ** -0.5) * math.log2(math.e)
    dims = (n_heads, q_lora, kv_lora, nope, rope, vh, scale)
    pad_l = V7X_LANES - rope - nope
    row = lambda v: v.reshape(1, -1).astype(F32)

    def rope_tables(pos):
        inv = ROPE_THETA ** (-jnp.arange(0, rope, 2, dtype=F32) / rope)
        ang = pos.astype(F32)[:, None] * inv[None, :]
        cos, sin = jnp.cos(ang), jnp.sin(ang)
        n = pos.shape[0]
        ones = jnp.ones((n, V7X_LANES - rope - half), F32)
        return (jnp.concatenate([cos, cos, ones, jnp.zeros((n, half), F32)], axis=1),
                jnp.concatenate([-sin, sin, jnp.zeros((n, V7X_LANES - rope), F32)], axis=1))

    tabs_p = rope_tables(jnp.arange(tp))
    tabs_s = rope_tables(past + jnp.arange(ts))

    xp, xs = x_prompt, x_sample
    a_p = a_s = w_pre = None
    ckv_p, kr_p, ckv_s, kr_s = [], [], [], []
    lh_p, lh_s, lc_p, lc_s = [], [], [], []
    fc_p, fc_s = [], []
    tiles = _tiles(tp, bs, ts, n_heads, vh, dff, page_table.shape[1])
    for i in range(depth):
        j = i // n_mixers
        gmix = row(norm_mix[i])
        if i % n_mixers == 0:
            wdn = attn_w_down[j]
            wd = jnp.concatenate([wdn, jnp.zeros((d, V7X_LANES - rope - half), F32), wdn[:, -half:]],
                                 axis=1).astype(BF16)
            gkr = jnp.concatenate([attn_g_kr[j], jnp.zeros((V7X_LANES - rope - half,), F32),
                                   attn_g_kr[j][half:]]).reshape(1, -1)
            wq = attn_w_uq[j].reshape(q_lora, n_heads, nope + rope)
            wuq = jnp.concatenate([wq[:, :, nope:], wq[:, :, :nope], jnp.zeros((q_lora, n_heads, pad_l - half), F32),
                                   wq[:, :, nope + half:]],
                                  axis=2).reshape(q_lora, n_heads * V7X_LANES).astype(BF16)
            gq = jnp.concatenate([attn_g_qr[j], attn_g_qn[j], jnp.zeros((pad_l - half,), F32),
                                  attn_g_qr[j][half:]]).reshape(1, -1)
            wk = attn_w_uk[j]
            wuk = jnp.concatenate([jnp.zeros((kv_lora, n_heads, rope), F32), wk,
                                   jnp.zeros((kv_lora, n_heads, pad_l), F32)],
                                  axis=2).reshape(kv_lora, n_heads * V7X_LANES).astype(BF16)
            gk = jnp.concatenate([jnp.zeros((rope,), F32), attn_g_kn[j], jnp.zeros((pad_l,), F32)]).reshape(1, -1)
            wuv = jnp.concatenate([attn_w_uv[j], jnp.zeros((kv_lora, n_heads, V7X_LANES - vh), F32)],
                                  axis=2).reshape(kv_lora, n_heads * V7X_LANES).astype(BF16)
            vone = jnp.tile((jnp.arange(V7X_LANES) == vh).astype(F32), n_heads).reshape(1, -1)
            proj_w = (gmix, wd, row(attn_g_q_lora[j]), row(attn_g_kv_lora[j]), gkr, wuq, gq, wuk, gk, wuv, vone)

            q_p, c_p, krb_p, k_p, v_p = _mla_proj(xp, *proj_w, *tabs_p, bb=1, tt=tiles.prompt_rows, dims=dims,
                                                  with_kv=True)
            wk_perm = jnp.transpose(wk, (0, 2, 1)).reshape(kv_lora, nope * n_heads).astype(BF16)
            proj_s = proj_w[:7] + (wk_perm,) + proj_w[8:]
            q_s, c_s, krb_s, rs_s = _mla_proj(xs, *proj_s, *tabs_s, bb=tiles.decode_seqs, tt=ts, dims=dims,
                                              with_kv=False)
            amax = lambda g: jnp.max(jnp.abs(g))
            score_bound = 1.02 * scale * (nope * amax(attn_g_qn[j]) * amax(attn_g_kn[j])
                                          + rope * amax(attn_g_qr[j]) * amax(attn_g_kr[j]))
            bounded = (score_bound <= SCORE_EXP2_LIMIT).astype(jnp.int32).reshape(1)
            a_p = _flash(bounded, q_p, k_p, v_p, tq=tiles.attn_q_rows, n_heads=n_heads, vh=vh,
                         hps=tiles.attn_heads)

            wa = jnp.transpose(wk * attn_g_kn[j][None, None, :], (1, 2, 0))
            wa = jnp.concatenate([jnp.zeros((n_heads, rope, kv_lora), F32), wa,
                                  jnp.zeros((n_heads, pad_l, kv_lora), F32)], axis=1).astype(BF16)
            qa = _absorb(q_s.reshape(bs * ts, n_heads * V7X_LANES), wa, tr=tiles.decode_seqs * ts)
            qa = qa.reshape(bs, ts * n_heads, kv_lora)
            qr = q_s.reshape(bs, ts * n_heads, V7X_LANES)
            ctx = _decode_attn(page_table, cache_ckv, cache_krope, j, wk_perm, qa, qr, c_s, krb_s, rs_s,
                               pg=tiles.pages_per_group, dims=dims)
            wv = attn_w_uv[j]
            zer = jnp.zeros((kv_lora, n_heads // 2, vh), F32)
            wv2 = jnp.concatenate([jnp.concatenate([wv[:, 0::2], zer], axis=2),
                                   jnp.concatenate([zer, wv[:, 1::2]], axis=2)], axis=0)
            wv2 = jnp.transpose(wv2, (1, 0, 2)).astype(BF16)
            a_s = _unabsorb(ctx.reshape(bs * ts, n_heads * kv_lora), wv2, tr=tiles.decode_seqs * ts)
            a_s = a_s.reshape(bs, ts, n_heads * vh)
            w_pre = attn_w_o[j].astype(BF16)
            ckv_p.append(c_p)
            kr_p.append(krb_p[:, :, :rope])
            ckv_s.append(c_s)
            kr_s.append(krb_s[:, :, :rope])
        else:
            n_blk, blk = rec_w_a.shape[1:3]
            assert n_blk % 2 == 0

            def pair(wm):
                z = jnp.zeros((n_blk // 2, blk, blk), F32)
                return jnp.concatenate([jnp.concatenate([wm[0::2], z], axis=2),
                                        jnp.concatenate([z, wm[1::2]], axis=2)], axis=1)

            w_gate = jnp.concatenate([pair(rec_w_a[j]), pair(rec_w_i[j])], axis=2).astype(BF16)
            rec_w = (gmix, rec_w_in[j].astype(BF16), rec_conv_w[j], row(rec_conv_b[j]), w_gate, row(rec_b_a[j]),
                     row(rec_b_i[j]), row(rec_lambda[j]))
            a_p, cv_p, h_p = _rec(xp, *rec_w, None, None, bb=1, tt=tiles.prompt_rows)
            h0 = jnp.broadcast_to(state_lru_h[j][:, None, :], (bs, ts, lru_w))
            a_s, cv_s, h_s = _rec(xs, *rec_w, _pad_state(state_lru_conv[j]), h0, bb=tiles.decode_seqs, tt=ts)
            w_pre = rec_w_out[j].astype(BF16)
            kw = rec_conv_w.shape[1] - 1
            lh_p.append(h_p[:, V7X_SUBLANES - 1])
            lh_s.append(h_s[:, V7X_SUBLANES - 1])
            lc_p.append(cv_p[:, V7X_SUBLANES - kw:])
            lc_s.append(cv_s[:, V7X_SUBLANES - kw:])
        ffn_w = (w_pre, row(norm_ffn[i]), ffn_w_up[i].astype(BF16), ffn_conv_w[i], row(ffn_conv_b[i]),
                 ffn_w_down[i].astype(BF16))
        xp, fb_p = _ffn(xp, a_p, *ffn_w, None, bb=1, tt=tiles.prompt_rows, fc=tiles.ffn_chunk)
        xs, fb_s = _ffn(xs, a_s, *ffn_w, _pad_state(state_ffn_conv[i]), bb=tiles.decode_seqs, tt=ts,
                        fc=tiles.ffn_chunk)
        kf = ffn_conv_w.shape[1] - 1
        fc_p.append(fb_p[:, V7X_SUBLANES - kf:])
        fc_s.append(fb_s[:, V7X_SUBLANES - kf:])
    return (xp, xs, jnp.stack(ckv_p), jnp.stack(kr_p), jnp.stack(ckv_s), jnp.stack(kr_s),
            jnp.stack(lh_p), jnp.stack(lh_s), jnp.stack(lc_p), jnp.stack(lc_s), jnp.stack(fc_p), jnp.stack(fc_s))
```

```python
import functools
import math
from typing import NamedTuple

import jax
import jax.numpy as jnp
from jax import lax
from jax.experimental import pallas as pl
from jax.experimental.pallas import tpu as pltpu

EPS = 1e-6
ROPE_THETA = 10000.0
LRU_C = 8.0
SCORE_EXP2_LIMIT = 64.0

V7X_LANES = 128
V7X_SUBLANES = 8
V7X_VMEM_BYTES = 64 * 1024 * 1024
VMEM_LIMIT = V7X_VMEM_BYTES * 7 // 8

F32 = jnp.float32
BF16 = jnp.bfloat16


def _params(*sem):
    return pltpu.CompilerParams(dimension_semantics=sem, vmem_limit_bytes=VMEM_LIMIT)


def _const_spec(a):
    nd = a.ndim
    return pl.BlockSpec(a.shape, lambda *_: (0,) * nd)


def _rms(x, g):
    ms = jnp.mean(x * x, axis=-1, keepdims=True)
    return x * lax.rsqrt(ms + EPS) * g


def _sigmoid(x):
    return 1.0 / (1.0 + jnp.exp(-x))


def _rope(y, cos, sin, half):
    return y * cos + pltpu.roll(y, half, axis=1) * sin


def _tile_rows(tab_ref, bb):
    t = tab_ref[...]
    if bb == 1:
        return t
    return jnp.broadcast_to(t[None], (bb,) + t.shape).reshape(bb * t.shape[0], t.shape[1])


def _key_norm_factor(cb, wk_ref, n_heads, nope):
    k = jnp.dot(cb, wk_ref[...], preferred_element_type=F32)
    ksq = k * k
    part = ksq[:, :V7X_LANES]
    for i in range(1, wk_ref.shape[1] // V7X_LANES):
        part = part + ksq[:, V7X_LANES * i:V7X_LANES * (i + 1)]
    shift = V7X_LANES // 2
    while shift >= n_heads:
        part = part + pltpu.roll(part, shift, axis=1)
        shift //= 2
    return lax.rsqrt(part / nope + EPS)


def _mla_proj_kernel(x_ref, gmix_ref, wd_ref, gql_ref, gkv_ref, gkr_ref, wuq_ref, gq_ref, wuk_ref, gk_ref, wuv_ref,
                     vone_ref, cos_ref, sin_ref, *out_refs, n_heads, q_lora, kv_lora, nope, rope, scale, with_kv):
    if with_kv:
        q_ref, ckv_ref, kr_ref, k_ref, v_ref = out_refs
    else:
        q_ref, ckv_ref, kr_ref, rs_ref = out_refs
    bb, tt, d = x_ref.shape
    rows = bb * tt
    half = rope // 2
    xn = _rms(x_ref[...].reshape(rows, d), gmix_ref[...])
    dd = jnp.dot(xn.astype(BF16), wd_ref[...], preferred_element_type=F32)
    cq = _rms(dd[:, :q_lora], gql_ref[...])
    ckv = _rms(dd[:, q_lora:q_lora + kv_lora], gkv_ref[...])
    kr = dd[:, q_lora + kv_lora:]
    cos = _tile_rows(cos_ref, bb)
    sin = _tile_rows(sin_ref, bb)
    lane = lax.broadcasted_iota(jnp.int32, (1, V7X_LANES), 1)
    is_rope = lane < rope
    is_nope = jnp.logical_and(lane >= rope, lane < rope + nope)

    kr_ss = jnp.sum(jnp.where(is_rope, kr * kr, 0.0), axis=-1, keepdims=True)
    krr = _rope(kr * lax.rsqrt(kr_ss / rope + EPS) * gkr_ref[...], cos, sin, half)
    kr_ref[...] = krr.reshape(bb, tt, V7X_LANES)
    ckv_ref[...] = ckv.reshape(bb, tt, kv_lora)

    q = jnp.dot(cq.astype(BF16), wuq_ref[...], preferred_element_type=F32)
    ckv_b = ckv.astype(BF16)
    if with_kv:
        kk = jnp.dot(ckv_b, wuk_ref[...], preferred_element_type=F32)
        vv = jnp.dot(ckv_b, wuv_ref[...], preferred_element_type=F32) + vone_ref[...]
        v_ref[...] = vv.astype(BF16).reshape(v_ref.shape)
    else:
        rs_ref[...] = _key_norm_factor(ckv_b, wuk_ref, n_heads, nope).reshape(bb, tt, V7X_LANES)
    for h in range(n_heads):
        sl = slice(V7X_LANES * h, V7X_LANES * (h + 1))
        qb = q[:, sl]
        sq = qb * qb
        ss_r = jnp.sum(jnp.where(is_rope, sq, 0.0), axis=-1, keepdims=True)
        ss_n = jnp.sum(jnp.where(is_nope, sq, 0.0), axis=-1, keepdims=True)
        rs = jnp.where(is_nope, lax.rsqrt(ss_n / nope + EPS), lax.rsqrt(ss_r / rope + EPS))
        y = _rope(qb * rs * gq_ref[...], cos, sin, half)
        q_ref[:, :, sl] = (y * scale).astype(BF16).reshape(bb, tt, V7X_LANES)
        if with_kv:
            kb = kk[:, sl]
            ss_k = jnp.sum(kb * kb, axis=-1, keepdims=True)
            kn = kb * lax.rsqrt(ss_k / nope + EPS) * gk_ref[...]
            k_ref[:, :, sl] = (kn + krr).astype(BF16).reshape(bb, tt, V7X_LANES)


def _mla_proj(x, gmix, wd, gql, gkv, gkr, wuq, gq, wuk, gk, wuv, vone, cos, sin, *, bb, tt, dims, with_kv):
    n_heads, q_lora, kv_lora, nope, rope, vh, scale = dims
    b, t, d = x.shape
    hl = n_heads * V7X_LANES
    tok = lambda c: pl.BlockSpec((bb, tt, c), lambda i, j: (i, j, 0))
    tab = pl.BlockSpec((tt, V7X_LANES), lambda i, j: (j, 0))
    out_shape = [jax.ShapeDtypeStruct((b, t, hl), BF16), jax.ShapeDtypeStruct((b, t, kv_lora), F32),
                 jax.ShapeDtypeStruct((b, t, V7X_LANES), F32)]
    out_specs = [tok(hl), tok(kv_lora), tok(V7X_LANES)]
    if with_kv:
        out_shape += [jax.ShapeDtypeStruct((b, t, hl), BF16), jax.ShapeDtypeStruct((b, t, hl), BF16)]
        out_specs += [tok(hl), tok(hl)]
    else:
        out_shape.append(jax.ShapeDtypeStruct((b, t, V7X_LANES), F32))
        out_specs.append(tok(V7X_LANES))
    consts = (gmix, wd, gql, gkv, gkr, wuq, gq, wuk, gk, wuv, vone)
    return pl.pallas_call(
        functools.partial(_mla_proj_kernel, n_heads=n_heads, q_lora=q_lora, kv_lora=kv_lora, nope=nope, rope=rope,
                          scale=scale, with_kv=with_kv),
        grid=(b // bb, t // tt),
        in_specs=[tok(d)] + [_const_spec(a) for a in consts] + [tab, tab],
        out_specs=out_specs, out_shape=out_shape,
        compiler_params=_params("parallel", "parallel"), name="mla_proj",
    )(x, *consts, cos, sin)


def _attn_block(q, k_ref, v_ref, lanes, vh, past, tq, bounded):
    dn = (((1,), (1,)), ((), ()))
    row = lax.broadcasted_iota(jnp.int32, (tq, tq), 0)
    col = lax.broadcasted_iota(jnp.int32, (tq, tq), 1)
    s_d = lax.dot_general(q, k_ref[0, past:past + tq, lanes], dn, preferred_element_type=F32)
    s_d = jnp.where(col <= row, s_d, -jnp.inf)
    if past:
        s_p = lax.dot_general(q, k_ref[0, :past, lanes], dn, preferred_element_type=F32)
    if not bounded:
        m = jnp.max(s_d, axis=1, keepdims=True)
        if past:
            m = jnp.maximum(m, jnp.max(s_p, axis=1, keepdims=True))
            s_p = s_p - m
        s_d = s_d - m
    acc = jnp.dot(jnp.exp2(s_d).astype(BF16), v_ref[0, past:past + tq, lanes], preferred_element_type=F32)
    if past:
        acc = acc + jnp.dot(jnp.exp2(s_p).astype(BF16), v_ref[0, :past, lanes], preferred_element_type=F32)
    return acc[:, :vh] / acc[:, vh:vh + 1]


def _flash_kernel(bounded_ref, q_ref, k_ref, v_ref, o_ref, *, tq, vh, n_q, hps):
    qi = pl.program_id(2)
    is_bounded = bounded_ref[0] != 0
    for n in range(n_q):
        for bounded in (True, False):
            @pl.when(jnp.logical_and(qi == n, is_bounded == bounded))
            def _(n=n, bounded=bounded):
                outs = []
                for hh in range(hps):
                    lanes = slice(V7X_LANES * hh, V7X_LANES * (hh + 1))
                    outs.append(_attn_block(q_ref[0, :, lanes], k_ref, v_ref, lanes, vh, n * tq, tq, bounded))
                o_ref[0] = jnp.concatenate(outs, axis=1).astype(BF16)


def _flash(bounded, q, k, v, *, tq, n_heads, vh, hps):
    b, t, _ = q.shape
    grid_spec = pltpu.PrefetchScalarGridSpec(
        num_scalar_prefetch=1, grid=(b, n_heads // hps, t // tq),
        in_specs=[pl.BlockSpec((1, tq, hps * V7X_LANES), lambda i, h, j, f: (i, j, h)),
                  pl.BlockSpec((1, t, hps * V7X_LANES), lambda i, h, j, f: (i, 0, h)),
                  pl.BlockSpec((1, t, hps * V7X_LANES), lambda i, h, j, f: (i, 0, h))],
        out_specs=pl.BlockSpec((1, tq, hps * vh), lambda i, h, j, f: (i, j, h)))
    return pl.pallas_call(
        functools.partial(_flash_kernel, tq=tq, vh=vh, n_q=t // tq, hps=hps),
        grid_spec=grid_spec, out_shape=jax.ShapeDtypeStruct((b, t, n_heads * vh), BF16),
        compiler_params=_params("parallel", "parallel", "arbitrary"), name="mla_prompt_attn",
    )(bounded, q, k, v)


def _absorb_kernel(q_ref, wa_ref, qa_ref, *, n_heads, kv_lora):
    for h in range(n_heads):
        qh = q_ref[:, V7X_LANES * h:V7X_LANES * (h + 1)]
        qa_ref[:, kv_lora * h:kv_lora * (h + 1)] = jnp.dot(qh, wa_ref[h], preferred_element_type=F32).astype(BF16)


def _absorb(q2d, wa, *, tr):
    r = q2d.shape[0]
    n_heads, _, kv_lora = wa.shape
    return pl.pallas_call(
        functools.partial(_absorb_kernel, n_heads=n_heads, kv_lora=kv_lora),
        grid=(r // tr,),
        in_specs=[pl.BlockSpec((tr, n_heads * V7X_LANES), lambda i: (i, 0)), _const_spec(wa)],
        out_specs=pl.BlockSpec((tr, n_heads * kv_lora), lambda i: (i, 0)),
        out_shape=jax.ShapeDtypeStruct((r, n_heads * kv_lora), BF16),
        compiler_params=_params("parallel"), name="mla_absorb_q",
    )(q2d, wa)


def _decode_attn_kernel(pt_ref, ckv_hbm, krt_hbm, wk_ref, perm_ref, qa_ref, qr_ref, cnew_ref, krnew_ref, rsnew_ref,
                        o_ref, cbuf, krbuf, cb_scr, s_scr, m_scr, l_scr, acc_scr, csem, ksem,
                        *, layer, pg, n_groups, n_heads, nope, rope):
    b = pl.program_id(0)
    page = ckv_hbm.shape[2]
    kv_lora = ckv_hbm.shape[3]
    dn_last = (((1,), (1,)), ((), ()))
    dn_first = (((0,), (0,)), ((), ()))
    qa = qa_ref[...]
    qr = qr_ref[:, :rope]

    def copies(seq, grp, slot):
        out = []
        for s in range(pg):
            pid = pt_ref[seq, grp * pg + s]
            rows = pl.ds(s * page, page)
            out.append(pltpu.make_async_copy(ckv_hbm.at[layer, pid], cbuf.at[slot, rows, :], csem.at[slot]))
            out.append(pltpu.make_async_copy(krt_hbm.at[layer, pid], krbuf.at[slot, :, rows], ksem.at[slot]))
        return out

    def fetch(seq, grp, slot):
        for cp in copies(seq, grp, slot):
            cp.start()

    def wait(seq, grp, slot):
        for cp in copies(seq, grp, slot):
            cp.wait()

    @pl.when(b == 0)
    def _():
        fetch(0, 0, 0)

    def start_from_new_tokens():
        n_new = cnew_ref.shape[0]
        pad = 2 * V7X_SUBLANES - n_new
        c_new = jnp.concatenate([cnew_ref[...], jnp.zeros((pad, kv_lora), F32)], axis=0).astype(BF16)
        kr_new = jnp.concatenate([krnew_ref[:, :rope], jnp.zeros((pad, rope), F32)], axis=0).astype(BF16)
        rs_new = jnp.concatenate([rsnew_ref[...], jnp.ones((pad, V7X_LANES), F32)], axis=0)
        s = lax.dot_general(c_new, qa, dn_last, preferred_element_type=F32) * rs_new
        s = s + lax.dot_general(kr_new, qr, dn_last, preferred_element_type=F32)
        key = lax.broadcasted_iota(jnp.int32, s.shape, 0)
        qry = lax.broadcasted_iota(jnp.int32, s.shape, 1) // n_heads
        s = jnp.where(key <= qry, s, -jnp.inf)
        m0 = jnp.max(s, axis=0, keepdims=True)
        p = jnp.exp2(s - m0)
        m_scr[...] = m0
        l_scr[...] = jnp.sum(p, axis=0, keepdims=True)
        acc_scr[...] = lax.dot_general(c_new, p.astype(BF16), dn_first, preferred_element_type=F32)

    def stage_a(slot):
        cb = cbuf[slot].astype(BF16)
        cb_scr[slot] = cb
        rs = _key_norm_factor(cb, wk_ref, n_heads, nope)
        s = lax.dot_general(cb, qa, dn_last, preferred_element_type=F32) * rs
        krb = krbuf[slot].astype(BF16)
        s_scr[slot] = s + lax.dot_general(krb, qr, (((0,), (1,)), ((), ())), preferred_element_type=F32)

    def stage_b(slot):
        s = s_scr[slot]
        m_old = m_scr[...]
        m_new = jnp.maximum(m_old, jnp.max(s, axis=0, keepdims=True))
        p = jnp.exp2(s - m_new)
        alpha = jnp.exp2(m_old - m_new)
        l_scr[...] = alpha * l_scr[...] + jnp.sum(p, axis=0, keepdims=True)
        pv = lax.dot_general(cb_scr[slot], p.astype(BF16), dn_first, preferred_element_type=F32)
        acc_scr[...] = alpha * acc_scr[...] + pv
        m_scr[...] = m_new

    for i in range(n_groups + 1):
        slot = i % 2
        if i + 1 < n_groups:
            fetch(b, i + 1, 1 - slot)
        elif i + 1 == n_groups:
            @pl.when(b + 1 < pl.num_programs(0))
            def _():
                fetch(b + 1, 0, 0)
        if i < n_groups:
            wait(b, i, slot)
            if i == 0:
                start_from_new_tokens()
            stage_a(slot)
        if i >= 1:
            stage_b(1 - slot)
    ctx = acc_scr[...] / l_scr[...]
    o_ref[...] = jnp.dot(ctx, perm_ref[...], preferred_element_type=F32).T


def _decode_attn(page_table, cache_ckv, cache_krope, layer, wk_perm, qa, qr, c_new, kr_new, rs_new, *, pg, dims):
    n_heads, _, kv_lora, nope, rope, _, _ = dims
    b, n_pages = page_table.shape
    page = cache_ckv.shape[2]
    n_new = c_new.shape[1]
    rows = n_new * n_heads
    n_groups = n_pages // pg
    keys = pg * page
    assert rows == V7X_LANES and n_pages % pg == 0 and n_groups % 2 == 0 and n_new <= 2 * V7X_SUBLANES
    krope_t = jnp.swapaxes(cache_krope, 2, 3)
    lane = jnp.arange(rows)
    perm = (lane[:, None] == (lane[None, :] % n_new) * n_heads + lane[None, :] // n_new).astype(F32)
    per_seq = lambda r, c: pl.BlockSpec((None, r, c), lambda i, pt: (i, 0, 0))
    hbm = pl.BlockSpec(memory_space=pl.ANY)
    grid_spec = pltpu.PrefetchScalarGridSpec(
        num_scalar_prefetch=1, grid=(b,),
        in_specs=[hbm, hbm, pl.BlockSpec(wk_perm.shape, lambda i, pt: (0, 0)),
                  pl.BlockSpec(perm.shape, lambda i, pt: (0, 0)), per_seq(rows, kv_lora), per_seq(rows, V7X_LANES),
                  per_seq(n_new, kv_lora), per_seq(n_new, V7X_LANES), per_seq(n_new, V7X_LANES)],
        out_specs=per_seq(rows, kv_lora),
        scratch_shapes=[pltpu.VMEM((2, keys, kv_lora), F32), pltpu.VMEM((2, rope, keys), F32),
                        pltpu.VMEM((2, keys, kv_lora), BF16), pltpu.VMEM((2, keys, V7X_LANES), F32),
                        pltpu.VMEM((1, V7X_LANES), F32), pltpu.VMEM((1, V7X_LANES), F32),
                        pltpu.VMEM((kv_lora, V7X_LANES), F32),
                        pltpu.SemaphoreType.DMA((2,)), pltpu.SemaphoreType.DMA((2,))])
    return pl.pallas_call(
        functools.partial(_decode_attn_kernel, layer=layer, pg=pg, n_groups=n_groups, n_heads=n_heads, nope=nope,
                          rope=rope),
        grid_spec=grid_spec, out_shape=jax.ShapeDtypeStruct((b, rows, kv_lora), F32),
        compiler_params=_params("arbitrary"), name="mla_decode_attn",
    )(page_table, cache_ckv, krope_t, wk_perm, perm, qa, qr, c_new, kr_new, rs_new)


def _unabsorb_kernel(ctx_ref, wv_ref, o_ref, *, n_pairs, n_q):
    bb, _, kv_lora = ctx_ref.shape
    head = lambda h: ctx_ref[:, n_q * h:n_q * (h + 1), :].reshape(bb * n_q, kv_lora)
    for p in range(n_pairs):
        x = jnp.concatenate([head(2 * p), head(2 * p + 1)], axis=1).astype(BF16)
        o_ref[:, V7X_LANES * p:V7X_LANES * (p + 1)] = jnp.dot(x, wv_ref[p], preferred_element_type=F32).astype(BF16)


def _unabsorb(ctx, wv2, *, bb, n_q):
    b, rows, kv_lora = ctx.shape
    n_pairs = wv2.shape[0]
    return pl.pallas_call(
        functools.partial(_unabsorb_kernel, n_pairs=n_pairs, n_q=n_q),
        grid=(b // bb,),
        in_specs=[pl.BlockSpec((bb, rows, kv_lora), lambda i: (i, 0, 0)), _const_spec(wv2)],
        out_specs=pl.BlockSpec((bb * n_q, n_pairs * V7X_LANES), lambda i: (i, 0)),
        out_shape=jax.ShapeDtypeStruct((b * n_q, n_pairs * V7X_LANES), BF16),
        compiler_params=_params("parallel"), name="mla_unabsorb_v",
    )(ctx, wv2)


def _conv_prompt(g, w, b, buf_ref, carry):
    rows = g.shape[0]
    width = w.shape[0]
    buf_ref[0:V7X_SUBLANES, :] = carry
    buf_ref[V7X_SUBLANES:, :] = g
    y = b + g * w[width - 1:width, :]
    for s in range(1, width):
        y = y + buf_ref[V7X_SUBLANES - s:V7X_SUBLANES - s + rows, :] * w[width - 1 - s:width - s, :]
    return y


def _conv_decode(g, w, b, prev, t_idx):
    width = w.shape[0]
    rows = g.shape[0]
    y = b + g * w[width - 1:width, :]
    for s in range(1, width):
        sh = jnp.where(t_idx >= s, pltpu.roll(g, s, axis=0), pltpu.roll(prev, rows - V7X_SUBLANES + s, axis=0))
        y = y + sh * w[width - 1 - s:width - s, :]
    return y


def _ffn_kernel(*refs, decode, fc):
    if decode:
        x_ref, a_ref, wpre_ref, gn_ref, wup_ref, cw_ref, cb_ref, wd_ref, st_ref, y_ref, so_ref = refs
    else:
        (x_ref, a_ref, wpre_ref, gn_ref, wup_ref, cw_ref, cb_ref, wd_ref,
         y_ref, so_ref, carry_scr, buf_scr) = refs
    bb, tt, d = x_ref.shape
    rows = bb * tt
    dff = wd_ref.shape[0]
    a = a_ref[...].reshape(rows, a_ref.shape[2])
    x1 = x_ref[...].reshape(rows, d) + jnp.dot(a, wpre_ref[...], preferred_element_type=F32)
    xn = _rms(x1, gn_ref[...]).astype(BF16)
    if decode:
        t_idx = lax.broadcasted_iota(jnp.int32, (rows, fc), 0) % tt
    else:
        @pl.when(pl.program_id(1) == 0)
        def _():
            carry_scr[...] = jnp.zeros(carry_scr.shape, F32)

    acc = x1
    for c in range(dff // fc):
        cols = slice(c * fc, (c + 1) * fc)
        g = jnp.dot(xn, wup_ref[:, cols], preferred_element_type=F32)
        u = jnp.dot(xn, wup_ref[:, dff + c * fc:dff + (c + 1) * fc], preferred_element_type=F32)
        if decode:
            gc = _conv_decode(g, cw_ref[:, cols], cb_ref[:, cols], st_ref[:, :, cols].reshape(rows, fc), t_idx)
            so_ref[:, :, cols] = g.reshape(bb, tt, fc)
        else:
            gc = _conv_prompt(g, cw_ref[:, cols], cb_ref[:, cols], buf_scr.at[c], carry_scr[:, cols])
            last = g[rows - V7X_SUBLANES:, :]
            carry_scr[:, cols] = last
            so_ref[0, :, cols] = last
        h = (gc * _sigmoid(gc) * u).astype(BF16)
        acc = acc + jnp.dot(h, wd_ref[cols, :], preferred_element_type=F32)
    y_ref[...] = acc.reshape(bb, tt, d)


def _resident_spec(a, layer=None):
    if layer is None:
        nd = a.ndim
        return pl.BlockSpec(a.shape, lambda *_: (0,) * nd, pipeline_mode=pl.Buffered(1))
    nd = a.ndim - 1
    return pl.BlockSpec((None,) + a.shape[1:], lambda *_: (layer,) + (0,) * nd, pipeline_mode=pl.Buffered(1))


def _ffn(x, a, w_pre, gn, wup, cw, cb, wd, state, *, layer, bb, tt, fc):
    b, t, d = x.shape
    da = a.shape[2]
    dff = wd.shape[1]
    rows = bb * tt
    decode = state is not None
    tok = lambda c: pl.BlockSpec((bb, tt, c), lambda i, j: (i, j, 0))
    weights = [w_pre, gn, wup, cw, cb, wd]
    in_specs = [tok(d), tok(da), _resident_spec(w_pre)] + [_resident_spec(w, layer) for w in weights[1:]]
    args = [x, a] + weights
    scratch = []
    if decode:
        in_specs.append(pl.BlockSpec((bb, tt, dff), lambda i, j: (i, 0, 0)))
        args.append(state)
    else:
        assert bb == 1
        scratch = [pltpu.VMEM((V7X_SUBLANES, dff), F32), pltpu.VMEM((dff // fc, V7X_SUBLANES + rows, fc), F32)]
    so_spec = pl.BlockSpec((bb, None, V7X_SUBLANES, dff), lambda i, j: (i, j, 0, 0))
    y, so = pl.pallas_call(
        functools.partial(_ffn_kernel, decode=decode, fc=fc),
        grid=(b // bb, t // tt), in_specs=in_specs,
        out_specs=[tok(d), so_spec],
        out_shape=[jax.ShapeDtypeStruct((b, t, d), F32),
                   jax.ShapeDtypeStruct((b, t // tt, V7X_SUBLANES, dff), F32)],
        scratch_shapes=scratch,
        compiler_params=_params("parallel", "arbitrary"), name="conv_ffn",
    )(*args)
    return y, so[:, -1]


def _rec_kernel(*refs, decode):
    if decode:
        (x_ref, gn_ref, win_ref, cw_ref, cb_ref, wgate_ref, ba_ref, bi_ref, lam_ref, st_ref, h0_ref,
         a_ref, co_ref, ho_ref, a_scr, b_scr) = refs
    else:
        (x_ref, gn_ref, win_ref, cw_ref, cb_ref, wgate_ref, ba_ref, bi_ref, lam_ref,
         a_ref, co_ref, ho_ref, a_scr, b_scr, h_scr, carry_scr, hc_scr, buf_scr) = refs
    bb, tt, d = x_ref.shape
    rows = bb * tt
    w = cw_ref.shape[1]
    t = pl.program_id(1)
    xn = _rms(x_ref[...].reshape(rows, d), gn_ref[...]).astype(BF16)
    z = jnp.dot(xn, win_ref[...], preferred_element_type=F32)
    xr = z[:, :w]
    gin = z[:, w:]
    gate = 0.5 * gin * (1.0 + jnp.tanh(0.7978845608028654 * (gin + 0.044715 * (gin * gin * gin))))
    t_idx = lax.broadcasted_iota(jnp.int32, (rows, w), 0) % V7X_SUBLANES
    if decode:
        xc = _conv_decode(xr, cw_ref[...], cb_ref[...], st_ref[...].reshape(rows, w), t_idx)
        co_ref[...] = xr.reshape(bb, tt, w)
    else:
        @pl.when(t == 0)
        def _():
            carry_scr[...] = jnp.zeros(carry_scr.shape, F32)
            hc_scr[...] = jnp.zeros(hc_scr.shape, F32)

        xc = _conv_prompt(xr, cw_ref[...], cb_ref[...], buf_scr, carry_scr[...])
        last = xr[rows - V7X_SUBLANES:, :]
        carry_scr[...] = last
        co_ref[0] = last

    n_blk = wgate_ref.shape[0]
    kb = wgate_ref.shape[1]
    xcb = xc.astype(BF16)
    ra, ri = [], []
    for p in range(n_blk):
        zz = jnp.dot(xcb[:, kb * p:kb * (p + 1)], wgate_ref[p], preferred_element_type=F32)
        ra.append(zz[:, :kb])
        ri.append(zz[:, kb:])
    r = _sigmoid(jnp.concatenate(ra, axis=1) + ba_ref[...])
    gi = _sigmoid(jnp.concatenate(ri, axis=1) + bi_ref[...])
    neg_lam = -lam_ref[...]
    softplus = jnp.maximum(neg_lam, 0.0) + jnp.log(1.0 + jnp.exp(-jnp.abs(neg_lam)))
    av = jnp.exp(r * (-LRU_C * softplus))
    bv = jnp.sqrt(1.0 - av * av) * (gi * xc)

    grp3 = (rows // V7X_SUBLANES, V7X_SUBLANES, w)
    av, bv = av.reshape(grp3), bv.reshape(grp3)
    t3 = lax.broadcasted_iota(jnp.int32, grp3, 1)
    for s in (1, 2, 4):
        keep = t3 >= s
        bv = jnp.where(keep, av * pltpu.roll(bv, s, axis=1) + bv, bv)
        av = jnp.where(keep, av * pltpu.roll(av, s, axis=1), av)
    av, bv = av.reshape(rows, w), bv.reshape(rows, w)
    if decode:
        hs = av * h0_ref[...].reshape(rows, w) + bv
        ho_ref[...] = hs.reshape(bb, tt, w)
    else:
        a_scr[...] = av
        b_scr[...] = bv

        def group(i, h_in):
            sl = pl.ds(pl.multiple_of(i * V7X_SUBLANES, V7X_SUBLANES), V7X_SUBLANES)
            hg = a_scr[sl, :] * h_in + b_scr[sl, :]
            h_scr[sl, :] = hg
            return jnp.broadcast_to(hg[V7X_SUBLANES - 1:, :], (V7X_SUBLANES, w))

        h_last = lax.fori_loop(0, rows // V7X_SUBLANES, group, hc_scr[...])
        hc_scr[...] = h_last
        ho_ref[0] = h_last
        hs = h_scr[...]
    a_ref[...] = (hs * gate).astype(BF16).reshape(bb, tt, w)


def _rec(x, gn, w_in, cw, cb, w_gate, b_a, b_i, lam, state, h0, *, bb, tt):
    b, t, d = x.shape
    w = cw.shape[1]
    rows = bb * tt
    decode = state is not None
    tok = lambda c: pl.BlockSpec((bb, tt, c), lambda i, j: (i, j, 0))
    consts = (gn, w_in, cw, cb, w_gate, b_a, b_i, lam)
    in_specs = [tok(d)] + [_const_spec(a) for a in consts]
    args = [x, *consts]
    scratch = [pltpu.VMEM((rows, w), F32), pltpu.VMEM((rows, w), F32)]
    grp = pl.BlockSpec((bb, V7X_SUBLANES, w), lambda i, j: (i, 0, 0))
    if decode:
        in_specs += [grp, grp]
        args += [state, h0]
    else:
        assert bb == 1
        scratch += [pltpu.VMEM((rows, w), F32), pltpu.VMEM((V7X_SUBLANES, w), F32), pltpu.VMEM((V7X_SUBLANES, w), F32),
                    pltpu.VMEM((V7X_SUBLANES + rows, w), F32)]
    return pl.pallas_call(
        functools.partial(_rec_kernel, decode=decode),
        grid=(b // bb, t // tt), in_specs=in_specs,
        out_specs=[tok(w), grp, grp],
        out_shape=[jax.ShapeDtypeStruct((b, t, w), BF16), jax.ShapeDtypeStruct((b, V7X_SUBLANES, w), F32),
                   jax.ShapeDtypeStruct((b, V7X_SUBLANES, w), F32)],
        scratch_shapes=scratch,
        compiler_params=_params("parallel", "arbitrary"), name="rglru_mixer",
    )(*args)


def _pad_state(buf):
    return jnp.pad(buf, ((0, 0), (V7X_SUBLANES - buf.shape[1], 0), (0, 0)))


class _Tiles(NamedTuple):
    prompt_rows: int
    decode_seqs: int
    attn_q_rows: int
    attn_heads: int
    ffn_chunk: int
    pages_per_group: int


def _tiles(tp, bs, ts, n_heads, vh, dff, n_pages):
    heads = 4 if n_heads % 4 == 0 and (4 * vh) % V7X_LANES == 0 else 2
    return _Tiles(prompt_rows=min(tp, 512), decode_seqs=min(bs, 256 // ts), attn_q_rows=min(tp, 256),
                  attn_heads=heads, ffn_chunk=min(dff, 1024), pages_per_group=min(n_pages, 16))


def kernel(x_prompt, x_sample, cache_ckv, cache_krope, page_table, state_lru_h, state_lru_conv, state_ffn_conv,
           norm_mix, norm_ffn, attn_w_down, attn_g_q_lora, attn_w_uq, attn_g_kv_lora, attn_g_qn, attn_g_qr,
           attn_g_kn, attn_g_kr, attn_w_uk, attn_w_uv, attn_w_o, rec_w_in, rec_conv_w, rec_conv_b, rec_w_a,
           rec_b_a, rec_w_i, rec_b_i, rec_lambda, rec_w_out, ffn_w_up, ffn_conv_w, ffn_conv_b, ffn_w_down):
    bp, tp, d = x_prompt.shape
    bs, ts, _ = x_sample.shape
    depth = norm_mix.shape[0]
    n_mixers = 2
    kv_lora, n_heads, nope = attn_w_uk.shape[1:]
    vh = attn_w_uv.shape[3]
    rope = attn_g_qr.shape[1]
    q_lora = attn_g_q_lora.shape[1]
    half = rope // 2
    page = cache_ckv.shape[2]
    past = page_table.shape[1] * page
    dff = ffn_conv_w.shape[2]
    lru_w = rec_conv_w.shape[2]
    assert ts == V7X_SUBLANES and rope + nope + half <= V7X_LANES and vh < V7X_LANES and n_heads % 2 == 0
    scale = float((nope + rope) ** -0.5) * math.log2(math.e)
    dims = (n_heads, q_lora, kv_lora, nope, rope, vh, scale)
    pad_l = V7X_LANES - rope - nope
    row = lambda v: v.reshape(1, -1).astype(F32)

    def rope_tables(pos):
        inv = ROPE_THETA ** (-jnp.arange(0, rope, 2, dtype=F32) / rope)
        ang = pos.astype(F32)[:, None] * inv[None, :]
        cos, sin = jnp.cos(ang), jnp.sin(ang)
        n = pos.shape[0]
        ones = jnp.ones((n, V7X_LANES - rope - half), F32)
        return (jnp.concatenate([cos, cos, ones, jnp.zeros((n, half), F32)], axis=1),
                jnp.concatenate([-sin, sin, jnp.zeros((n, V7X_LANES - rope), F32)], axis=1))

    tabs_p = rope_tables(jnp.arange(tp))
    tabs_s = rope_tables(past + jnp.arange(ts))

    xp, xs = x_prompt, x_sample
    a_p = a_s = w_pre = None
    ckv_p, kr_p, ckv_s, kr_s = [], [], [], []
    lh_p, lh_s, lc_p, lc_s = [], [], [], []
    fc_p, fc_s = [], []
    tiles = _tiles(tp, bs, ts, n_heads, vh, dff, page_table.shape[1])
    ffn_stacked = (norm_ffn[:, None, :], ffn_w_up.astype(BF16), ffn_conv_w, ffn_conv_b[:, None, :],
                   ffn_w_down.astype(BF16))
    for i in range(depth):
        j = i // n_mixers
        gmix = row(norm_mix[i])
        if i % n_mixers == 0:
            wdn = attn_w_down[j]
            wd = jnp.concatenate([wdn, jnp.zeros((d, V7X_LANES - rope - half), F32), wdn[:, -half:]],
                                 axis=1).astype(BF16)
            gkr = jnp.concatenate([attn_g_kr[j], jnp.zeros((V7X_LANES - rope - half,), F32),
                                   attn_g_kr[j][half:]]).reshape(1, -1)
            wq = attn_w_uq[j].reshape(q_lora, n_heads, nope + rope)
            wuq = jnp.concatenate([wq[:, :, nope:], wq[:, :, :nope], jnp.zeros((q_lora, n_heads, pad_l - half), F32),
                                   wq[:, :, nope + half:]],
                                  axis=2).reshape(q_lora, n_heads * V7X_LANES).astype(BF16)
            gq = jnp.concatenate([attn_g_qr[j], attn_g_qn[j], jnp.zeros((pad_l - half,), F32),
                                  attn_g_qr[j][half:]]).reshape(1, -1)
            wk = attn_w_uk[j]
            wuk = jnp.concatenate([jnp.zeros((kv_lora, n_heads, rope), F32), wk,
                                   jnp.zeros((kv_lora, n_heads, pad_l), F32)],
                                  axis=2).reshape(kv_lora, n_heads * V7X_LANES).astype(BF16)
            gk = jnp.concatenate([jnp.zeros((rope,), F32), attn_g_kn[j], jnp.zeros((pad_l,), F32)]).reshape(1, -1)
            wuv = jnp.concatenate([attn_w_uv[j], jnp.zeros((kv_lora, n_heads, V7X_LANES - vh), F32)],
                                  axis=2).reshape(kv_lora, n_heads * V7X_LANES).astype(BF16)
            vone = jnp.tile((jnp.arange(V7X_LANES) == vh).astype(F32), n_heads).reshape(1, -1)
            proj_w = (gmix, wd, row(attn_g_q_lora[j]), row(attn_g_kv_lora[j]), gkr, wuq, gq, wuk, gk, wuv, vone)

            q_p, c_p, krb_p, k_p, v_p = _mla_proj(xp, *proj_w, *tabs_p, bb=1, tt=tiles.prompt_rows, dims=dims,
                                                  with_kv=True)
            wk_perm = jnp.transpose(wk, (0, 2, 1)).reshape(kv_lora, nope * n_heads).astype(BF16)
            proj_s = proj_w[:7] + (wk_perm,) + proj_w[8:]
            q_s, c_s, krb_s, rs_s = _mla_proj(xs, *proj_s, *tabs_s, bb=tiles.decode_seqs, tt=ts, dims=dims,
                                              with_kv=False)
            amax = lambda g: jnp.max(jnp.abs(g))
            score_bound = 1.02 * scale * (nope * amax(attn_g_qn[j]) * amax(attn_g_kn[j])
                                          + rope * amax(attn_g_qr[j]) * amax(attn_g_kr[j]))
            bounded = (score_bound <= SCORE_EXP2_LIMIT).astype(jnp.int32).reshape(1)
            a_p = _flash(bounded, q_p, k_p, v_p, tq=tiles.attn_q_rows, n_heads=n_heads, vh=vh,
                         hps=tiles.attn_heads)

            wa = jnp.transpose(wk * attn_g_kn[j][None, None, :], (1, 2, 0))
            wa = jnp.concatenate([jnp.zeros((n_heads, rope, kv_lora), F32), wa,
                                  jnp.zeros((n_heads, pad_l, kv_lora), F32)], axis=1).astype(BF16)
            qa = _absorb(q_s.reshape(bs * ts, n_heads * V7X_LANES), wa, tr=tiles.decode_seqs * ts)
            qa = qa.reshape(bs, ts * n_heads, kv_lora)
            qr = q_s.reshape(bs, ts * n_heads, V7X_LANES)
            ctx = _decode_attn(page_table, cache_ckv, cache_krope, j, wk_perm, qa, qr, c_s, krb_s, rs_s,
                               pg=tiles.pages_per_group, dims=dims)
            wv = attn_w_uv[j]
            zer = jnp.zeros((kv_lora, n_heads // 2, vh), F32)
            wv2 = jnp.concatenate([jnp.concatenate([wv[:, 0::2], zer], axis=2),
                                   jnp.concatenate([zer, wv[:, 1::2]], axis=2)], axis=0)
            wv2 = jnp.transpose(wv2, (1, 0, 2)).astype(BF16)
            a_s = _unabsorb(ctx, wv2, bb=tiles.decode_seqs, n_q=ts)
            a_s = a_s.reshape(bs, ts, n_heads * vh)
            w_pre = attn_w_o[j].astype(BF16)
            ckv_p.append(c_p)
            kr_p.append(krb_p[:, :, :rope])
            ckv_s.append(c_s)
            kr_s.append(krb_s[:, :, :rope])
        else:
            n_blk, blk = rec_w_a.shape[1:3]
            assert n_blk % 2 == 0

            def pair(wm):
                z = jnp.zeros((n_blk // 2, blk, blk), F32)
                return jnp.concatenate([jnp.concatenate([wm[0::2], z], axis=2),
                                        jnp.concatenate([z, wm[1::2]], axis=2)], axis=1)

            w_gate = jnp.concatenate([pair(rec_w_a[j]), pair(rec_w_i[j])], axis=2).astype(BF16)
            rec_w = (gmix, rec_w_in[j].astype(BF16), rec_conv_w[j], row(rec_conv_b[j]), w_gate, row(rec_b_a[j]),
                     row(rec_b_i[j]), row(rec_lambda[j]))
            a_p, cv_p, h_p = _rec(xp, *rec_w, None, None, bb=1, tt=tiles.prompt_rows)
            h0 = jnp.broadcast_to(state_lru_h[j][:, None, :], (bs, ts, lru_w))
            a_s, cv_s, h_s = _rec(xs, *rec_w, _pad_state(state_lru_conv[j]), h0, bb=tiles.decode_seqs, tt=ts)
            w_pre = rec_w_out[j].astype(BF16)
            kw = rec_conv_w.shape[1] - 1
            lh_p.append(h_p[:, V7X_SUBLANES - 1])
            lh_s.append(h_s[:, V7X_SUBLANES - 1])
            lc_p.append(cv_p[:, V7X_SUBLANES - kw:])
            lc_s.append(cv_s[:, V7X_SUBLANES - kw:])
        ffn_w = (w_pre,) + ffn_stacked
        xp, fb_p = _ffn(xp, a_p, *ffn_w, None, layer=i, bb=1, tt=tiles.prompt_rows, fc=tiles.ffn_chunk)
        xs, fb_s = _ffn(xs, a_s, *ffn_w, _pad_state(state_ffn_conv[i]), layer=i, bb=tiles.decode_seqs, tt=ts,
                        fc=tiles.ffn_chunk)
        kf = ffn_conv_w.shape[1] - 1
        fc_p.append(fb_p[:, V7X_SUBLANES - kf:])
        fc_s.append(fb_s[:, V7X_SUBLANES - kf:])
    return (xp, xs, jnp.stack(ckv_p), jnp.stack(kr_p), jnp.stack(ckv_s), jnp.stack(kr_s),
            jnp.stack(lh_p), jnp.stack(lh_s), jnp.stack(lc_p), jnp.stack(lc_s), jnp.stack(fc_p), jnp.stack(fc_s))
```

```python
import functools
import math
from typing import NamedTuple

import jax
import jax.numpy as jnp
from jax import lax
from jax.experimental import pallas as pl
from jax.experimental.pallas import tpu as pltpu

EPS = 1e-6
ROPE_THETA = 10000.0
LRU_C = 8.0
SCORE_EXP2_LIMIT = 64.0

V7X_LANES = 128
V7X_SUBLANES = 8
V7X_VMEM_BYTES = 64 * 1024 * 1024
VMEM_LIMIT = V7X_VMEM_BYTES * 7 // 8

F32 = jnp.float32
BF16 = jnp.bfloat16


def _params(*sem):
    return pltpu.CompilerParams(dimension_semantics=sem, vmem_limit_bytes=VMEM_LIMIT)


def _const_spec(a):
    nd = a.ndim
    return pl.BlockSpec(a.shape, lambda *_: (0,) * nd)


def _rms(x, g):
    ms = jnp.mean(x * x, axis=-1, keepdims=True)
    return x * lax.rsqrt(ms + EPS) * g


def _sigmoid(x):
    return 1.0 / (1.0 + jnp.exp(-x))


def _rope(y, cos, sin, half):
    return y * cos + pltpu.roll(y, half, axis=1) * sin


def _tile_rows(tab_ref, bb):
    t = tab_ref[...]
    if bb == 1:
        return t
    return jnp.broadcast_to(t[None], (bb,) + t.shape).reshape(bb * t.shape[0], t.shape[1])


def _key_norm_factor(cb, wk_ref, n_heads, nope):
    k = jnp.dot(cb, wk_ref[...], preferred_element_type=F32)
    ksq = k * k
    part = ksq[:, :V7X_LANES]
    for i in range(1, wk_ref.shape[1] // V7X_LANES):
        part = part + ksq[:, V7X_LANES * i:V7X_LANES * (i + 1)]
    shift = V7X_LANES // 2
    while shift >= n_heads:
        part = part + pltpu.roll(part, shift, axis=1)
        shift //= 2
    return lax.rsqrt(part / nope + EPS)


def _mla_proj_kernel(x_ref, gmix_ref, wd_ref, gql_ref, gkv_ref, gkr_ref, wuq_ref, gq_ref, wuk_ref, gk_ref, wuv_ref,
                     vone_ref, cos_ref, sin_ref, *out_refs, n_heads, q_lora, kv_lora, nope, rope, scale, with_kv):
    if with_kv:
        q_ref, ckv_ref, kr_ref, k_ref, v_ref = out_refs
    else:
        q_ref, ckv_ref, kr_ref, rs_ref = out_refs
    bb, tt, d = x_ref.shape
    rows = bb * tt
    half = rope // 2
    xn = _rms(x_ref[...].reshape(rows, d), gmix_ref[...])
    dd = jnp.dot(xn.astype(BF16), wd_ref[...], preferred_element_type=F32)
    cq = _rms(dd[:, :q_lora], gql_ref[...])
    ckv = _rms(dd[:, q_lora:q_lora + kv_lora], gkv_ref[...])
    kr = dd[:, q_lora + kv_lora:]
    cos = _tile_rows(cos_ref, bb)
    sin = _tile_rows(sin_ref, bb)
    lane = lax.broadcasted_iota(jnp.int32, (1, V7X_LANES), 1)
    is_rope = lane < rope
    is_nope = jnp.logical_and(lane >= rope, lane < rope + nope)

    kr_ss = jnp.sum(jnp.where(is_rope, kr * kr, 0.0), axis=-1, keepdims=True)
    krr = _rope(kr * lax.rsqrt(kr_ss / rope + EPS) * gkr_ref[...], cos, sin, half)
    kr_ref[...] = krr.reshape(bb, tt, V7X_LANES)
    ckv_ref[...] = ckv.reshape(bb, tt, kv_lora)

    q = jnp.dot(cq.astype(BF16), wuq_ref[...], preferred_element_type=F32)
    ckv_b = ckv.astype(BF16)
    if with_kv:
        kk = jnp.dot(ckv_b, wuk_ref[...], preferred_element_type=F32)
        vv = jnp.dot(ckv_b, wuv_ref[...], preferred_element_type=F32) + vone_ref[...]
        v_ref[...] = vv.astype(BF16).reshape(v_ref.shape)
    else:
        rs_ref[...] = _key_norm_factor(ckv_b, wuk_ref, n_heads, nope).reshape(bb, tt, V7X_LANES)
    for h in range(n_heads):
        sl = slice(V7X_LANES * h, V7X_LANES * (h + 1))
        qb = q[:, sl]
        sq = qb * qb
        ss_r = jnp.sum(jnp.where(is_rope, sq, 0.0), axis=-1, keepdims=True)
        ss_n = jnp.sum(jnp.where(is_nope, sq, 0.0), axis=-1, keepdims=True)
        rs = jnp.where(is_nope, lax.rsqrt(ss_n / nope + EPS), lax.rsqrt(ss_r / rope + EPS))
        y = _rope(qb * rs * gq_ref[...], cos, sin, half)
        q_ref[:, :, sl] = (y * scale).astype(BF16).reshape(bb, tt, V7X_LANES)
        if with_kv:
            kb = kk[:, sl]
            ss_k = jnp.sum(kb * kb, axis=-1, keepdims=True)
            kn = kb * lax.rsqrt(ss_k / nope + EPS) * gk_ref[...]
            k_ref[:, :, sl] = (kn + krr).astype(BF16).reshape(bb, tt, V7X_LANES)


def _mla_proj(x, gmix, wd, gql, gkv, gkr, wuq, gq, wuk, gk, wuv, vone, cos, sin, *, bb, tt, dims, with_kv):
    n_heads, q_lora, kv_lora, nope, rope, vh, scale = dims
    b, t, d = x.shape
    hl = n_heads * V7X_LANES
    tok = lambda c: pl.BlockSpec((bb, tt, c), lambda i, j: (i, j, 0))
    tab = pl.BlockSpec((tt, V7X_LANES), lambda i, j: (j, 0))
    out_shape = [jax.ShapeDtypeStruct((b, t, hl), BF16), jax.ShapeDtypeStruct((b, t, kv_lora), F32),
                 jax.ShapeDtypeStruct((b, t, V7X_LANES), F32)]
    out_specs = [tok(hl), tok(kv_lora), tok(V7X_LANES)]
    if with_kv:
        out_shape += [jax.ShapeDtypeStruct((b, t, hl), BF16), jax.ShapeDtypeStruct((b, t, hl), BF16)]
        out_specs += [tok(hl), tok(hl)]
    else:
        out_shape.append(jax.ShapeDtypeStruct((b, t, V7X_LANES), F32))
        out_specs.append(tok(V7X_LANES))
    consts = (gmix, wd, gql, gkv, gkr, wuq, gq, wuk, gk, wuv, vone)
    return pl.pallas_call(
        functools.partial(_mla_proj_kernel, n_heads=n_heads, q_lora=q_lora, kv_lora=kv_lora, nope=nope, rope=rope,
                          scale=scale, with_kv=with_kv),
        grid=(b // bb, t // tt),
        in_specs=[tok(d)] + [_const_spec(a) for a in consts] + [tab, tab],
        out_specs=out_specs, out_shape=out_shape,
        compiler_params=_params("parallel", "parallel"), name="mla_proj",
    )(x, *consts, cos, sin)


def _attn_block(q, k_ref, v_ref, lanes, vh, past, tq, bounded):
    dn = (((1,), (1,)), ((), ()))
    row = lax.broadcasted_iota(jnp.int32, (tq, tq), 0)
    col = lax.broadcasted_iota(jnp.int32, (tq, tq), 1)
    s_d = lax.dot_general(q, k_ref[0, past:past + tq, lanes], dn, preferred_element_type=F32)
    s_d = jnp.where(col <= row, s_d, -jnp.inf)
    if past:
        s_p = lax.dot_general(q, k_ref[0, :past, lanes], dn, preferred_element_type=F32)
    if not bounded:
        m = jnp.max(s_d, axis=1, keepdims=True)
        if past:
            m = jnp.maximum(m, jnp.max(s_p, axis=1, keepdims=True))
            s_p = s_p - m
        s_d = s_d - m
    acc = jnp.dot(jnp.exp2(s_d).astype(BF16), v_ref[0, past:past + tq, lanes], preferred_element_type=F32)
    if past:
        acc = acc + jnp.dot(jnp.exp2(s_p).astype(BF16), v_ref[0, :past, lanes], preferred_element_type=F32)
    return acc[:, :vh] / acc[:, vh:vh + 1]


def _flash_kernel(bounded_ref, q_ref, k_ref, v_ref, o_ref, *, tq, vh, hps):
    is_bounded = bounded_ref[0] != 0
    for bounded in (True, False):
        @pl.when(is_bounded == bounded)
        def _(bounded=bounded):
            for n in range(q_ref.shape[1] // tq):
                rows = slice(n * tq, (n + 1) * tq)
                outs = []
                for hh in range(hps):
                    lanes = slice(V7X_LANES * hh, V7X_LANES * (hh + 1))
                    outs.append(_attn_block(q_ref[0, rows, lanes], k_ref, v_ref, lanes, vh, n * tq, tq, bounded))
                o_ref[0, rows, :] = jnp.concatenate(outs, axis=1).astype(BF16)


def _flash(bounded, q, k, v, *, tq, n_heads, vh, hps):
    b, t, _ = q.shape
    seq = lambda c: pl.BlockSpec((1, t, hps * c), lambda i, h, f: (i, 0, h))
    grid_spec = pltpu.PrefetchScalarGridSpec(
        num_scalar_prefetch=1, grid=(b, n_heads // hps),
        in_specs=[seq(V7X_LANES), seq(V7X_LANES), seq(V7X_LANES)], out_specs=seq(vh))
    return pl.pallas_call(
        functools.partial(_flash_kernel, tq=tq, vh=vh, hps=hps),
        grid_spec=grid_spec, out_shape=jax.ShapeDtypeStruct((b, t, n_heads * vh), BF16),
        compiler_params=_params("parallel", "parallel"), name="mla_prompt_attn",
    )(bounded, q, k, v)


def _absorb_kernel(q_ref, wa_ref, qa_ref, *, n_heads, kv_lora):
    for h in range(n_heads):
        qh = q_ref[:, V7X_LANES * h:V7X_LANES * (h + 1)]
        qa_ref[:, kv_lora * h:kv_lora * (h + 1)] = jnp.dot(qh, wa_ref[h], preferred_element_type=F32).astype(BF16)


def _absorb(q2d, wa, *, tr):
    r = q2d.shape[0]
    n_heads, _, kv_lora = wa.shape
    return pl.pallas_call(
        functools.partial(_absorb_kernel, n_heads=n_heads, kv_lora=kv_lora),
        grid=(r // tr,),
        in_specs=[pl.BlockSpec((tr, n_heads * V7X_LANES), lambda i: (i, 0)), _const_spec(wa)],
        out_specs=pl.BlockSpec((tr, n_heads * kv_lora), lambda i: (i, 0)),
        out_shape=jax.ShapeDtypeStruct((r, n_heads * kv_lora), BF16),
        compiler_params=_params("parallel"), name="mla_absorb_q",
    )(q2d, wa)


def _decode_attn_kernel(pt_ref, ckv_hbm, krt_hbm, wk_ref, perm_ref, qa_ref, qr_ref, cnew_ref, krnew_ref, rsnew_ref,
                        o_ref, cbuf, krbuf, cb_scr, s_scr, m_scr, l_scr, acc_scr, csem, ksem,
                        *, layer, pg, n_groups, n_heads, nope, rope):
    b = pl.program_id(0)
    page = ckv_hbm.shape[2]
    kv_lora = ckv_hbm.shape[3]
    dn_last = (((1,), (1,)), ((), ()))
    dn_first = (((0,), (0,)), ((), ()))
    qa = qa_ref[...]
    qr = qr_ref[:, :rope]

    def copies(seq, grp, slot):
        out = []
        for s in range(pg):
            pid = pt_ref[seq, grp * pg + s]
            rows = pl.ds(s * page, page)
            out.append(pltpu.make_async_copy(ckv_hbm.at[layer, pid], cbuf.at[slot, rows, :], csem.at[slot]))
            out.append(pltpu.make_async_copy(krt_hbm.at[layer, pid], krbuf.at[slot, :, rows], ksem.at[slot]))
        return out

    def fetch(seq, grp, slot):
        for cp in copies(seq, grp, slot):
            cp.start()

    def wait(seq, grp, slot):
        for cp in copies(seq, grp, slot):
            cp.wait()

    @pl.when(b == 0)
    def _():
        fetch(0, 0, 0)

    def start_from_new_tokens():
        n_new = cnew_ref.shape[0]
        pad = 2 * V7X_SUBLANES - n_new
        c_new = jnp.concatenate([cnew_ref[...], jnp.zeros((pad, kv_lora), F32)], axis=0).astype(BF16)
        kr_new = jnp.concatenate([krnew_ref[:, :rope], jnp.zeros((pad, rope), F32)], axis=0).astype(BF16)
        rs_new = jnp.concatenate([rsnew_ref[...], jnp.ones((pad, V7X_LANES), F32)], axis=0)
        s = lax.dot_general(c_new, qa, dn_last, preferred_element_type=F32) * rs_new
        s = s + lax.dot_general(kr_new, qr, dn_last, preferred_element_type=F32)
        key = lax.broadcasted_iota(jnp.int32, s.shape, 0)
        qry = lax.broadcasted_iota(jnp.int32, s.shape, 1) // n_heads
        s = jnp.where(key <= qry, s, -jnp.inf)
        m0 = jnp.max(s, axis=0, keepdims=True)
        p = jnp.exp2(s - m0)
        m_scr[...] = m0
        l_scr[...] = jnp.sum(p, axis=0, keepdims=True)
        acc_scr[...] = lax.dot_general(c_new, p.astype(BF16), dn_first, preferred_element_type=F32)

    def stage_a(slot):
        cb = cbuf[slot].astype(BF16)
        cb_scr[slot] = cb
        rs = _key_norm_factor(cb, wk_ref, n_heads, nope)
        s = lax.dot_general(cb, qa, dn_last, preferred_element_type=F32) * rs
        krb = krbuf[slot].astype(BF16)
        s_scr[slot] = s + lax.dot_general(krb, qr, (((0,), (1,)), ((), ())), preferred_element_type=F32)

    def stage_b(slot):
        s = s_scr[slot]
        m_old = m_scr[...]
        m_new = jnp.maximum(m_old, jnp.max(s, axis=0, keepdims=True))
        p = jnp.exp2(s - m_new)
        alpha = jnp.exp2(m_old - m_new)
        l_scr[...] = alpha * l_scr[...] + jnp.sum(p, axis=0, keepdims=True)
        pv = lax.dot_general(cb_scr[slot], p.astype(BF16), dn_first, preferred_element_type=F32)
        acc_scr[...] = alpha * acc_scr[...] + pv
        m_scr[...] = m_new

    for i in range(n_groups + 1):
        slot = i % 2
        if i + 1 < n_groups:
            fetch(b, i + 1, 1 - slot)
        elif i + 1 == n_groups:
            @pl.when(b + 1 < pl.num_programs(0))
            def _():
                fetch(b + 1, 0, 0)
        if i < n_groups:
            wait(b, i, slot)
            if i == 0:
                start_from_new_tokens()
            stage_a(slot)
        if i >= 1:
            stage_b(1 - slot)
    ctx = acc_scr[...] / l_scr[...]
    o_ref[...] = jnp.dot(ctx, perm_ref[...], preferred_element_type=F32).T


def _decode_attn(page_table, cache_ckv, cache_krope, layer, wk_perm, qa, qr, c_new, kr_new, rs_new, *, pg, dims):
    n_heads, _, kv_lora, nope, rope, _, _ = dims
    b, n_pages = page_table.shape
    page = cache_ckv.shape[2]
    n_new = c_new.shape[1]
    rows = n_new * n_heads
    n_groups = n_pages // pg
    keys = pg * page
    assert rows == V7X_LANES and n_pages % pg == 0 and n_groups % 2 == 0 and n_new <= 2 * V7X_SUBLANES
    krope_t = jnp.swapaxes(cache_krope, 2, 3)
    lane = jnp.arange(rows)
    perm = (lane[:, None] == (lane[None, :] % n_new) * n_heads + lane[None, :] // n_new).astype(F32)
    per_seq = lambda r, c: pl.BlockSpec((None, r, c), lambda i, pt: (i, 0, 0))
    hbm = pl.BlockSpec(memory_space=pl.ANY)
    grid_spec = pltpu.PrefetchScalarGridSpec(
        num_scalar_prefetch=1, grid=(b,),
        in_specs=[hbm, hbm, pl.BlockSpec(wk_perm.shape, lambda i, pt: (0, 0)),
                  pl.BlockSpec(perm.shape, lambda i, pt: (0, 0)), per_seq(rows, kv_lora), per_seq(rows, V7X_LANES),
                  per_seq(n_new, kv_lora), per_seq(n_new, V7X_LANES), per_seq(n_new, V7X_LANES)],
        out_specs=per_seq(rows, kv_lora),
        scratch_shapes=[pltpu.VMEM((2, keys, kv_lora), F32), pltpu.VMEM((2, rope, keys), F32),
                        pltpu.VMEM((2, keys, kv_lora), BF16), pltpu.VMEM((2, keys, V7X_LANES), F32),
                        pltpu.VMEM((1, V7X_LANES), F32), pltpu.VMEM((1, V7X_LANES), F32),
                        pltpu.VMEM((kv_lora, V7X_LANES), F32),
                        pltpu.SemaphoreType.DMA((2,)), pltpu.SemaphoreType.DMA((2,))])
    return pl.pallas_call(
        functools.partial(_decode_attn_kernel, layer=layer, pg=pg, n_groups=n_groups, n_heads=n_heads, nope=nope,
                          rope=rope),
        grid_spec=grid_spec, out_shape=jax.ShapeDtypeStruct((b, rows, kv_lora), F32),
        compiler_params=_params("arbitrary"), name="mla_decode_attn",
    )(page_table, cache_ckv, krope_t, wk_perm, perm, qa, qr, c_new, kr_new, rs_new)


def _unabsorb_kernel(ctx_ref, wv_ref, o_ref, *, n_pairs, n_q):
    bb, _, kv_lora = ctx_ref.shape
    head = lambda h: ctx_ref[:, n_q * h:n_q * (h + 1), :].reshape(bb * n_q, kv_lora)
    for p in range(n_pairs):
        x = jnp.concatenate([head(2 * p), head(2 * p + 1)], axis=1).astype(BF16)
        o_ref[:, V7X_LANES * p:V7X_LANES * (p + 1)] = jnp.dot(x, wv_ref[p], preferred_element_type=F32).astype(BF16)


def _unabsorb(ctx, wv2, *, bb, n_q):
    b, rows, kv_lora = ctx.shape
    n_pairs = wv2.shape[0]
    return pl.pallas_call(
        functools.partial(_unabsorb_kernel, n_pairs=n_pairs, n_q=n_q),
        grid=(b // bb,),
        in_specs=[pl.BlockSpec((bb, rows, kv_lora), lambda i: (i, 0, 0)), _const_spec(wv2)],
        out_specs=pl.BlockSpec((bb * n_q, n_pairs * V7X_LANES), lambda i: (i, 0)),
        out_shape=jax.ShapeDtypeStruct((b * n_q, n_pairs * V7X_LANES), BF16),
        compiler_params=_params("parallel"), name="mla_unabsorb_v",
    )(ctx, wv2)


def _conv_prompt(g, w, b, buf_ref, carry):
    rows = g.shape[0]
    width = w.shape[0]
    buf_ref[0:V7X_SUBLANES, :] = carry
    buf_ref[V7X_SUBLANES:, :] = g
    y = b + g * w[width - 1:width, :]
    for s in range(1, width):
        y = y + buf_ref[V7X_SUBLANES - s:V7X_SUBLANES - s + rows, :] * w[width - 1 - s:width - s, :]
    return y


def _conv_decode(g, w, b, prev, t_idx):
    width = w.shape[0]
    rows = g.shape[0]
    y = b + g * w[width - 1:width, :]
    for s in range(1, width):
        sh = jnp.where(t_idx >= s, pltpu.roll(g, s, axis=0), pltpu.roll(prev, rows - V7X_SUBLANES + s, axis=0))
        y = y + sh * w[width - 1 - s:width - s, :]
    return y


def _ffn_kernel(*refs, decode, fc):
    if decode:
        x_ref, a_ref, wpre_ref, gn_ref, wup_ref, cw_ref, cb_ref, wd_ref, st_ref, y_ref, so_ref = refs
    else:
        (x_ref, a_ref, wpre_ref, gn_ref, wup_ref, cw_ref, cb_ref, wd_ref,
         y_ref, so_ref, carry_scr, buf_scr) = refs
    bb, tt, d = x_ref.shape
    rows = bb * tt
    dff = wd_ref.shape[0]
    a = a_ref[...].reshape(rows, a_ref.shape[2])
    x1 = x_ref[...].reshape(rows, d) + jnp.dot(a, wpre_ref[...], preferred_element_type=F32)
    xn = _rms(x1, gn_ref[...]).astype(BF16)
    if decode:
        t_idx = lax.broadcasted_iota(jnp.int32, (rows, fc), 0) % tt
    else:
        @pl.when(pl.program_id(1) == 0)
        def _():
            carry_scr[...] = jnp.zeros(carry_scr.shape, F32)

    acc = x1
    for c in range(dff // fc):
        cols = slice(c * fc, (c + 1) * fc)
        g = jnp.dot(xn, wup_ref[:, cols], preferred_element_type=F32)
        u = jnp.dot(xn, wup_ref[:, dff + c * fc:dff + (c + 1) * fc], preferred_element_type=F32)
        if decode:
            gc = _conv_decode(g, cw_ref[:, cols], cb_ref[:, cols], st_ref[:, :, cols].reshape(rows, fc), t_idx)
            so_ref[:, :, cols] = g.reshape(bb, tt, fc)
        else:
            gc = _conv_prompt(g, cw_ref[:, cols], cb_ref[:, cols], buf_scr.at[c], carry_scr[:, cols])
            last = g[rows - V7X_SUBLANES:, :]
            carry_scr[:, cols] = last
            so_ref[0, :, cols] = last
        h = (gc * _sigmoid(gc) * u).astype(BF16)
        acc = acc + jnp.dot(h, wd_ref[cols, :], preferred_element_type=F32)
    y_ref[...] = acc.reshape(bb, tt, d)


def _resident_spec(a, layer=None):
    if layer is None:
        nd = a.ndim
        return pl.BlockSpec(a.shape, lambda *_: (0,) * nd, pipeline_mode=pl.Buffered(1))
    nd = a.ndim - 1
    return pl.BlockSpec((None,) + a.shape[1:], lambda *_: (layer,) + (0,) * nd, pipeline_mode=pl.Buffered(1))


def _ffn(x, a, w_pre, gn, wup, cw, cb, wd, state, *, layer, bb, tt, fc):
    b, t, d = x.shape
    da = a.shape[2]
    dff = wd.shape[1]
    rows = bb * tt
    decode = state is not None
    tok = lambda c: pl.BlockSpec((bb, tt, c), lambda i, j: (i, j, 0))
    weights = [w_pre, gn, wup, cw, cb, wd]
    in_specs = [tok(d), tok(da), _resident_spec(w_pre)] + [_resident_spec(w, layer) for w in weights[1:]]
    args = [x, a] + weights
    scratch = []
    if decode:
        in_specs.append(pl.BlockSpec((bb, tt, dff), lambda i, j: (i, 0, 0)))
        args.append(state)
    else:
        assert bb == 1
        scratch = [pltpu.VMEM((V7X_SUBLANES, dff), F32), pltpu.VMEM((dff // fc, V7X_SUBLANES + rows, fc), F32)]
    so_spec = pl.BlockSpec((bb, None, V7X_SUBLANES, dff), lambda i, j: (i, j, 0, 0))
    y, so = pl.pallas_call(
        functools.partial(_ffn_kernel, decode=decode, fc=fc),
        grid=(b // bb, t // tt), in_specs=in_specs,
        out_specs=[tok(d), so_spec],
        out_shape=[jax.ShapeDtypeStruct((b, t, d), F32),
                   jax.ShapeDtypeStruct((b, t // tt, V7X_SUBLANES, dff), F32)],
        scratch_shapes=scratch,
        compiler_params=_params("parallel", "arbitrary"), name="conv_ffn",
    )(*args)
    return y, so[:, -1]


def _rec_kernel(*refs, decode):
    if decode:
        (x_ref, gn_ref, win_ref, cw_ref, cb_ref, wgate_ref, ba_ref, bi_ref, lam_ref, st_ref, h0_ref,
         a_ref, co_ref, ho_ref, a_scr, b_scr) = refs
    else:
        (x_ref, gn_ref, win_ref, cw_ref, cb_ref, wgate_ref, ba_ref, bi_ref, lam_ref,
         a_ref, co_ref, ho_ref, a_scr, b_scr, h_scr, carry_scr, hc_scr, buf_scr) = refs
    bb, tt, d = x_ref.shape
    rows = bb * tt
    w = cw_ref.shape[1]
    t = pl.program_id(1)
    xn = _rms(x_ref[...].reshape(rows, d), gn_ref[...]).astype(BF16)
    z = jnp.dot(xn, win_ref[...], preferred_element_type=F32)
    xr = z[:, :w]
    gin = z[:, w:]
    gate = 0.5 * gin * (1.0 + jnp.tanh(0.7978845608028654 * (gin + 0.044715 * (gin * gin * gin))))
    t_idx = lax.broadcasted_iota(jnp.int32, (rows, w), 0) % V7X_SUBLANES
    if decode:
        xc = _conv_decode(xr, cw_ref[...], cb_ref[...], st_ref[...].reshape(rows, w), t_idx)
        co_ref[...] = xr.reshape(bb, tt, w)
    else:
        @pl.when(t == 0)
        def _():
            carry_scr[...] = jnp.zeros(carry_scr.shape, F32)
            hc_scr[...] = jnp.zeros(hc_scr.shape, F32)

        xc = _conv_prompt(xr, cw_ref[...], cb_ref[...], buf_scr, carry_scr[...])
        last = xr[rows - V7X_SUBLANES:, :]
        carry_scr[...] = last
        co_ref[0] = last

    n_blk = wgate_ref.shape[0]
    kb = wgate_ref.shape[1]
    xcb = xc.astype(BF16)
    ra, ri = [], []
    for p in range(n_blk):
        zz = jnp.dot(xcb[:, kb * p:kb * (p + 1)], wgate_ref[p], preferred_element_type=F32)
        ra.append(zz[:, :kb])
        ri.append(zz[:, kb:])
    r = _sigmoid(jnp.concatenate(ra, axis=1) + ba_ref[...])
    gi = _sigmoid(jnp.concatenate(ri, axis=1) + bi_ref[...])
    neg_lam = -lam_ref[...]
    softplus = jnp.maximum(neg_lam, 0.0) + jnp.log(1.0 + jnp.exp(-jnp.abs(neg_lam)))
    av = jnp.exp(r * (-LRU_C * softplus))
    bv = jnp.sqrt(1.0 - av * av) * (gi * xc)

    grp3 = (rows // V7X_SUBLANES, V7X_SUBLANES, w)
    av, bv = av.reshape(grp3), bv.reshape(grp3)
    t3 = lax.broadcasted_iota(jnp.int32, grp3, 1)
    for s in (1, 2, 4):
        keep = t3 >= s
        bv = jnp.where(keep, av * pltpu.roll(bv, s, axis=1) + bv, bv)
        av = jnp.where(keep, av * pltpu.roll(av, s, axis=1), av)
    av, bv = av.reshape(rows, w), bv.reshape(rows, w)
    if decode:
        hs = av * h0_ref[...].reshape(rows, w) + bv
        ho_ref[...] = hs.reshape(bb, tt, w)
    else:
        a_scr[...] = av
        b_scr[...] = bv

        def group(i, h_in):
            sl = pl.ds(pl.multiple_of(i * V7X_SUBLANES, V7X_SUBLANES), V7X_SUBLANES)
            hg = a_scr[sl, :] * h_in + b_scr[sl, :]
            h_scr[sl, :] = hg
            return jnp.broadcast_to(hg[V7X_SUBLANES - 1:, :], (V7X_SUBLANES, w))

        h_last = lax.fori_loop(0, rows // V7X_SUBLANES, group, hc_scr[...])
        hc_scr[...] = h_last
        ho_ref[0] = h_last
        hs = h_scr[...]
    a_ref[...] = (hs * gate).astype(BF16).reshape(bb, tt, w)


def _rec(x, gn, w_in, cw, cb, w_gate, b_a, b_i, lam, state, h0, *, bb, tt):
    b, t, d = x.shape
    w = cw.shape[1]
    rows = bb * tt
    decode = state is not None
    tok = lambda c: pl.BlockSpec((bb, tt, c), lambda i, j: (i, j, 0))
    consts = (gn, w_in, cw, cb, w_gate, b_a, b_i, lam)
    in_specs = [tok(d)] + [_const_spec(a) for a in consts]
    args = [x, *consts]
    scratch = [pltpu.VMEM((rows, w), F32), pltpu.VMEM((rows, w), F32)]
    grp = pl.BlockSpec((bb, V7X_SUBLANES, w), lambda i, j: (i, 0, 0))
    if decode:
        in_specs += [grp, grp]
        args += [state, h0]
    else:
        assert bb == 1
        scratch += [pltpu.VMEM((rows, w), F32), pltpu.VMEM((V7X_SUBLANES, w), F32), pltpu.VMEM((V7X_SUBLANES, w), F32),
                    pltpu.VMEM((V7X_SUBLANES + rows, w), F32)]
    return pl.pallas_call(
        functools.partial(_rec_kernel, decode=decode),
        grid=(b // bb, t // tt), in_specs=in_specs,
        out_specs=[tok(w), grp, grp],
        out_shape=[jax.ShapeDtypeStruct((b, t, w), BF16), jax.ShapeDtypeStruct((b, V7X_SUBLANES, w), F32),
                   jax.ShapeDtypeStruct((b, V7X_SUBLANES, w), F32)],
        scratch_shapes=scratch,
        compiler_params=_params("parallel", "arbitrary"), name="rglru_mixer",
    )(*args)


def _pad_state(buf):
    return jnp.pad(buf, ((0, 0), (V7X_SUBLANES - buf.shape[1], 0), (0, 0)))


class _Tiles(NamedTuple):
    prompt_rows: int
    decode_seqs: int
    attn_q_rows: int
    attn_heads: int
    ffn_chunk: int
    pages_per_group: int


def _tiles(tp, bs, ts, n_heads, vh, dff, n_pages):
    heads = 4 if n_heads % 4 == 0 and (4 * vh) % V7X_LANES == 0 else 2
    return _Tiles(prompt_rows=min(tp, 512), decode_seqs=min(bs, 256 // ts), attn_q_rows=min(tp, 256),
                  attn_heads=heads, ffn_chunk=min(dff, 1024), pages_per_group=min(n_pages, 16))


def kernel(x_prompt, x_sample, cache_ckv, cache_krope, page_table, state_lru_h, state_lru_conv, state_ffn_conv,
           norm_mix, norm_ffn, attn_w_down, attn_g_q_lora, attn_w_uq, attn_g_kv_lora, attn_g_qn, attn_g_qr,
           attn_g_kn, attn_g_kr, attn_w_uk, attn_w_uv, attn_w_o, rec_w_in, rec_conv_w, rec_conv_b, rec_w_a,
           rec_b_a, rec_w_i, rec_b_i, rec_lambda, rec_w_out, ffn_w_up, ffn_conv_w, ffn_conv_b, ffn_w_down):
    bp, tp, d = x_prompt.shape
    bs, ts, _ = x_sample.shape
    depth = norm_mix.shape[0]
    n_mixers = 2
    kv_lora, n_heads, nope = attn_w_uk.shape[1:]
    vh = attn_w_uv.shape[3]
    rope = attn_g_qr.shape[1]
    q_lora = attn_g_q_lora.shape[1]
    half = rope // 2
    page = cache_ckv.shape[2]
    past = page_table.shape[1] * page
    dff = ffn_conv_w.shape[2]
    lru_w = rec_conv_w.shape[2]
    assert ts == V7X_SUBLANES and rope + nope + half <= V7X_LANES and vh < V7X_LANES and n_heads % 2 == 0
    scale = float((nope + rope) ** -0.5) * math.log2(math.e)
    dims = (n_heads, q_lora, kv_lora, nope, rope, vh, scale)
    pad_l = V7X_LANES - rope - nope
    row = lambda v: v.reshape(1, -1).astype(F32)

    def rope_tables(pos):
        inv = ROPE_THETA ** (-jnp.arange(0, rope, 2, dtype=F32) / rope)
        ang = pos.astype(F32)[:, None] * inv[None, :]
        cos, sin = jnp.cos(ang), jnp.sin(ang)
        n = pos.shape[0]
        ones = jnp.ones((n, V7X_LANES - rope - half), F32)
        return (jnp.concatenate([cos, cos, ones, jnp.zeros((n, half), F32)], axis=1),
                jnp.concatenate([-sin, sin, jnp.zeros((n, V7X_LANES - rope), F32)], axis=1))

    tabs_p = rope_tables(jnp.arange(tp))
    tabs_s = rope_tables(past + jnp.arange(ts))

    xp, xs = x_prompt, x_sample
    a_p = a_s = w_pre = None
    ckv_p, kr_p, ckv_s, kr_s = [], [], [], []
    lh_p, lh_s, lc_p, lc_s = [], [], [], []
    fc_p, fc_s = [], []
    tiles = _tiles(tp, bs, ts, n_heads, vh, dff, page_table.shape[1])
    ffn_stacked = (norm_ffn[:, None, :], ffn_w_up.astype(BF16), ffn_conv_w, ffn_conv_b[:, None, :],
                   ffn_w_down.astype(BF16))
    for i in range(depth):
        j = i // n_mixers
        gmix = row(norm_mix[i])
        if i % n_mixers == 0:
            wdn = attn_w_down[j]
            wd = jnp.concatenate([wdn, jnp.zeros((d, V7X_LANES - rope - half), F32), wdn[:, -half:]],
                                 axis=1).astype(BF16)
            gkr = jnp.concatenate([attn_g_kr[j], jnp.zeros((V7X_LANES - rope - half,), F32),
                                   attn_g_kr[j][half:]]).reshape(1, -1)
            wq = attn_w_uq[j].reshape(q_lora, n_heads, nope + rope)
            wuq = jnp.concatenate([wq[:, :, nope:], wq[:, :, :nope], jnp.zeros((q_lora, n_heads, pad_l - half), F32),
                                   wq[:, :, nope + half:]],
                                  axis=2).reshape(q_lora, n_heads * V7X_LANES).astype(BF16)
            gq = jnp.concatenate([attn_g_qr[j], attn_g_qn[j], jnp.zeros((pad_l - half,), F32),
                                  attn_g_qr[j][half:]]).reshape(1, -1)
            wk = attn_w_uk[j]
            wuk = jnp.concatenate([jnp.zeros((kv_lora, n_heads, rope), F32), wk,
                                   jnp.zeros((kv_lora, n_heads, pad_l), F32)],
                                  axis=2).reshape(kv_lora, n_heads * V7X_LANES).astype(BF16)
            gk = jnp.concatenate([jnp.zeros((rope,), F32), attn_g_kn[j], jnp.zeros((pad_l,), F32)]).reshape(1, -1)
            wuv = jnp.concatenate([attn_w_uv[j], jnp.zeros((kv_lora, n_heads, V7X_LANES - vh), F32)],
                                  axis=2).reshape(kv_lora, n_heads * V7X_LANES).astype(BF16)
            vone = jnp.tile((jnp.arange(V7X_LANES) == vh).astype(F32), n_heads).reshape(1, -1)
            proj_w = (gmix, wd, row(attn_g_q_lora[j]), row(attn_g_kv_lora[j]), gkr, wuq, gq, wuk, gk, wuv, vone)

            q_p, c_p, krb_p, k_p, v_p = _mla_proj(xp, *proj_w, *tabs_p, bb=1, tt=tiles.prompt_rows, dims=dims,
                                                  with_kv=True)
            wk_perm = jnp.transpose(wk, (0, 2, 1)).reshape(kv_lora, nope * n_heads).astype(BF16)
            proj_s = proj_w[:7] + (wk_perm,) + proj_w[8:]
            q_s, c_s, krb_s, rs_s = _mla_proj(xs, *proj_s, *tabs_s, bb=tiles.decode_seqs, tt=ts, dims=dims,
                                              with_kv=False)
            amax = lambda g: jnp.max(jnp.abs(g))
            score_bound = 1.02 * scale * (nope * amax(attn_g_qn[j]) * amax(attn_g_kn[j])
                                          + rope * amax(attn_g_qr[j]) * amax(attn_g_kr[j]))
            bounded = (score_bound <= SCORE_EXP2_LIMIT).astype(jnp.int32).reshape(1)
            a_p = _flash(bounded, q_p, k_p, v_p, tq=tiles.attn_q_rows, n_heads=n_heads, vh=vh,
                         hps=tiles.attn_heads)

            wa = jnp.transpose(wk * attn_g_kn[j][None, None, :], (1, 2, 0))
            wa = jnp.concatenate([jnp.zeros((n_heads, rope, kv_lora), F32), wa,
                                  jnp.zeros((n_heads, pad_l, kv_lora), F32)], axis=1).astype(BF16)
            qa = _absorb(q_s.reshape(bs * ts, n_heads * V7X_LANES), wa, tr=tiles.decode_seqs * ts)
            qa = qa.reshape(bs, ts * n_heads, kv_lora)
            qr = q_s.reshape(bs, ts * n_heads, V7X_LANES)
            ctx = _decode_attn(page_table, cache_ckv, cache_krope, j, wk_perm, qa, qr, c_s, krb_s, rs_s,
                               pg=tiles.pages_per_group, dims=dims)
            wv = attn_w_uv[j]
            zer = jnp.zeros((kv_lora, n_heads // 2, vh), F32)
            wv2 = jnp.concatenate([jnp.concatenate([wv[:, 0::2], zer], axis=2),
                                   jnp.concatenate([zer, wv[:, 1::2]], axis=2)], axis=0)
            wv2 = jnp.transpose(wv2, (1, 0, 2)).astype(BF16)
            a_s = _unabsorb(ctx, wv2, bb=tiles.decode_seqs, n_q=ts)
            a_s = a_s.reshape(bs, ts, n_heads * vh)
            w_pre = attn_w_o[j].astype(BF16)
            ckv_p.append(c_p)
            kr_p.append(krb_p[:, :, :rope])
            ckv_s.append(c_s)
            kr_s.append(krb_s[:, :, :rope])
        else:
            n_blk, blk = rec_w_a.shape[1:3]
            assert n_blk % 2 == 0

            def pair(wm):
                z = jnp.zeros((n_blk // 2, blk, blk), F32)
                return jnp.concatenate([jnp.concatenate([wm[0::2], z], axis=2),
                                        jnp.concatenate([z, wm[1::2]], axis=2)], axis=1)

            w_gate = jnp.concatenate([pair(rec_w_a[j]), pair(rec_w_i[j])], axis=2).astype(BF16)
            rec_w = (gmix, rec_w_in[j].astype(BF16), rec_conv_w[j], row(rec_conv_b[j]), w_gate, row(rec_b_a[j]),
                     row(rec_b_i[j]), row(rec_lambda[j]))
            a_p, cv_p, h_p = _rec(xp, *rec_w, None, None, bb=1, tt=tiles.prompt_rows)
            h0 = jnp.broadcast_to(state_lru_h[j][:, None, :], (bs, ts, lru_w))
            a_s, cv_s, h_s = _rec(xs, *rec_w, _pad_state(state_lru_conv[j]), h0, bb=tiles.decode_seqs, tt=ts)
            w_pre = rec_w_out[j].astype(BF16)
            kw = rec_conv_w.shape[1] - 1
            lh_p.append(h_p[:, V7X_SUBLANES - 1])
            lh_s.append(h_s[:, V7X_SUBLANES - 1])
            lc_p.append(cv_p[:, V7X_SUBLANES - kw:])
            lc_s.append(cv_s[:, V7X_SUBLANES - kw:])
        ffn_w = (w_pre,) + ffn_stacked
        xp, fb_p = _ffn(xp, a_p, *ffn_w, None, layer=i, bb=1, tt=tiles.prompt_rows, fc=tiles.ffn_chunk)
        xs, fb_s = _ffn(xs, a_s, *ffn_w, _pad_state(state_ffn_conv[i]), layer=i, bb=tiles.decode_seqs, tt=ts,
                        fc=tiles.ffn_chunk)
        kf = ffn_conv_w.shape[1] - 1
        fc_p.append(fb_p[:, V7X_SUBLANES - kf:])
        fc_s.append(fb_s[:, V7X_SUBLANES - kf:])
    return (xp, xs, jnp.stack(ckv_p), jnp.stack(kr_p), jnp.stack(ckv_s), jnp.stack(kr_s),
            jnp.stack(lh_p), jnp.stack(lh_s), jnp.stack(lc_p), jnp.stack(lc_s), jnp.stack(fc_p), jnp.stack(fc_s))
```

```python
import functools
import math
from typing import NamedTuple

import jax
import jax.numpy as jnp
from jax import lax
from jax.experimental import pallas as pl
from jax.experimental.pallas import tpu as pltpu

EPS = 1e-6
ROPE_THETA = 10000.0
LRU_C = 8.0
SCORE_EXP2_LIMIT = 64.0

V7X_LANES = 128
V7X_SUBLANES = 8
V7X_VMEM_BYTES = 64 * 1024 * 1024
VMEM_LIMIT = V7X_VMEM_BYTES * 7 // 8

F32 = jnp.float32
BF16 = jnp.bfloat16


def _params(*sem):
    return pltpu.CompilerParams(dimension_semantics=sem, vmem_limit_bytes=VMEM_LIMIT)


def _const_spec(a):
    nd = a.ndim
    return pl.BlockSpec(a.shape, lambda *_: (0,) * nd)


def _rms(x, g):
    ms = jnp.mean(x * x, axis=-1, keepdims=True)
    return x * lax.rsqrt(ms + EPS) * g


def _sigmoid(x):
    return 1.0 / (1.0 + jnp.exp(-x))


def _rope(y, cos, sin, half):
    return y * cos + pltpu.roll(y, half, axis=1) * sin


def _tile_rows(tab_ref, bb):
    t = tab_ref[...]
    if bb == 1:
        return t
    return jnp.broadcast_to(t[None], (bb,) + t.shape).reshape(bb * t.shape[0], t.shape[1])


def _key_norm_factor(cb, wk_ref, n_heads, nope):
    k = jnp.dot(cb, wk_ref[...], preferred_element_type=F32)
    ksq = k * k
    part = ksq[:, :V7X_LANES]
    for i in range(1, wk_ref.shape[1] // V7X_LANES):
        part = part + ksq[:, V7X_LANES * i:V7X_LANES * (i + 1)]
    shift = V7X_LANES // 2
    while shift >= n_heads:
        part = part + pltpu.roll(part, shift, axis=1)
        shift //= 2
    return lax.rsqrt(part / nope + EPS)


def _mla_proj_kernel(x_ref, gmix_ref, wd_ref, gql_ref, gkv_ref, gkr_ref, wuq_ref, gq_ref, wuk_ref, gk_ref, wuv_ref,
                     vone_ref, cos_ref, sin_ref, *out_refs, n_heads, q_lora, kv_lora, nope, rope, scale, with_kv):
    if with_kv:
        q_ref, ckv_ref, kr_ref, k_ref, v_ref = out_refs
    else:
        q_ref, ckv_ref, kr_ref, rs_ref = out_refs
    bb, tt, d = x_ref.shape
    rows = bb * tt
    half = rope // 2
    xn = _rms(x_ref[...].reshape(rows, d), gmix_ref[...])
    dd = jnp.dot(xn.astype(BF16), wd_ref[...], preferred_element_type=F32)
    cq = _rms(dd[:, :q_lora], gql_ref[...])
    ckv = _rms(dd[:, q_lora:q_lora + kv_lora], gkv_ref[...])
    kr = dd[:, q_lora + kv_lora:]
    cos = _tile_rows(cos_ref, bb)
    sin = _tile_rows(sin_ref, bb)
    lane = lax.broadcasted_iota(jnp.int32, (1, V7X_LANES), 1)
    is_rope = lane < rope
    is_nope = jnp.logical_and(lane >= rope, lane < rope + nope)

    kr_ss = jnp.sum(jnp.where(is_rope, kr * kr, 0.0), axis=-1, keepdims=True)
    krr = _rope(kr * lax.rsqrt(kr_ss / rope + EPS) * gkr_ref[...], cos, sin, half)
    kr_ref[...] = krr.reshape(bb, tt, V7X_LANES)
    ckv_ref[...] = ckv.reshape(bb, tt, kv_lora)

    q = jnp.dot(cq.astype(BF16), wuq_ref[...], preferred_element_type=F32)
    ckv_b = ckv.astype(BF16)
    if with_kv:
        kk = jnp.dot(ckv_b, wuk_ref[...], preferred_element_type=F32)
        vv = jnp.dot(ckv_b, wuv_ref[...], preferred_element_type=F32) + vone_ref[...]
        v_ref[...] = vv.astype(BF16).reshape(v_ref.shape)
    else:
        rs_ref[...] = _key_norm_factor(ckv_b, wuk_ref, n_heads, nope).reshape(bb, tt, V7X_LANES)
    for h in range(n_heads):
        sl = slice(V7X_LANES * h, V7X_LANES * (h + 1))
        qb = q[:, sl]
        sq = qb * qb
        ss_r = jnp.sum(jnp.where(is_rope, sq, 0.0), axis=-1, keepdims=True)
        ss_n = jnp.sum(jnp.where(is_nope, sq, 0.0), axis=-1, keepdims=True)
        rs = jnp.where(is_nope, lax.rsqrt(ss_n / nope + EPS), lax.rsqrt(ss_r / rope + EPS))
        y = _rope(qb * rs * gq_ref[...], cos, sin, half)
        q_ref[:, :, sl] = (y * scale).astype(BF16).reshape(bb, tt, V7X_LANES)
        if with_kv:
            kb = kk[:, sl]
            ss_k = jnp.sum(kb * kb, axis=-1, keepdims=True)
            kn = kb * lax.rsqrt(ss_k / nope + EPS) * gk_ref[...]
            k_ref[:, :, sl] = (kn + krr).astype(BF16).reshape(bb, tt, V7X_LANES)


def _mla_proj(x, gmix, wd, gql, gkv, gkr, wuq, gq, wuk, gk, wuv, vone, cos, sin, *, bb, tt, dims, with_kv):
    n_heads, q_lora, kv_lora, nope, rope, vh, scale = dims
    b, t, d = x.shape
    hl = n_heads * V7X_LANES
    tok = lambda c: pl.BlockSpec((bb, tt, c), lambda i, j: (i, j, 0))
    tab = pl.BlockSpec((tt, V7X_LANES), lambda i, j: (j, 0))
    out_shape = [jax.ShapeDtypeStruct((b, t, hl), BF16), jax.ShapeDtypeStruct((b, t, kv_lora), F32),
                 jax.ShapeDtypeStruct((b, t, V7X_LANES), F32)]
    out_specs = [tok(hl), tok(kv_lora), tok(V7X_LANES)]
    if with_kv:
        out_shape += [jax.ShapeDtypeStruct((b, t, hl), BF16), jax.ShapeDtypeStruct((b, t, hl), BF16)]
        out_specs += [tok(hl), tok(hl)]
    else:
        out_shape.append(jax.ShapeDtypeStruct((b, t, V7X_LANES), F32))
        out_specs.append(tok(V7X_LANES))
    consts = (gmix, wd, gql, gkv, gkr, wuq, gq, wuk, gk, wuv, vone)
    return pl.pallas_call(
        functools.partial(_mla_proj_kernel, n_heads=n_heads, q_lora=q_lora, kv_lora=kv_lora, nope=nope, rope=rope,
                          scale=scale, with_kv=with_kv),
        grid=(b // bb, t // tt),
        in_specs=[tok(d)] + [_const_spec(a) for a in consts] + [tab, tab],
        out_specs=out_specs, out_shape=out_shape,
        compiler_params=_params("parallel", "parallel"), name="mla_proj",
    )(x, *consts, cos, sin)


def _attn_block(q, k_ref, v_ref, lanes, vh, past, tq, bounded):
    dn = (((1,), (1,)), ((), ()))
    row = lax.broadcasted_iota(jnp.int32, (tq, tq), 0)
    col = lax.broadcasted_iota(jnp.int32, (tq, tq), 1)
    s_d = lax.dot_general(q, k_ref[0, past:past + tq, lanes], dn, preferred_element_type=F32)
    s_d = jnp.where(col <= row, s_d, -jnp.inf)
    if past:
        s_p = lax.dot_general(q, k_ref[0, :past, lanes], dn, preferred_element_type=F32)
    if not bounded:
        m = jnp.max(s_d, axis=1, keepdims=True)
        if past:
            m = jnp.maximum(m, jnp.max(s_p, axis=1, keepdims=True))
            s_p = s_p - m
        s_d = s_d - m
    acc = jnp.dot(jnp.exp2(s_d).astype(BF16), v_ref[0, past:past + tq, lanes], preferred_element_type=F32)
    if past:
        acc = acc + jnp.dot(jnp.exp2(s_p).astype(BF16), v_ref[0, :past, lanes], preferred_element_type=F32)
    return acc[:, :vh] / acc[:, vh:vh + 1]


def _flash_kernel(bounded_ref, q_ref, k_ref, v_ref, o_ref, *, tq, vh, hps):
    is_bounded = bounded_ref[0] != 0
    for bounded in (True, False):
        @pl.when(is_bounded == bounded)
        def _(bounded=bounded):
            for n in range(q_ref.shape[1] // tq):
                rows = slice(n * tq, (n + 1) * tq)
                outs = []
                for hh in range(hps):
                    lanes = slice(V7X_LANES * hh, V7X_LANES * (hh + 1))
                    outs.append(_attn_block(q_ref[0, rows, lanes], k_ref, v_ref, lanes, vh, n * tq, tq, bounded))
                o_ref[0, rows, :] = jnp.concatenate(outs, axis=1).astype(BF16)


def _flash(bounded, q, k, v, *, tq, n_heads, vh, hps):
    b, t, _ = q.shape
    seq = lambda c: pl.BlockSpec((1, t, hps * c), lambda i, h, f: (i, 0, h))
    grid_spec = pltpu.PrefetchScalarGridSpec(
        num_scalar_prefetch=1, grid=(b, n_heads // hps),
        in_specs=[seq(V7X_LANES), seq(V7X_LANES), seq(V7X_LANES)], out_specs=seq(vh))
    return pl.pallas_call(
        functools.partial(_flash_kernel, tq=tq, vh=vh, hps=hps),
        grid_spec=grid_spec, out_shape=jax.ShapeDtypeStruct((b, t, n_heads * vh), BF16),
        compiler_params=_params("parallel", "parallel"), name="mla_prompt_attn",
    )(bounded, q, k, v)


def _absorb_kernel(q_ref, wa_ref, qa_ref, *, n_heads, kv_lora):
    for h in range(n_heads):
        qh = q_ref[:, V7X_LANES * h:V7X_LANES * (h + 1)]
        qa_ref[:, kv_lora * h:kv_lora * (h + 1)] = jnp.dot(qh, wa_ref[h], preferred_element_type=F32).astype(BF16)


def _absorb(q2d, wa, *, tr):
    r = q2d.shape[0]
    n_heads, _, kv_lora = wa.shape
    return pl.pallas_call(
        functools.partial(_absorb_kernel, n_heads=n_heads, kv_lora=kv_lora),
        grid=(r // tr,),
        in_specs=[pl.BlockSpec((tr, n_heads * V7X_LANES), lambda i: (i, 0)), _const_spec(wa)],
        out_specs=pl.BlockSpec((tr, n_heads * kv_lora), lambda i: (i, 0)),
        out_shape=jax.ShapeDtypeStruct((r, n_heads * kv_lora), BF16),
        compiler_params=_params("parallel"), name="mla_absorb_q",
    )(q2d, wa)


def _decode_attn_kernel(pt_ref, ckv_hbm, krt_hbm, wk_ref, perm_ref, qa_ref, qr_ref, cnew_ref, krnew_ref, rsnew_ref,
                        o_ref, cbuf, krbuf, cb_scr, s_scr, m_scr, l_scr, acc_scr, csem, ksem,
                        *, layer, pg, n_groups, n_heads, nope, rope):
    b = pl.program_id(0)
    page = ckv_hbm.shape[2]
    kv_lora = ckv_hbm.shape[3]
    dn_last = (((1,), (1,)), ((), ()))
    dn_first = (((0,), (0,)), ((), ()))
    qa = qa_ref[...]
    qr = qr_ref[:, :rope]

    def copies(seq, grp, slot):
        out = []
        for s in range(pg):
            pid = pt_ref[seq, grp * pg + s]
            rows = pl.ds(s * page, page)
            out.append(pltpu.make_async_copy(ckv_hbm.at[layer, pid], cbuf.at[slot, rows, :], csem.at[slot]))
            out.append(pltpu.make_async_copy(krt_hbm.at[layer, pid], krbuf.at[slot, :, rows], ksem.at[slot]))
        return out

    def fetch(seq, grp, slot):
        for cp in copies(seq, grp, slot):
            cp.start()

    def wait(seq, grp, slot):
        for cp in copies(seq, grp, slot):
            cp.wait()

    @pl.when(b == 0)
    def _():
        fetch(0, 0, 0)

    def start_from_new_tokens():
        n_new = cnew_ref.shape[0]
        pad = 2 * V7X_SUBLANES - n_new
        c_new = jnp.concatenate([cnew_ref[...], jnp.zeros((pad, kv_lora), F32)], axis=0).astype(BF16)
        kr_new = jnp.concatenate([krnew_ref[:, :rope], jnp.zeros((pad, rope), F32)], axis=0).astype(BF16)
        rs_new = jnp.concatenate([rsnew_ref[...], jnp.ones((pad, V7X_LANES), F32)], axis=0)
        s = lax.dot_general(c_new, qa, dn_last, preferred_element_type=F32) * rs_new
        s = s + lax.dot_general(kr_new, qr, dn_last, preferred_element_type=F32)
        key = lax.broadcasted_iota(jnp.int32, s.shape, 0)
        qry = lax.broadcasted_iota(jnp.int32, s.shape, 1) // n_heads
        s = jnp.where(key <= qry, s, -jnp.inf)
        m0 = jnp.max(s, axis=0, keepdims=True)
        p = jnp.exp2(s - m0)
        m_scr[...] = m0
        l_scr[...] = jnp.sum(p, axis=0, keepdims=True)
        acc_scr[...] = lax.dot_general(c_new, p.astype(BF16), dn_first, preferred_element_type=F32)

    def stage_a(slot):
        cb = cbuf[slot].astype(BF16)
        cb_scr[slot] = cb
        rs = _key_norm_factor(cb, wk_ref, n_heads, nope)
        s = lax.dot_general(cb, qa, dn_last, preferred_element_type=F32) * rs
        krb = krbuf[slot].astype(BF16)
        s_scr[slot] = s + lax.dot_general(krb, qr, (((0,), (1,)), ((), ())), preferred_element_type=F32)

    def stage_b(slot):
        s = s_scr[slot]
        m_old = m_scr[...]
        m_new = jnp.maximum(m_old, jnp.max(s, axis=0, keepdims=True))
        p = jnp.exp2(s - m_new)
        alpha = jnp.exp2(m_old - m_new)
        l_scr[...] = alpha * l_scr[...] + jnp.sum(p, axis=0, keepdims=True)
        pv = lax.dot_general(cb_scr[slot], p.astype(BF16), dn_first, preferred_element_type=F32)
        acc_scr[...] = alpha * acc_scr[...] + pv
        m_scr[...] = m_new

    for i in range(n_groups + 1):
        slot = i % 2
        if i + 1 < n_groups:
            fetch(b, i + 1, 1 - slot)
        elif i + 1 == n_groups:
            @pl.when(b + 1 < pl.num_programs(0))
            def _():
                fetch(b + 1, 0, 0)
        if i < n_groups:
            wait(b, i, slot)
            if i == 0:
                start_from_new_tokens()
            stage_a(slot)
        if i >= 1:
            stage_b(1 - slot)
    ctx = acc_scr[...] / l_scr[...]
    o_ref[...] = jnp.dot(ctx, perm_ref[...], preferred_element_type=F32).T


def _decode_attn(page_table, cache_ckv, cache_krope, layer, wk_perm, qa, qr, c_new, kr_new, rs_new, *, pg, dims):
    n_heads, _, kv_lora, nope, rope, _, _ = dims
    b, n_pages = page_table.shape
    page = cache_ckv.shape[2]
    n_new = c_new.shape[1]
    rows = n_new * n_heads
    n_groups = n_pages // pg
    keys = pg * page
    assert rows == V7X_LANES and n_pages % pg == 0 and n_groups % 2 == 0 and n_new <= 2 * V7X_SUBLANES
    krope_t = jnp.swapaxes(cache_krope, 2, 3)
    lane = jnp.arange(rows)
    perm = (lane[:, None] == (lane[None, :] % n_new) * n_heads + lane[None, :] // n_new).astype(F32)
    per_seq = lambda r, c: pl.BlockSpec((None, r, c), lambda i, pt: (i, 0, 0))
    hbm = pl.BlockSpec(memory_space=pl.ANY)
    grid_spec = pltpu.PrefetchScalarGridSpec(
        num_scalar_prefetch=1, grid=(b,),
        in_specs=[hbm, hbm, pl.BlockSpec(wk_perm.shape, lambda i, pt: (0, 0)),
                  pl.BlockSpec(perm.shape, lambda i, pt: (0, 0)), per_seq(rows, kv_lora), per_seq(rows, V7X_LANES),
                  per_seq(n_new, kv_lora), per_seq(n_new, V7X_LANES), per_seq(n_new, V7X_LANES)],
        out_specs=per_seq(rows, kv_lora),
        scratch_shapes=[pltpu.VMEM((2, keys, kv_lora), F32), pltpu.VMEM((2, rope, keys), F32),
                        pltpu.VMEM((2, keys, kv_lora), BF16), pltpu.VMEM((2, keys, V7X_LANES), F32),
                        pltpu.VMEM((1, V7X_LANES), F32), pltpu.VMEM((1, V7X_LANES), F32),
                        pltpu.VMEM((kv_lora, V7X_LANES), F32),
                        pltpu.SemaphoreType.DMA((2,)), pltpu.SemaphoreType.DMA((2,))])
    return pl.pallas_call(
        functools.partial(_decode_attn_kernel, layer=layer, pg=pg, n_groups=n_groups, n_heads=n_heads, nope=nope,
                          rope=rope),
        grid_spec=grid_spec, out_shape=jax.ShapeDtypeStruct((b, rows, kv_lora), F32),
        compiler_params=_params("arbitrary"), name="mla_decode_attn",
    )(page_table, cache_ckv, krope_t, wk_perm, perm, qa, qr, c_new, kr_new, rs_new)


def _unabsorb_kernel(ctx_ref, wv_ref, o_ref, *, n_pairs, n_q):
    bb, _, kv_lora = ctx_ref.shape
    head = lambda h: ctx_ref[:, n_q * h:n_q * (h + 1), :].reshape(bb * n_q, kv_lora)
    for p in range(n_pairs):
        x = jnp.concatenate([head(2 * p), head(2 * p + 1)], axis=1).astype(BF16)
        o_ref[:, V7X_LANES * p:V7X_LANES * (p + 1)] = jnp.dot(x, wv_ref[p], preferred_element_type=F32).astype(BF16)


def _unabsorb(ctx, wv2, *, bb, n_q):
    b, rows, kv_lora = ctx.shape
    n_pairs = wv2.shape[0]
    return pl.pallas_call(
        functools.partial(_unabsorb_kernel, n_pairs=n_pairs, n_q=n_q),
        grid=(b // bb,),
        in_specs=[pl.BlockSpec((bb, rows, kv_lora), lambda i: (i, 0, 0)), _const_spec(wv2)],
        out_specs=pl.BlockSpec((bb * n_q, n_pairs * V7X_LANES), lambda i: (i, 0)),
        out_shape=jax.ShapeDtypeStruct((b * n_q, n_pairs * V7X_LANES), BF16),
        compiler_params=_params("parallel"), name="mla_unabsorb_v",
    )(ctx, wv2)


def _conv_prompt(g, w, b, buf_ref, carry):
    rows = g.shape[0]
    width = w.shape[0]
    buf_ref[0:V7X_SUBLANES, :] = carry
    buf_ref[V7X_SUBLANES:, :] = g
    y = b + g * w[width - 1:width, :]
    for s in range(1, width):
        y = y + buf_ref[V7X_SUBLANES - s:V7X_SUBLANES - s + rows, :] * w[width - 1 - s:width - s, :]
    return y


def _conv_decode(g, w, b, prev, t_idx):
    width = w.shape[0]
    rows = g.shape[0]
    y = b + g * w[width - 1:width, :]
    for s in range(1, width):
        sh = jnp.where(t_idx >= s, pltpu.roll(g, s, axis=0), pltpu.roll(prev, rows - V7X_SUBLANES + s, axis=0))
        y = y + sh * w[width - 1 - s:width - s, :]
    return y


def _ffn_kernel(*refs, decode, fc):
    if decode:
        x_ref, a_ref, wpre_ref, gn_ref, wup_ref, cw_ref, cb_ref, wd_ref, st_ref, y_ref, so_ref = refs
    else:
        (x_ref, a_ref, wpre_ref, gn_ref, wup_ref, cw_ref, cb_ref, wd_ref,
         y_ref, so_ref, carry_scr, buf_scr) = refs
    bb, tt, d = x_ref.shape
    rows = bb * tt
    dff = wd_ref.shape[0]
    a = a_ref[...].reshape(rows, a_ref.shape[2])
    x1 = x_ref[...].reshape(rows, d) + jnp.dot(a, wpre_ref[...], preferred_element_type=F32)
    xn = _rms(x1, gn_ref[...]).astype(BF16)
    if decode:
        t_idx = lax.broadcasted_iota(jnp.int32, (rows, fc), 0) % tt
    else:
        @pl.when(pl.program_id(1) == 0)
        def _():
            carry_scr[...] = jnp.zeros(carry_scr.shape, F32)

    acc = x1
    for c in range(dff // fc):
        cols = slice(c * fc, (c + 1) * fc)
        g = jnp.dot(xn, wup_ref[:, cols], preferred_element_type=F32)
        u = jnp.dot(xn, wup_ref[:, dff + c * fc:dff + (c + 1) * fc], preferred_element_type=F32)
        if decode:
            gc = _conv_decode(g, cw_ref[:, cols], cb_ref[:, cols], st_ref[:, :, cols].reshape(rows, fc), t_idx)
            so_ref[:, :, cols] = g.reshape(bb, tt, fc)
        else:
            gc = _conv_prompt(g, cw_ref[:, cols], cb_ref[:, cols], buf_scr.at[c], carry_scr[:, cols])
            last = g[rows - V7X_SUBLANES:, :]
            carry_scr[:, cols] = last
            so_ref[0, :, cols] = last
        h = (gc * _sigmoid(gc) * u).astype(BF16)
        acc = acc + jnp.dot(h, wd_ref[cols, :], preferred_element_type=F32)
    y_ref[...] = acc.reshape(bb, tt, d)


def _resident_spec(a, layer=None):
    if layer is None:
        nd = a.ndim
        return pl.BlockSpec(a.shape, lambda *_: (0,) * nd, pipeline_mode=pl.Buffered(1))
    nd = a.ndim - 1
    return pl.BlockSpec((None,) + a.shape[1:], lambda *_: (layer,) + (0,) * nd, pipeline_mode=pl.Buffered(1))


def _ffn(x, a, w_pre, gn, wup, cw, cb, wd, state, *, layer, bb, tt, fc):
    b, t, d = x.shape
    da = a.shape[2]
    dff = wd.shape[1]
    rows = bb * tt
    decode = state is not None
    tok = lambda c: pl.BlockSpec((bb, tt, c), lambda i, j: (i, j, 0))
    weights = [w_pre, gn, wup, cw, cb, wd]
    in_specs = [tok(d), tok(da), _resident_spec(w_pre)] + [_resident_spec(w, layer) for w in weights[1:]]
    args = [x, a] + weights
    scratch = []
    if decode:
        in_specs.append(pl.BlockSpec((bb, tt, dff), lambda i, j: (i, 0, 0)))
        args.append(state)
    else:
        assert bb == 1
        scratch = [pltpu.VMEM((V7X_SUBLANES, dff), F32), pltpu.VMEM((dff // fc, V7X_SUBLANES + rows, fc), F32)]
    so_spec = pl.BlockSpec((bb, None, V7X_SUBLANES, dff), lambda i, j: (i, j, 0, 0))
    y, so = pl.pallas_call(
        functools.partial(_ffn_kernel, decode=decode, fc=fc),
        grid=(b // bb, t // tt), in_specs=in_specs,
        out_specs=[tok(d), so_spec],
        out_shape=[jax.ShapeDtypeStruct((b, t, d), F32),
                   jax.ShapeDtypeStruct((b, t // tt, V7X_SUBLANES, dff), F32)],
        scratch_shapes=scratch,
        compiler_params=_params("parallel", "arbitrary"), name="conv_ffn",
    )(*args)
    return y, so[:, -1]


def _rec_kernel(*refs, decode):
    if decode:
        (x_ref, gn_ref, win_ref, cw_ref, cb_ref, wgate_ref, ba_ref, bi_ref, lam_ref, st_ref, h0_ref,
         a_ref, co_ref, ho_ref, a_scr, b_scr) = refs
    else:
        (x_ref, gn_ref, win_ref, cw_ref, cb_ref, wgate_ref, ba_ref, bi_ref, lam_ref,
         a_ref, co_ref, ho_ref, a_scr, b_scr, h_scr, carry_scr, hc_scr, buf_scr) = refs
    bb, tt, d = x_ref.shape
    rows = bb * tt
    w = cw_ref.shape[1]
    t = pl.program_id(1)
    xn = _rms(x_ref[...].reshape(rows, d), gn_ref[...]).astype(BF16)
    z = jnp.dot(xn, win_ref[...], preferred_element_type=F32)
    xr = z[:, :w]
    gin = z[:, w:]
    gate = 0.5 * gin * (1.0 + jnp.tanh(0.7978845608028654 * (gin + 0.044715 * (gin * gin * gin))))
    t_idx = lax.broadcasted_iota(jnp.int32, (rows, w), 0) % V7X_SUBLANES
    if decode:
        xc = _conv_decode(xr, cw_ref[...], cb_ref[...], st_ref[...].reshape(rows, w), t_idx)
        co_ref[...] = xr.reshape(bb, tt, w)
    else:
        @pl.when(t == 0)
        def _():
            carry_scr[...] = jnp.zeros(carry_scr.shape, F32)
            hc_scr[...] = jnp.zeros(hc_scr.shape, F32)

        xc = _conv_prompt(xr, cw_ref[...], cb_ref[...], buf_scr, carry_scr[...])
        last = xr[rows - V7X_SUBLANES:, :]
        carry_scr[...] = last
        co_ref[0] = last

    n_blk = wgate_ref.shape[0]
    kb = wgate_ref.shape[1]
    xcb = xc.astype(BF16)
    ra, ri = [], []
    for p in range(n_blk):
        zz = jnp.dot(xcb[:, kb * p:kb * (p + 1)], wgate_ref[p], preferred_element_type=F32)
        ra.append(zz[:, :kb])
        ri.append(zz[:, kb:])
    r = _sigmoid(jnp.concatenate(ra, axis=1) + ba_ref[...])
    gi = _sigmoid(jnp.concatenate(ri, axis=1) + bi_ref[...])
    neg_lam = -lam_ref[...]
    softplus = jnp.maximum(neg_lam, 0.0) + jnp.log(1.0 + jnp.exp(-jnp.abs(neg_lam)))
    av = jnp.exp(r * (-LRU_C * softplus))
    bv = jnp.sqrt(1.0 - av * av) * (gi * xc)

    grp3 = (rows // V7X_SUBLANES, V7X_SUBLANES, w)
    av, bv = av.reshape(grp3), bv.reshape(grp3)
    t3 = lax.broadcasted_iota(jnp.int32, grp3, 1)
    for s in (1, 2, 4):
        keep = t3 >= s
        bv = jnp.where(keep, av * pltpu.roll(bv, s, axis=1) + bv, bv)
        av = jnp.where(keep, av * pltpu.roll(av, s, axis=1), av)
    av, bv = av.reshape(rows, w), bv.reshape(rows, w)
    if decode:
        hs = av * h0_ref[...].reshape(rows, w) + bv
        ho_ref[...] = hs.reshape(bb, tt, w)
    else:
        a_scr[...] = av
        b_scr[...] = bv

        def group(i, h_in):
            sl = pl.ds(pl.multiple_of(i * V7X_SUBLANES, V7X_SUBLANES), V7X_SUBLANES)
            hg = a_scr[sl, :] * h_in + b_scr[sl, :]
            h_scr[sl, :] = hg
            return jnp.broadcast_to(hg[V7X_SUBLANES - 1:, :], (V7X_SUBLANES, w))

        h_last = lax.fori_loop(0, rows // V7X_SUBLANES, group, hc_scr[...])
        hc_scr[...] = h_last
        ho_ref[0] = h_last
        hs = h_scr[...]
    a_ref[...] = (hs * gate).astype(BF16).reshape(bb, tt, w)


def _rec(x, gn, w_in, cw, cb, w_gate, b_a, b_i, lam, state, h0, *, bb, tt):
    b, t, d = x.shape
    w = cw.shape[1]
    rows = bb * tt
    decode = state is not None
    tok = lambda c: pl.BlockSpec((bb, tt, c), lambda i, j: (i, j, 0))
    consts = (gn, w_in, cw, cb, w_gate, b_a, b_i, lam)
    in_specs = [tok(d)] + [_const_spec(a) for a in consts]
    args = [x, *consts]
    scratch = [pltpu.VMEM((rows, w), F32), pltpu.VMEM((rows, w), F32)]
    grp = pl.BlockSpec((bb, V7X_SUBLANES, w), lambda i, j: (i, 0, 0))
    if decode:
        in_specs += [grp, grp]
        args += [state, h0]
    else:
        assert bb == 1
        scratch += [pltpu.VMEM((rows, w), F32), pltpu.VMEM((V7X_SUBLANES, w), F32), pltpu.VMEM((V7X_SUBLANES, w), F32),
                    pltpu.VMEM((V7X_SUBLANES + rows, w), F32)]
    return pl.pallas_call(
        functools.partial(_rec_kernel, decode=decode),
        grid=(b // bb, t // tt), in_specs=in_specs,
        out_specs=[tok(w), grp, grp],
        out_shape=[jax.ShapeDtypeStruct((b, t, w), BF16), jax.ShapeDtypeStruct((b, V7X_SUBLANES, w), F32),
                   jax.ShapeDtypeStruct((b, V7X_SUBLANES, w), F32)],
        scratch_shapes=scratch,
        compiler_params=_params("parallel", "arbitrary"), name="rglru_mixer",
    )(*args)


def _pad_state(buf):
    return jnp.pad(buf, ((0, 0), (V7X_SUBLANES - buf.shape[1], 0), (0, 0)))


class _Tiles(NamedTuple):
    prompt_rows: int
    decode_seqs: int
    attn_q_rows: int
    attn_heads: int
    ffn_chunk: int
    pages_per_group: int


def _tiles(tp, bs, ts, n_heads, vh, dff, n_pages):
    heads = 4 if n_heads % 4 == 0 and (4 * vh) % V7X_LANES == 0 else 2
    return _Tiles(prompt_rows=min(tp, 512), decode_seqs=min(bs, 256 // ts), attn_q_rows=min(tp, 256),
                  attn_heads=heads, ffn_chunk=min(dff, 1024), pages_per_group=min(n_pages, 32))


def kernel(x_prompt, x_sample, cache_ckv, cache_krope, page_table, state_lru_h, state_lru_conv, state_ffn_conv,
           norm_mix, norm_ffn, attn_w_down, attn_g_q_lora, attn_w_uq, attn_g_kv_lora, attn_g_qn, attn_g_qr,
           attn_g_kn, attn_g_kr, attn_w_uk, attn_w_uv, attn_w_o, rec_w_in, rec_conv_w, rec_conv_b, rec_w_a,
           rec_b_a, rec_w_i, rec_b_i, rec_lambda, rec_w_out, ffn_w_up, ffn_conv_w, ffn_conv_b, ffn_w_down):
    bp, tp, d = x_prompt.shape
    bs, ts, _ = x_sample.shape
    depth = norm_mix.shape[0]
    n_mixers = 2
    kv_lora, n_heads, nope = attn_w_uk.shape[1:]
    vh = attn_w_uv.shape[3]
    rope = attn_g_qr.shape[1]
    q_lora = attn_g_q_lora.shape[1]
    half = rope // 2
    page = cache_ckv.shape[2]
    past = page_table.shape[1] * page
    dff = ffn_conv_w.shape[2]
    lru_w = rec_conv_w.shape[2]
    assert ts == V7X_SUBLANES and rope + nope + half <= V7X_LANES and vh < V7X_LANES and n_heads % 2 == 0
    scale = float((nope + rope) ** -0.5) * math.log2(math.e)
    dims = (n_heads, q_lora, kv_lora, nope, rope, vh, scale)
    pad_l = V7X_LANES - rope - nope
    row = lambda v: v.reshape(1, -1).astype(F32)

    def rope_tables(pos):
        inv = ROPE_THETA ** (-jnp.arange(0, rope, 2, dtype=F32) / rope)
        ang = pos.astype(F32)[:, None] * inv[None, :]
        cos, sin = jnp.cos(ang), jnp.sin(ang)
        n = pos.shape[0]
        ones = jnp.ones((n, V7X_LANES - rope - half), F32)
        return (jnp.concatenate([cos, cos, ones, jnp.zeros((n, half), F32)], axis=1),
                jnp.concatenate([-sin, sin, jnp.zeros((n, V7X_LANES - rope), F32)], axis=1))

    tabs_p = rope_tables(jnp.arange(tp))
    tabs_s = rope_tables(past + jnp.arange(ts))

    xp, xs = x_prompt, x_sample
    a_p = a_s = w_pre = None
    ckv_p, kr_p, ckv_s, kr_s = [], [], [], []
    lh_p, lh_s, lc_p, lc_s = [], [], [], []
    fc_p, fc_s = [], []
    tiles = _tiles(tp, bs, ts, n_heads, vh, dff, page_table.shape[1])
    ffn_stacked = (norm_ffn[:, None, :], ffn_w_up.astype(BF16), ffn_conv_w, ffn_conv_b[:, None, :],
                   ffn_w_down.astype(BF16))
    for i in range(depth):
        j = i // n_mixers
        gmix = row(norm_mix[i])
        if i % n_mixers == 0:
            wdn = attn_w_down[j]
            wd = jnp.concatenate([wdn, jnp.zeros((d, V7X_LANES - rope - half), F32), wdn[:, -half:]],
                                 axis=1).astype(BF16)
            gkr = jnp.concatenate([attn_g_kr[j], jnp.zeros((V7X_LANES - rope - half,), F32),
                                   attn_g_kr[j][half:]]).reshape(1, -1)
            wq = attn_w_uq[j].reshape(q_lora, n_heads, nope + rope)
            wuq = jnp.concatenate([wq[:, :, nope:], wq[:, :, :nope], jnp.zeros((q_lora, n_heads, pad_l - half), F32),
                                   wq[:, :, nope + half:]],
                                  axis=2).reshape(q_lora, n_heads * V7X_LANES).astype(BF16)
            gq = jnp.concatenate([attn_g_qr[j], attn_g_qn[j], jnp.zeros((pad_l - half,), F32),
                                  attn_g_qr[j][half:]]).reshape(1, -1)
            wk = attn_w_uk[j]
            wuk = jnp.concatenate([jnp.zeros((kv_lora, n_heads, rope), F32), wk,
                                   jnp.zeros((kv_lora, n_heads, pad_l), F32)],
                                  axis=2).reshape(kv_lora, n_heads * V7X_LANES).astype(BF16)
            gk = jnp.concatenate([jnp.zeros((rope,), F32), attn_g_kn[j], jnp.zeros((pad_l,), F32)]).reshape(1, -1)
            wuv = jnp.concatenate([attn_w_uv[j], jnp.zeros((kv_lora, n_heads, V7X_LANES - vh), F32)],
                                  axis=2).reshape(kv_lora, n_heads * V7X_LANES).astype(BF16)
            vone = jnp.tile((jnp.arange(V7X_LANES) == vh).astype(F32), n_heads).reshape(1, -1)
            proj_w = (gmix, wd, row(attn_g_q_lora[j]), row(attn_g_kv_lora[j]), gkr, wuq, gq, wuk, gk, wuv, vone)

            q_p, c_p, krb_p, k_p, v_p = _mla_proj(xp, *proj_w, *tabs_p, bb=1, tt=tiles.prompt_rows, dims=dims,
                                                  with_kv=True)
            wk_perm = jnp.transpose(wk, (0, 2, 1)).reshape(kv_lora, nope * n_heads).astype(BF16)
            proj_s = proj_w[:7] + (wk_perm,) + proj_w[8:]
            q_s, c_s, krb_s, rs_s = _mla_proj(xs, *proj_s, *tabs_s, bb=tiles.decode_seqs, tt=ts, dims=dims,
                                              with_kv=False)
            amax = lambda g: jnp.max(jnp.abs(g))
            score_bound = 1.02 * scale * (nope * amax(attn_g_qn[j]) * amax(attn_g_kn[j])
                                          + rope * amax(attn_g_qr[j]) * amax(attn_g_kr[j]))
            bounded = (score_bound <= SCORE_EXP2_LIMIT).astype(jnp.int32).reshape(1)
            a_p = _flash(bounded, q_p, k_p, v_p, tq=tiles.attn_q_rows, n_heads=n_heads, vh=vh,
                         hps=tiles.attn_heads)

            wa = jnp.transpose(wk * attn_g_kn[j][None, None, :], (1, 2, 0))
            wa = jnp.concatenate([jnp.zeros((n_heads, rope, kv_lora), F32), wa,
                                  jnp.zeros((n_heads, pad_l, kv_lora), F32)], axis=1).astype(BF16)
            qa = _absorb(q_s.reshape(bs * ts, n_heads * V7X_LANES), wa, tr=tiles.decode_seqs * ts)
            qa = qa.reshape(bs, ts * n_heads, kv_lora)
            qr = q_s.reshape(bs, ts * n_heads, V7X_LANES)
            ctx = _decode_attn(page_table, cache_ckv, cache_krope, j, wk_perm, qa, qr, c_s, krb_s, rs_s,
                               pg=tiles.pages_per_group, dims=dims)
            wv = attn_w_uv[j]
            zer = jnp.zeros((kv_lora, n_heads // 2, vh), F32)
            wv2 = jnp.concatenate([jnp.concatenate([wv[:, 0::2], zer], axis=2),
                                   jnp.concatenate([zer, wv[:, 1::2]], axis=2)], axis=0)
            wv2 = jnp.transpose(wv2, (1, 0, 2)).astype(BF16)
            a_s = _unabsorb(ctx, wv2, bb=tiles.decode_seqs, n_q=ts)
            a_s = a_s.reshape(bs, ts, n_heads * vh)
            w_pre = attn_w_o[j].astype(BF16)
            ckv_p.append(c_p)
            kr_p.append(krb_p[:, :, :rope])
            ckv_s.append(c_s)
            kr_s.append(krb_s[:, :, :rope])
        else:
            n_blk, blk = rec_w_a.shape[1:3]
            assert n_blk % 2 == 0

            def pair(wm):
                z = jnp.zeros((n_blk // 2, blk, blk), F32)
                return jnp.concatenate([jnp.concatenate([wm[0::2], z], axis=2),
                                        jnp.concatenate([z, wm[1::2]], axis=2)], axis=1)

            w_gate = jnp.concatenate([pair(rec_w_a[j]), pair(rec_w_i[j])], axis=2).astype(BF16)
            rec_w = (gmix, rec_w_in[j].astype(BF16), rec_conv_w[j], row(rec_conv_b[j]), w_gate, row(rec_b_a[j]),
                     row(rec_b_i[j]), row(rec_lambda[j]))
            a_p, cv_p, h_p = _rec(xp, *rec_w, None, None, bb=1, tt=tiles.prompt_rows)
            h0 = jnp.broadcast_to(state_lru_h[j][:, None, :], (bs, ts, lru_w))
            a_s, cv_s, h_s = _rec(xs, *rec_w, _pad_state(state_lru_conv[j]), h0, bb=tiles.decode_seqs, tt=ts)
            w_pre = rec_w_out[j].astype(BF16)
            kw = rec_conv_w.shape[1] - 1
            lh_p.append(h_p[:, V7X_SUBLANES - 1])
            lh_s.append(h_s[:, V7X_SUBLANES - 1])
            lc_p.append(cv_p[:, V7X_SUBLANES - kw:])
            lc_s.append(cv_s[:, V7X_SUBLANES - kw:])
        ffn_w = (w_pre,) + ffn_stacked
        xp, fb_p = _ffn(xp, a_p, *ffn_w, None, layer=i, bb=1, tt=tiles.prompt_rows, fc=tiles.ffn_chunk)
        xs, fb_s = _ffn(xs, a_s, *ffn_w, _pad_state(state_ffn_conv[i]), layer=i, bb=tiles.decode_seqs, tt=ts,
                        fc=tiles.ffn_chunk)
        kf = ffn_conv_w.shape[1] - 1
        fc_p.append(fb_p[:, V7X_SUBLANES - kf:])
        fc_s.append(fb_s[:, V7X_SUBLANES - kf:])
    return (xp, xs, jnp.stack(ckv_p), jnp.stack(kr_p), jnp.stack(ckv_s), jnp.stack(kr_s),
            jnp.stack(lh_p), jnp.stack(lh_s), jnp.stack(lc_p), jnp.stack(lc_s), jnp.stack(fc_p), jnp.stack(fc_s))
```

```python
import functools
import math
from typing import NamedTuple

import jax
import jax.numpy as jnp
from jax import lax
from jax.experimental import pallas as pl
from jax.experimental.pallas import tpu as pltpu

EPS = 1e-6
ROPE_THETA = 10000.0
LRU_C = 8.0
SCORE_EXP2_LIMIT = 64.0

V7X_LANES = 128
V7X_SUBLANES = 8
V7X_VMEM_BYTES = 64 * 1024 * 1024
VMEM_LIMIT = V7X_VMEM_BYTES * 7 // 8

F32 = jnp.float32
BF16 = jnp.bfloat16


def _params(*sem):
    return pltpu.CompilerParams(dimension_semantics=sem, vmem_limit_bytes=VMEM_LIMIT)


def _const_spec(a):
    nd = a.ndim
    return pl.BlockSpec(a.shape, lambda *_: (0,) * nd)


def _rms(x, g):
    ms = jnp.mean(x * x, axis=-1, keepdims=True)
    return x * lax.rsqrt(ms + EPS) * g


def _sigmoid(x):
    return 1.0 / (1.0 + jnp.exp(-x))


def _rope(y, cos, sin, half):
    return y * cos + pltpu.roll(y, half, axis=1) * sin


def _tile_rows(tab_ref, bb):
    t = tab_ref[...]
    if bb == 1:
        return t
    return jnp.broadcast_to(t[None], (bb,) + t.shape).reshape(bb * t.shape[0], t.shape[1])


def _key_norm_factor(cb, wk_ref, n_heads, nope):
    k = jnp.dot(cb, wk_ref[...], preferred_element_type=F32)
    ksq = k * k
    part = ksq[:, :V7X_LANES]
    for i in range(1, wk_ref.shape[1] // V7X_LANES):
        part = part + ksq[:, V7X_LANES * i:V7X_LANES * (i + 1)]
    shift = V7X_LANES // 2
    while shift >= n_heads:
        part = part + pltpu.roll(part, shift, axis=1)
        shift //= 2
    return lax.rsqrt(part / nope + EPS)


def _mla_proj_kernel(x_ref, gmix_ref, wd_ref, gql_ref, gkv_ref, gkr_ref, wuq_ref, gq_ref, wuk_ref, gk_ref, wuv_ref,
                     vone_ref, cos_ref, sin_ref, *out_refs, n_heads, q_lora, kv_lora, nope, rope, scale, with_kv):
    if with_kv:
        q_ref, ckv_ref, kr_ref, k_ref, v_ref = out_refs
    else:
        q_ref, ckv_ref, kr_ref, rs_ref = out_refs
    bb, tt, d = x_ref.shape
    rows = bb * tt
    half = rope // 2
    xn = _rms(x_ref[...].reshape(rows, d), gmix_ref[...])
    dd = jnp.dot(xn.astype(BF16), wd_ref[...], preferred_element_type=F32)
    cq = _rms(dd[:, :q_lora], gql_ref[...])
    ckv = _rms(dd[:, q_lora:q_lora + kv_lora], gkv_ref[...])
    kr = dd[:, q_lora + kv_lora:]
    cos = _tile_rows(cos_ref, bb)
    sin = _tile_rows(sin_ref, bb)
    lane = lax.broadcasted_iota(jnp.int32, (1, V7X_LANES), 1)
    is_rope = lane < rope
    is_nope = jnp.logical_and(lane >= rope, lane < rope + nope)

    kr_ss = jnp.sum(jnp.where(is_rope, kr * kr, 0.0), axis=-1, keepdims=True)
    krr = _rope(kr * lax.rsqrt(kr_ss / rope + EPS) * gkr_ref[...], cos, sin, half)
    kr_ref[...] = krr.reshape(bb, tt, V7X_LANES)
    ckv_ref[...] = ckv.reshape(bb, tt, kv_lora)

    q = jnp.dot(cq.astype(BF16), wuq_ref[...], preferred_element_type=F32)
    ckv_b = ckv.astype(BF16)
    if with_kv:
        kk = jnp.dot(ckv_b, wuk_ref[...], preferred_element_type=F32)
        vv = jnp.dot(ckv_b, wuv_ref[...], preferred_element_type=F32) + vone_ref[...]
        v_ref[...] = vv.astype(BF16).reshape(v_ref.shape)
    else:
        rs_ref[...] = _key_norm_factor(ckv_b, wuk_ref, n_heads, nope).reshape(bb, tt, V7X_LANES)
    for h in range(n_heads):
        sl = slice(V7X_LANES * h, V7X_LANES * (h + 1))
        qb = q[:, sl]
        sq = qb * qb
        ss_r = jnp.sum(jnp.where(is_rope, sq, 0.0), axis=-1, keepdims=True)
        ss_n = jnp.sum(jnp.where(is_nope, sq, 0.0), axis=-1, keepdims=True)
        rs = jnp.where(is_nope, lax.rsqrt(ss_n / nope + EPS), lax.rsqrt(ss_r / rope + EPS))
        y = _rope(qb * rs * gq_ref[...], cos, sin, half)
        q_ref[:, :, sl] = (y * scale).astype(BF16).reshape(bb, tt, V7X_LANES)
        if with_kv:
            kb = kk[:, sl]
            ss_k = jnp.sum(kb * kb, axis=-1, keepdims=True)
            kn = kb * lax.rsqrt(ss_k / nope + EPS) * gk_ref[...]
            k_ref[:, :, sl] = (kn + krr).astype(BF16).reshape(bb, tt, V7X_LANES)


def _mla_proj(x, gmix, wd, gql, gkv, gkr, wuq, gq, wuk, gk, wuv, vone, cos, sin, *, bb, tt, dims, with_kv):
    n_heads, q_lora, kv_lora, nope, rope, vh, scale = dims
    b, t, d = x.shape
    hl = n_heads * V7X_LANES
    tok = lambda c: pl.BlockSpec((bb, tt, c), lambda i, j: (i, j, 0))
    tab = pl.BlockSpec((tt, V7X_LANES), lambda i, j: (j, 0))
    out_shape = [jax.ShapeDtypeStruct((b, t, hl), BF16), jax.ShapeDtypeStruct((b, t, kv_lora), F32),
                 jax.ShapeDtypeStruct((b, t, V7X_LANES), F32)]
    out_specs = [tok(hl), tok(kv_lora), tok(V7X_LANES)]
    if with_kv:
        out_shape += [jax.ShapeDtypeStruct((b, t, hl), BF16), jax.ShapeDtypeStruct((b, t, hl), BF16)]
        out_specs += [tok(hl), tok(hl)]
    else:
        out_shape.append(jax.ShapeDtypeStruct((b, t, V7X_LANES), F32))
        out_specs.append(tok(V7X_LANES))
    consts = (gmix, wd, gql, gkv, gkr, wuq, gq, wuk, gk, wuv, vone)
    return pl.pallas_call(
        functools.partial(_mla_proj_kernel, n_heads=n_heads, q_lora=q_lora, kv_lora=kv_lora, nope=nope, rope=rope,
                          scale=scale, with_kv=with_kv),
        grid=(b // bb, t // tt),
        in_specs=[tok(d)] + [_const_spec(a) for a in consts] + [tab, tab],
        out_specs=out_specs, out_shape=out_shape,
        compiler_params=_params("parallel", "parallel"), name="mla_proj",
    )(x, *consts, cos, sin)


def _attn_block(q, k_ref, v_ref, lanes, vh, past, tq, bounded):
    dn = (((1,), (1,)), ((), ()))
    row = lax.broadcasted_iota(jnp.int32, (tq, tq), 0)
    col = lax.broadcasted_iota(jnp.int32, (tq, tq), 1)
    s_d = lax.dot_general(q, k_ref[0, past:past + tq, lanes], dn, preferred_element_type=F32)
    s_d = jnp.where(col <= row, s_d, -jnp.inf)
    if past:
        s_p = lax.dot_general(q, k_ref[0, :past, lanes], dn, preferred_element_type=F32)
    if not bounded:
        m = jnp.max(s_d, axis=1, keepdims=True)
        if past:
            m = jnp.maximum(m, jnp.max(s_p, axis=1, keepdims=True))
            s_p = s_p - m
        s_d = s_d - m
    acc = jnp.dot(jnp.exp2(s_d).astype(BF16), v_ref[0, past:past + tq, lanes], preferred_element_type=F32)
    if past:
        acc = acc + jnp.dot(jnp.exp2(s_p).astype(BF16), v_ref[0, :past, lanes], preferred_element_type=F32)
    return acc[:, :vh] / acc[:, vh:vh + 1]


def _flash_kernel(bounded_ref, q_ref, k_ref, v_ref, o_ref, *, tq, vh, hps):
    is_bounded = bounded_ref[0] != 0
    for bounded in (True, False):
        @pl.when(is_bounded == bounded)
        def _(bounded=bounded):
            for n in range(q_ref.shape[1] // tq):
                rows = slice(n * tq, (n + 1) * tq)
                outs = []
                for hh in range(hps):
                    lanes = slice(V7X_LANES * hh, V7X_LANES * (hh + 1))
                    outs.append(_attn_block(q_ref[0, rows, lanes], k_ref, v_ref, lanes, vh, n * tq, tq, bounded))
                o_ref[0, rows, :] = jnp.concatenate(outs, axis=1).astype(BF16)


def _flash(bounded, q, k, v, *, tq, n_heads, vh, hps):
    b, t, _ = q.shape
    seq = lambda c: pl.BlockSpec((1, t, hps * c), lambda i, h, f: (i, 0, h))
    grid_spec = pltpu.PrefetchScalarGridSpec(
        num_scalar_prefetch=1, grid=(b, n_heads // hps),
        in_specs=[seq(V7X_LANES), seq(V7X_LANES), seq(V7X_LANES)], out_specs=seq(vh))
    return pl.pallas_call(
        functools.partial(_flash_kernel, tq=tq, vh=vh, hps=hps),
        grid_spec=grid_spec, out_shape=jax.ShapeDtypeStruct((b, t, n_heads * vh), BF16),
        compiler_params=_params("parallel", "parallel"), name="mla_prompt_attn",
    )(bounded, q, k, v)


def _absorb_kernel(q_ref, wa_ref, qa_ref, *, n_heads, kv_lora):
    for h in range(n_heads):
        qh = q_ref[:, V7X_LANES * h:V7X_LANES * (h + 1)]
        qa_ref[:, kv_lora * h:kv_lora * (h + 1)] = jnp.dot(qh, wa_ref[h], preferred_element_type=F32).astype(BF16)


def _absorb(q2d, wa, *, tr):
    r = q2d.shape[0]
    n_heads, _, kv_lora = wa.shape
    return pl.pallas_call(
        functools.partial(_absorb_kernel, n_heads=n_heads, kv_lora=kv_lora),
        grid=(r // tr,),
        in_specs=[pl.BlockSpec((tr, n_heads * V7X_LANES), lambda i: (i, 0)), _const_spec(wa)],
        out_specs=pl.BlockSpec((tr, n_heads * kv_lora), lambda i: (i, 0)),
        out_shape=jax.ShapeDtypeStruct((r, n_heads * kv_lora), BF16),
        compiler_params=_params("parallel"), name="mla_absorb_q",
    )(q2d, wa)


def _decode_attn_kernel(pt_ref, ckv_hbm, krt_hbm, wk_ref, perm_ref, qa_ref, qr_ref, cnew_ref, krnew_ref, rsnew_ref,
                        o_ref, cbuf, krbuf, cb_scr, s_scr, m_scr, l_scr, acc_scr, csem, ksem,
                        *, layer, pg, n_groups, n_heads, nope, rope):
    b = pl.program_id(0)
    page = ckv_hbm.shape[2]
    kv_lora = ckv_hbm.shape[3]
    dn_last = (((1,), (1,)), ((), ()))
    dn_first = (((0,), (0,)), ((), ()))
    qa = qa_ref[...]
    qr = qr_ref[:, :rope]

    def copies(seq, grp, slot):
        out = []
        for s in range(pg):
            pid = pt_ref[seq, grp * pg + s]
            rows = pl.ds(s * page, page)
            out.append(pltpu.make_async_copy(ckv_hbm.at[layer, pid], cbuf.at[slot, rows, :], csem.at[slot]))
            out.append(pltpu.make_async_copy(krt_hbm.at[layer, pid], krbuf.at[slot, :, rows], ksem.at[slot]))
        return out

    def fetch(seq, grp, slot):
        for cp in copies(seq, grp, slot):
            cp.start()

    def wait(seq, grp, slot):
        for cp in copies(seq, grp, slot):
            cp.wait()

    @pl.when(b == 0)
    def _():
        fetch(0, 0, 0)

    def start_from_new_tokens():
        n_new = cnew_ref.shape[0]
        pad = 2 * V7X_SUBLANES - n_new
        c_new = jnp.concatenate([cnew_ref[...], jnp.zeros((pad, kv_lora), F32)], axis=0).astype(BF16)
        kr_new = jnp.concatenate([krnew_ref[:, :rope], jnp.zeros((pad, rope), F32)], axis=0).astype(BF16)
        rs_new = jnp.concatenate([rsnew_ref[...], jnp.ones((pad, V7X_LANES), F32)], axis=0)
        s = lax.dot_general(c_new, qa, dn_last, preferred_element_type=F32) * rs_new
        s = s + lax.dot_general(kr_new, qr, dn_last, preferred_element_type=F32)
        key = lax.broadcasted_iota(jnp.int32, s.shape, 0)
        qry = lax.broadcasted_iota(jnp.int32, s.shape, 1) // n_heads
        s = jnp.where(key <= qry, s, -jnp.inf)
        m0 = jnp.max(s, axis=0, keepdims=True)
        p = jnp.exp2(s - m0)
        m_scr[...] = m0
        l_scr[...] = jnp.sum(p, axis=0, keepdims=True)
        acc_scr[...] = lax.dot_general(c_new, p.astype(BF16), dn_first, preferred_element_type=F32)

    def stage_a(slot):
        cb = cbuf[slot].astype(BF16)
        cb_scr[slot] = cb
        rs = _key_norm_factor(cb, wk_ref, n_heads, nope)
        s = lax.dot_general(cb, qa, dn_last, preferred_element_type=F32) * rs
        krb = krbuf[slot].astype(BF16)
        s_scr[slot] = s + lax.dot_general(krb, qr, (((0,), (1,)), ((), ())), preferred_element_type=F32)

    def stage_b(slot):
        s = s_scr[slot]
        m_old = m_scr[...]
        m_new = jnp.maximum(m_old, jnp.max(s, axis=0, keepdims=True))
        p = jnp.exp2(s - m_new)
        alpha = jnp.exp2(m_old - m_new)
        l_scr[...] = alpha * l_scr[...] + jnp.sum(p, axis=0, keepdims=True)
        pv = lax.dot_general(cb_scr[slot], p.astype(BF16), dn_first, preferred_element_type=F32)
        acc_scr[...] = alpha * acc_scr[...] + pv
        m_scr[...] = m_new

    for i in range(n_groups + 1):
        slot = i % 2
        if i + 1 < n_groups:
            fetch(b, i + 1, 1 - slot)
        elif i + 1 == n_groups:
            @pl.when(b + 1 < pl.num_programs(0))
            def _():
                fetch(b + 1, 0, 0)
        if i < n_groups:
            wait(b, i, slot)
            if i == 0:
                start_from_new_tokens()
            stage_a(slot)
        if i >= 1:
            stage_b(1 - slot)
    ctx = acc_scr[...] / l_scr[...]
    o_ref[...] = jnp.dot(ctx, perm_ref[...], preferred_element_type=F32).T


def _decode_attn(page_table, cache_ckv, cache_krope, layer, wk_perm, qa, qr, c_new, kr_new, rs_new, *, pg, dims):
    n_heads, _, kv_lora, nope, rope, _, _ = dims
    b, n_pages = page_table.shape
    page = cache_ckv.shape[2]
    n_new = c_new.shape[1]
    rows = n_new * n_heads
    n_groups = n_pages // pg
    keys = pg * page
    assert rows == V7X_LANES and n_pages % pg == 0 and n_groups % 2 == 0 and n_new <= 2 * V7X_SUBLANES
    krope_t = jnp.swapaxes(cache_krope, 2, 3)
    lane = jnp.arange(rows)
    perm = (lane[:, None] == (lane[None, :] % n_new) * n_heads + lane[None, :] // n_new).astype(F32)
    per_seq = lambda r, c: pl.BlockSpec((None, r, c), lambda i, pt: (i, 0, 0))
    hbm = pl.BlockSpec(memory_space=pl.ANY)
    grid_spec = pltpu.PrefetchScalarGridSpec(
        num_scalar_prefetch=1, grid=(b,),
        in_specs=[hbm, hbm, pl.BlockSpec(wk_perm.shape, lambda i, pt: (0, 0)),
                  pl.BlockSpec(perm.shape, lambda i, pt: (0, 0)), per_seq(rows, kv_lora), per_seq(rows, V7X_LANES),
                  per_seq(n_new, kv_lora), per_seq(n_new, V7X_LANES), per_seq(n_new, V7X_LANES)],
        out_specs=per_seq(rows, kv_lora),
        scratch_shapes=[pltpu.VMEM((2, keys, kv_lora), F32), pltpu.VMEM((2, rope, keys), F32),
                        pltpu.VMEM((2, keys, kv_lora), BF16), pltpu.VMEM((2, keys, V7X_LANES), F32),
                        pltpu.VMEM((1, V7X_LANES), F32), pltpu.VMEM((1, V7X_LANES), F32),
                        pltpu.VMEM((kv_lora, V7X_LANES), F32),
                        pltpu.SemaphoreType.DMA((2,)), pltpu.SemaphoreType.DMA((2,))])
    return pl.pallas_call(
        functools.partial(_decode_attn_kernel, layer=layer, pg=pg, n_groups=n_groups, n_heads=n_heads, nope=nope,
                          rope=rope),
        grid_spec=grid_spec, out_shape=jax.ShapeDtypeStruct((b, rows, kv_lora), F32),
        compiler_params=_params("arbitrary"), name="mla_decode_attn",
    )(page_table, cache_ckv, krope_t, wk_perm, perm, qa, qr, c_new, kr_new, rs_new)


def _unabsorb_kernel(ctx_ref, wv_ref, o_ref, *, n_pairs, n_q):
    bb, _, kv_lora = ctx_ref.shape
    head = lambda h: ctx_ref[:, n_q * h:n_q * (h + 1), :].reshape(bb * n_q, kv_lora)
    for p in range(n_pairs):
        x = jnp.concatenate([head(2 * p), head(2 * p + 1)], axis=1).astype(BF16)
        o_ref[:, V7X_LANES * p:V7X_LANES * (p + 1)] = jnp.dot(x, wv_ref[p], preferred_element_type=F32).astype(BF16)


def _unabsorb(ctx, wv2, *, bb, n_q):
    b, rows, kv_lora = ctx.shape
    n_pairs = wv2.shape[0]
    return pl.pallas_call(
        functools.partial(_unabsorb_kernel, n_pairs=n_pairs, n_q=n_q),
        grid=(b // bb,),
        in_specs=[pl.BlockSpec((bb, rows, kv_lora), lambda i: (i, 0, 0)), _const_spec(wv2)],
        out_specs=pl.BlockSpec((bb * n_q, n_pairs * V7X_LANES), lambda i: (i, 0)),
        out_shape=jax.ShapeDtypeStruct((b * n_q, n_pairs * V7X_LANES), BF16),
        compiler_params=_params("parallel"), name="mla_unabsorb_v",
    )(ctx, wv2)


def _conv_prompt(g, w, b, buf_ref, carry):
    rows = g.shape[0]
    width = w.shape[0]
    buf_ref[0:V7X_SUBLANES, :] = carry
    buf_ref[V7X_SUBLANES:, :] = g
    y = b + g * w[width - 1:width, :]
    for s in range(1, width):
        y = y + buf_ref[V7X_SUBLANES - s:V7X_SUBLANES - s + rows, :] * w[width - 1 - s:width - s, :]
    return y


def _conv_decode(g, w, b, prev, t_idx):
    width = w.shape[0]
    rows = g.shape[0]
    y = b + g * w[width - 1:width, :]
    for s in range(1, width):
        sh = jnp.where(t_idx >= s, pltpu.roll(g, s, axis=0), pltpu.roll(prev, rows - V7X_SUBLANES + s, axis=0))
        y = y + sh * w[width - 1 - s:width - s, :]
    return y


def _ffn_kernel(*refs, decode, fc):
    if decode:
        x_ref, a_ref, wpre_ref, gn_ref, wup_ref, cw_ref, cb_ref, wd_ref, st_ref, y_ref, so_ref = refs
    else:
        (x_ref, a_ref, wpre_ref, gn_ref, wup_ref, cw_ref, cb_ref, wd_ref,
         y_ref, so_ref, carry_scr, buf_scr) = refs
    bb, tt, d = x_ref.shape
    rows = bb * tt
    dff = wd_ref.shape[0]
    a = a_ref[...].reshape(rows, a_ref.shape[2])
    x1 = x_ref[...].reshape(rows, d) + jnp.dot(a, wpre_ref[...], preferred_element_type=F32)
    xn = _rms(x1, gn_ref[...]).astype(BF16)
    if decode:
        t_idx = lax.broadcasted_iota(jnp.int32, (rows, fc), 0) % tt
    else:
        @pl.when(pl.program_id(1) == 0)
        def _():
            carry_scr[...] = jnp.zeros(carry_scr.shape, F32)

    acc = x1
    for c in range(dff // fc):
        cols = slice(c * fc, (c + 1) * fc)
        g = jnp.dot(xn, wup_ref[:, cols], preferred_element_type=F32)
        u = jnp.dot(xn, wup_ref[:, dff + c * fc:dff + (c + 1) * fc], preferred_element_type=F32)
        if decode:
            gc = _conv_decode(g, cw_ref[:, cols], cb_ref[:, cols], st_ref[:, :, cols].reshape(rows, fc), t_idx)
            so_ref[:, :, cols] = g.reshape(bb, tt, fc)
        else:
            gc = _conv_prompt(g, cw_ref[:, cols], cb_ref[:, cols], buf_scr.at[c], carry_scr[:, cols])
            last = g[rows - V7X_SUBLANES:, :]
            carry_scr[:, cols] = last
            so_ref[0, :, cols] = last
        h = (gc * _sigmoid(gc) * u).astype(BF16)
        acc = acc + jnp.dot(h, wd_ref[cols, :], preferred_element_type=F32)
    y_ref[...] = acc.reshape(bb, tt, d)


def _resident_spec(a, layer=None):
    if layer is None:
        nd = a.ndim
        return pl.BlockSpec(a.shape, lambda *_: (0,) * nd, pipeline_mode=pl.Buffered(1))
    nd = a.ndim - 1
    return pl.BlockSpec((None,) + a.shape[1:], lambda *_: (layer,) + (0,) * nd, pipeline_mode=pl.Buffered(1))


def _ffn(x, a, w_pre, gn, wup, cw, cb, wd, state, *, layer, bb, tt, fc):
    b, t, d = x.shape
    da = a.shape[2]
    dff = wd.shape[1]
    rows = bb * tt
    decode = state is not None
    tok = lambda c: pl.BlockSpec((bb, tt, c), lambda i, j: (i, j, 0))
    weights = [w_pre, gn, wup, cw, cb, wd]
    in_specs = [tok(d), tok(da), _resident_spec(w_pre)] + [_resident_spec(w, layer) for w in weights[1:]]
    args = [x, a] + weights
    scratch = []
    if decode:
        in_specs.append(pl.BlockSpec((bb, tt, dff), lambda i, j: (i, 0, 0)))
        args.append(state)
    else:
        assert bb == 1
        scratch = [pltpu.VMEM((V7X_SUBLANES, dff), F32), pltpu.VMEM((dff // fc, V7X_SUBLANES + rows, fc), F32)]
    so_spec = pl.BlockSpec((bb, None, V7X_SUBLANES, dff), lambda i, j: (i, j, 0, 0))
    y, so = pl.pallas_call(
        functools.partial(_ffn_kernel, decode=decode, fc=fc),
        grid=(b // bb, t // tt), in_specs=in_specs,
        out_specs=[tok(d), so_spec],
        out_shape=[jax.ShapeDtypeStruct((b, t, d), F32),
                   jax.ShapeDtypeStruct((b, t // tt, V7X_SUBLANES, dff), F32)],
        scratch_shapes=scratch,
        compiler_params=_params("parallel", "arbitrary"), name="conv_ffn",
    )(*args)
    return y, so[:, -1]


def _rec_kernel(*refs, decode):
    if decode:
        (x_ref, gn_ref, win_ref, cw_ref, cb_ref, wgate_ref, ba_ref, bi_ref, lam_ref, st_ref, h0_ref,
         a_ref, co_ref, ho_ref, a_scr, b_scr) = refs
    else:
        (x_ref, gn_ref, win_ref, cw_ref, cb_ref, wgate_ref, ba_ref, bi_ref, lam_ref,
         a_ref, co_ref, ho_ref, a_scr, b_scr, h_scr, carry_scr, hc_scr, buf_scr) = refs
    bb, tt, d = x_ref.shape
    rows = bb * tt
    w = cw_ref.shape[1]
    t = pl.program_id(1)
    xn = _rms(x_ref[...].reshape(rows, d), gn_ref[...]).astype(BF16)
    z = jnp.dot(xn, win_ref[...], preferred_element_type=F32)
    xr = z[:, :w]
    gin = z[:, w:]
    gate = 0.5 * gin * (1.0 + jnp.tanh(0.7978845608028654 * (gin + 0.044715 * (gin * gin * gin))))
    t_idx = lax.broadcasted_iota(jnp.int32, (rows, w), 0) % V7X_SUBLANES
    if decode:
        xc = _conv_decode(xr, cw_ref[...], cb_ref[...], st_ref[...].reshape(rows, w), t_idx)
        co_ref[...] = xr.reshape(bb, tt, w)
    else:
        @pl.when(t == 0)
        def _():
            carry_scr[...] = jnp.zeros(carry_scr.shape, F32)
            hc_scr[...] = jnp.zeros(hc_scr.shape, F32)

        xc = _conv_prompt(xr, cw_ref[...], cb_ref[...], buf_scr, carry_scr[...])
        last = xr[rows - V7X_SUBLANES:, :]
        carry_scr[...] = last
        co_ref[0] = last

    n_blk = wgate_ref.shape[0]
    kb = wgate_ref.shape[1]
    xcb = xc.astype(BF16)
    ra, ri = [], []
    for p in range(n_blk):
        zz = jnp.dot(xcb[:, kb * p:kb * (p + 1)], wgate_ref[p], preferred_element_type=F32)
        ra.append(zz[:, :kb])
        ri.append(zz[:, kb:])
    r = _sigmoid(jnp.concatenate(ra, axis=1) + ba_ref[...])
    gi = _sigmoid(jnp.concatenate(ri, axis=1) + bi_ref[...])
    neg_lam = -lam_ref[...]
    softplus = jnp.maximum(neg_lam, 0.0) + jnp.log(1.0 + jnp.exp(-jnp.abs(neg_lam)))
    av = jnp.exp(r * (-LRU_C * softplus))
    bv = jnp.sqrt(1.0 - av * av) * (gi * xc)

    grp3 = (rows // V7X_SUBLANES, V7X_SUBLANES, w)
    av, bv = av.reshape(grp3), bv.reshape(grp3)
    t3 = lax.broadcasted_iota(jnp.int32, grp3, 1)
    for s in (1, 2, 4):
        keep = t3 >= s
        bv = jnp.where(keep, av * pltpu.roll(bv, s, axis=1) + bv, bv)
        av = jnp.where(keep, av * pltpu.roll(av, s, axis=1), av)
    av, bv = av.reshape(rows, w), bv.reshape(rows, w)
    if decode:
        hs = av * h0_ref[...].reshape(rows, w) + bv
        ho_ref[...] = hs.reshape(bb, tt, w)
    else:
        a_scr[...] = av
        b_scr[...] = bv

        def group(i, h_in):
            sl = pl.ds(pl.multiple_of(i * V7X_SUBLANES, V7X_SUBLANES), V7X_SUBLANES)
            hg = a_scr[sl, :] * h_in + b_scr[sl, :]
            h_scr[sl, :] = hg
            return jnp.broadcast_to(hg[V7X_SUBLANES - 1:, :], (V7X_SUBLANES, w))

        h_last = lax.fori_loop(0, rows // V7X_SUBLANES, group, hc_scr[...])
        hc_scr[...] = h_last
        ho_ref[0] = h_last
        hs = h_scr[...]
    a_ref[...] = (hs * gate).astype(BF16).reshape(bb, tt, w)


def _rec(x, gn, w_in, cw, cb, w_gate, b_a, b_i, lam, state, h0, *, bb, tt):
    b, t, d = x.shape
    w = cw.shape[1]
    rows = bb * tt
    decode = state is not None
    tok = lambda c: pl.BlockSpec((bb, tt, c), lambda i, j: (i, j, 0))
    consts = (gn, w_in, cw, cb, w_gate, b_a, b_i, lam)
    in_specs = [tok(d)] + [_const_spec(a) for a in consts]
    args = [x, *consts]
    scratch = [pltpu.VMEM((rows, w), F32), pltpu.VMEM((rows, w), F32)]
    grp = pl.BlockSpec((bb, V7X_SUBLANES, w), lambda i, j: (i, 0, 0))
    if decode:
        in_specs += [grp, grp]
        args += [state, h0]
    else:
        assert bb == 1
        scratch += [pltpu.VMEM((rows, w), F32), pltpu.VMEM((V7X_SUBLANES, w), F32), pltpu.VMEM((V7X_SUBLANES, w), F32),
                    pltpu.VMEM((V7X_SUBLANES + rows, w), F32)]
    return pl.pallas_call(
        functools.partial(_rec_kernel, decode=decode),
        grid=(b // bb, t // tt), in_specs=in_specs,
        out_specs=[tok(w), grp, grp],
        out_shape=[jax.ShapeDtypeStruct((b, t, w), BF16), jax.ShapeDtypeStruct((b, V7X_SUBLANES, w), F32),
                   jax.ShapeDtypeStruct((b, V7X_SUBLANES, w), F32)],
        scratch_shapes=scratch,
        compiler_params=_params("parallel", "arbitrary"), name="rglru_mixer",
    )(*args)


def _pad_state(buf):
    return jnp.pad(buf, ((0, 0), (V7X_SUBLANES - buf.shape[1], 0), (0, 0)))


class _Tiles(NamedTuple):
    prompt_rows: int
    decode_seqs: int
    attn_q_rows: int
    attn_heads: int
    ffn_chunk: int
    pages_per_group: int


def _tiles(tp, bs, ts, n_heads, vh, dff, n_pages):
    heads = 4 if n_heads % 4 == 0 and (4 * vh) % V7X_LANES == 0 else 2
    return _Tiles(prompt_rows=min(tp, 512), decode_seqs=min(bs, 256 // ts), attn_q_rows=min(tp, 256),
                  attn_heads=heads, ffn_chunk=min(dff, 1536), pages_per_group=min(n_pages, 32))


def kernel(x_prompt, x_sample, cache_ckv, cache_krope, page_table, state_lru_h, state_lru_conv, state_ffn_conv,
           norm_mix, norm_ffn, attn_w_down, attn_g_q_lora, attn_w_uq, attn_g_kv_lora, attn_g_qn, attn_g_qr,
           attn_g_kn, attn_g_kr, attn_w_uk, attn_w_uv, attn_w_o, rec_w_in, rec_conv_w, rec_conv_b, rec_w_a,
           rec_b_a, rec_w_i, rec_b_i, rec_lambda, rec_w_out, ffn_w_up, ffn_conv_w, ffn_conv_b, ffn_w_down):
    bp, tp, d = x_prompt.shape
    bs, ts, _ = x_sample.shape
    depth = norm_mix.shape[0]
    n_mixers = 2
    kv_lora, n_heads, nope = attn_w_uk.shape[1:]
    vh = attn_w_uv.shape[3]
    rope = attn_g_qr.shape[1]
    q_lora = attn_g_q_lora.shape[1]
    half = rope // 2
    page = cache_ckv.shape[2]
    past = page_table.shape[1] * page
    dff = ffn_conv_w.shape[2]
    lru_w = rec_conv_w.shape[2]
    assert ts == V7X_SUBLANES and rope + nope + half <= V7X_LANES and vh < V7X_LANES and n_heads % 2 == 0
    scale = float((nope + rope) ** -0.5) * math.log2(math.e)
    dims = (n_heads, q_lora, kv_lora, nope, rope, vh, scale)
    pad_l = V7X_LANES - rope - nope
    row = lambda v: v.reshape(1, -1).astype(F32)

    def rope_tables(pos):
        inv = ROPE_THETA ** (-jnp.arange(0, rope, 2, dtype=F32) / rope)
        ang = pos.astype(F32)[:, None] * inv[None, :]
        cos, sin = jnp.cos(ang), jnp.sin(ang)
        n = pos.shape[0]
        ones = jnp.ones((n, V7X_LANES - rope - half), F32)
        return (jnp.concatenate([cos, cos, ones, jnp.zeros((n, half), F32)], axis=1),
                jnp.concatenate([-sin, sin, jnp.zeros((n, V7X_LANES - rope), F32)], axis=1))

    tabs_p = rope_tables(jnp.arange(tp))
    tabs_s = rope_tables(past + jnp.arange(ts))

    xp, xs = x_prompt, x_sample
    a_p = a_s = w_pre = None
    ckv_p, kr_p, ckv_s, kr_s = [], [], [], []
    lh_p, lh_s, lc_p, lc_s = [], [], [], []
    fc_p, fc_s = [], []
    tiles = _tiles(tp, bs, ts, n_heads, vh, dff, page_table.shape[1])
    ffn_stacked = (norm_ffn[:, None, :], ffn_w_up.astype(BF16), ffn_conv_w, ffn_conv_b[:, None, :],
                   ffn_w_down.astype(BF16))
    for i in range(depth):
        j = i // n_mixers
        gmix = row(norm_mix[i])
        if i % n_mixers == 0:
            wdn = attn_w_down[j]
            wd = jnp.concatenate([wdn, jnp.zeros((d, V7X_LANES - rope - half), F32), wdn[:, -half:]],
                                 axis=1).astype(BF16)
            gkr = jnp.concatenate([attn_g_kr[j], jnp.zeros((V7X_LANES - rope - half,), F32),
                                   attn_g_kr[j][half:]]).reshape(1, -1)
            wq = attn_w_uq[j].reshape(q_lora, n_heads, nope + rope)
            wuq = jnp.concatenate([wq[:, :, nope:], wq[:, :, :nope], jnp.zeros((q_lora, n_heads, pad_l - half), F32),
                                   wq[:, :, nope + half:]],
                                  axis=2).reshape(q_lora, n_heads * V7X_LANES).astype(BF16)
            gq = jnp.concatenate([attn_g_qr[j], attn_g_qn[j], jnp.zeros((pad_l - half,), F32),
                                  attn_g_qr[j][half:]]).reshape(1, -1)
            wk = attn_w_uk[j]
            wuk = jnp.concatenate([jnp.zeros((kv_lora, n_heads, rope), F32), wk,
                                   jnp.zeros((kv_lora, n_heads, pad_l), F32)],
                                  axis=2).reshape(kv_lora, n_heads * V7X_LANES).astype(BF16)
            gk = jnp.concatenate([jnp.zeros((rope,), F32), attn_g_kn[j], jnp.zeros((pad_l,), F32)]).reshape(1, -1)
            wuv = jnp.concatenate([attn_w_uv[j], jnp.zeros((kv_lora, n_heads, V7X_LANES - vh), F32)],
                                  axis=2).reshape(kv_lora, n_heads * V7X_LANES).astype(BF16)
            vone = jnp.tile((jnp.arange(V7X_LANES) == vh).astype(F32), n_heads).reshape(1, -1)
            proj_w = (gmix, wd, row(attn_g_q_lora[j]), row(attn_g_kv_lora[j]), gkr, wuq, gq, wuk, gk, wuv, vone)

            q_p, c_p, krb_p, k_p, v_p = _mla_proj(xp, *proj_w, *tabs_p, bb=1, tt=tiles.prompt_rows, dims=dims,
                                                  with_kv=True)
            wk_perm = jnp.transpose(wk, (0, 2, 1)).reshape(kv_lora, nope * n_heads).astype(BF16)
            proj_s = proj_w[:7] + (wk_perm,) + proj_w[8:]
            q_s, c_s, krb_s, rs_s = _mla_proj(xs, *proj_s, *tabs_s, bb=tiles.decode_seqs, tt=ts, dims=dims,
                                              with_kv=False)
            amax = lambda g: jnp.max(jnp.abs(g))
            score_bound = 1.02 * scale * (nope * amax(attn_g_qn[j]) * amax(attn_g_kn[j])
                                          + rope * amax(attn_g_qr[j]) * amax(attn_g_kr[j]))
            bounded = (score_bound <= SCORE_EXP2_LIMIT).astype(jnp.int32).reshape(1)
            a_p = _flash(bounded, q_p, k_p, v_p, tq=tiles.attn_q_rows, n_heads=n_heads, vh=vh,
                         hps=tiles.attn_heads)

            wa = jnp.transpose(wk * attn_g_kn[j][None, None, :], (1, 2, 0))
            wa = jnp.concatenate([jnp.zeros((n_heads, rope, kv_lora), F32), wa,
                                  jnp.zeros((n_heads, pad_l, kv_lora), F32)], axis=1).astype(BF16)
            qa = _absorb(q_s.reshape(bs * ts, n_heads * V7X_LANES), wa, tr=tiles.decode_seqs * ts)
            qa = qa.reshape(bs, ts * n_heads, kv_lora)
            qr = q_s.reshape(bs, ts * n_heads, V7X_LANES)
            ctx = _decode_attn(page_table, cache_ckv, cache_krope, j, wk_perm, qa, qr, c_s, krb_s, rs_s,
                               pg=tiles.pages_per_group, dims=dims)
            wv = attn_w_uv[j]
            zer = jnp.zeros((kv_lora, n_heads // 2, vh), F32)
            wv2 = jnp.concatenate([jnp.concatenate([wv[:, 0::2], zer], axis=2),
                                   jnp.concatenate([zer, wv[:, 1::2]], axis=2)], axis=0)
            wv2 = jnp.transpose(wv2, (1, 0, 2)).astype(BF16)
            a_s = _unabsorb(ctx, wv2, bb=tiles.decode_seqs, n_q=ts)
            a_s = a_s.reshape(bs, ts, n_heads * vh)
            w_pre = attn_w_o[j].astype(BF16)
            ckv_p.append(c_p)
            kr_p.append(krb_p[:, :, :rope])
            ckv_s.append(c_s)
            kr_s.append(krb_s[:, :, :rope])
        else:
            n_blk, blk = rec_w_a.shape[1:3]
            assert n_blk % 2 == 0

            def pair(wm):
                z = jnp.zeros((n_blk // 2, blk, blk), F32)
                return jnp.concatenate([jnp.concatenate([wm[0::2], z], axis=2),
                                        jnp.concatenate([z, wm[1::2]], axis=2)], axis=1)

            w_gate = jnp.concatenate([pair(rec_w_a[j]), pair(rec_w_i[j])], axis=2).astype(BF16)
            rec_w = (gmix, rec_w_in[j].astype(BF16), rec_conv_w[j], row(rec_conv_b[j]), w_gate, row(rec_b_a[j]),
                     row(rec_b_i[j]), row(rec_lambda[j]))
            a_p, cv_p, h_p = _rec(xp, *rec_w, None, None, bb=1, tt=tiles.prompt_rows)
            h0 = jnp.broadcast_to(state_lru_h[j][:, None, :], (bs, ts, lru_w))
            a_s, cv_s, h_s = _rec(xs, *rec_w, _pad_state(state_lru_conv[j]), h0, bb=tiles.decode_seqs, tt=ts)
            w_pre = rec_w_out[j].astype(BF16)
            kw = rec_conv_w.shape[1] - 1
            lh_p.append(h_p[:, V7X_SUBLANES - 1])
            lh_s.append(h_s[:, V7X_SUBLANES - 1])
            lc_p.append(cv_p[:, V7X_SUBLANES - kw:])
            lc_s.append(cv_s[:, V7X_SUBLANES - kw:])
        ffn_w = (w_pre,) + ffn_stacked
        xp, fb_p = _ffn(xp, a_p, *ffn_w, None, layer=i, bb=1, tt=tiles.prompt_rows, fc=tiles.ffn_chunk)
        xs, fb_s = _ffn(xs, a_s, *ffn_w, _pad_state(state_ffn_conv[i]), layer=i, bb=tiles.decode_seqs, tt=ts,
                        fc=tiles.ffn_chunk)
        kf = ffn_conv_w.shape[1] - 1
        fc_p.append(fb_p[:, V7X_SUBLANES - kf:])
        fc_s.append(fb_s[:, V7X_SUBLANES - kf:])
    return (xp, xs, jnp.stack(ckv_p), jnp.stack(kr_p), jnp.stack(ckv_s), jnp.stack(kr_s),
            jnp.stack(lh_p), jnp.stack(lh_s), jnp.stack(lc_p), jnp.stack(lc_s), jnp.stack(fc_p), jnp.stack(fc_s))
```

```python
import functools
import math
from typing import NamedTuple

import jax
import jax.numpy as jnp
from jax import lax
from jax.experimental import pallas as pl
from jax.experimental.pallas import tpu as pltpu

EPS = 1e-6
ROPE_THETA = 10000.0
LRU_C = 8.0
SCORE_EXP2_LIMIT = 64.0

V7X_LANES = 128
V7X_SUBLANES = 8
V7X_VMEM_BYTES = 64 * 1024 * 1024
VMEM_LIMIT = V7X_VMEM_BYTES * 7 // 8

F32 = jnp.float32
BF16 = jnp.bfloat16


def _params(*sem):
    return pltpu.CompilerParams(dimension_semantics=sem, vmem_limit_bytes=VMEM_LIMIT)


def _const_spec(a):
    nd = a.ndim
    return pl.BlockSpec(a.shape, lambda *_: (0,) * nd)


def _rms(x, g):
    ms = jnp.mean(x * x, axis=-1, keepdims=True)
    return x * lax.rsqrt(ms + EPS) * g


def _sigmoid(x):
    return 1.0 / (1.0 + jnp.exp(-x))


def _rope(y, cos, sin, half):
    return y * cos + pltpu.roll(y, half, axis=1) * sin


def _tile_rows(tab_ref, bb):
    t = tab_ref[...]
    if bb == 1:
        return t
    return jnp.broadcast_to(t[None], (bb,) + t.shape).reshape(bb * t.shape[0], t.shape[1])


def _key_norm_factor(cb, wk_ref, n_heads, nope):
    k = jnp.dot(cb, wk_ref[...], preferred_element_type=F32)
    ksq = k * k
    part = ksq[:, :V7X_LANES]
    for i in range(1, wk_ref.shape[1] // V7X_LANES):
        part = part + ksq[:, V7X_LANES * i:V7X_LANES * (i + 1)]
    shift = V7X_LANES // 2
    while shift >= n_heads:
        part = part + pltpu.roll(part, shift, axis=1)
        shift //= 2
    return lax.rsqrt(part / nope + EPS)


def _mla_proj_kernel(x_ref, gmix_ref, wd_ref, gql_ref, gkv_ref, gkr_ref, wuq_ref, gq_ref, wuk_ref, gk_ref, wuv_ref,
                     vone_ref, cos_ref, sin_ref, *out_refs, n_heads, q_lora, kv_lora, nope, rope, scale, with_kv):
    if with_kv:
        q_ref, ckv_ref, kr_ref, k_ref, v_ref = out_refs
    else:
        q_ref, ckv_ref, kr_ref, rs_ref = out_refs
    bb, tt, d = x_ref.shape
    rows = bb * tt
    half = rope // 2
    xn = _rms(x_ref[...].reshape(rows, d), gmix_ref[...])
    dd = jnp.dot(xn.astype(BF16), wd_ref[...], preferred_element_type=F32)
    cq = _rms(dd[:, :q_lora], gql_ref[...])
    ckv = _rms(dd[:, q_lora:q_lora + kv_lora], gkv_ref[...])
    kr = dd[:, q_lora + kv_lora:]
    cos = _tile_rows(cos_ref, bb)
    sin = _tile_rows(sin_ref, bb)
    lane = lax.broadcasted_iota(jnp.int32, (1, V7X_LANES), 1)
    is_rope = lane < rope
    is_nope = jnp.logical_and(lane >= rope, lane < rope + nope)

    kr_ss = jnp.sum(jnp.where(is_rope, kr * kr, 0.0), axis=-1, keepdims=True)
    krr = _rope(kr * lax.rsqrt(kr_ss / rope + EPS) * gkr_ref[...], cos, sin, half)
    kr_ref[...] = krr.reshape(bb, tt, V7X_LANES)
    ckv_ref[...] = ckv.reshape(bb, tt, kv_lora)

    q = jnp.dot(cq.astype(BF16), wuq_ref[...], preferred_element_type=F32)
    ckv_b = ckv.astype(BF16)
    if with_kv:
        kk = jnp.dot(ckv_b, wuk_ref[...], preferred_element_type=F32)
        vv = jnp.dot(ckv_b, wuv_ref[...], preferred_element_type=F32) + vone_ref[...]
        v_ref[...] = vv.astype(BF16).reshape(v_ref.shape)
    else:
        rs_ref[...] = _key_norm_factor(ckv_b, wuk_ref, n_heads, nope).reshape(bb, tt, V7X_LANES)
    for h in range(n_heads):
        sl = slice(V7X_LANES * h, V7X_LANES * (h + 1))
        qb = q[:, sl]
        sq = qb * qb
        ss_r = jnp.sum(jnp.where(is_rope, sq, 0.0), axis=-1, keepdims=True)
        ss_n = jnp.sum(jnp.where(is_nope, sq, 0.0), axis=-1, keepdims=True)
        rs = jnp.where(is_nope, lax.rsqrt(ss_n / nope + EPS), lax.rsqrt(ss_r / rope + EPS))
        y = _rope(qb * rs * gq_ref[...], cos, sin, half)
        q_ref[:, :, sl] = (y * scale).astype(BF16).reshape(bb, tt, V7X_LANES)
        if with_kv:
            kb = kk[:, sl]
            ss_k = jnp.sum(kb * kb, axis=-1, keepdims=True)
            kn = kb * lax.rsqrt(ss_k / nope + EPS) * gk_ref[...]
            k_ref[:, :, sl] = (kn + krr).astype(BF16).reshape(bb, tt, V7X_LANES)


def _mla_proj(x, gmix, wd, gql, gkv, gkr, wuq, gq, wuk, gk, wuv, vone, cos, sin, *, bb, tt, dims, with_kv):
    n_heads, q_lora, kv_lora, nope, rope, vh, scale = dims
    b, t, d = x.shape
    hl = n_heads * V7X_LANES
    tok = lambda c: pl.BlockSpec((bb, tt, c), lambda i, j: (i, j, 0))
    tab = pl.BlockSpec((tt, V7X_LANES), lambda i, j: (j, 0))
    out_shape = [jax.ShapeDtypeStruct((b, t, hl), BF16), jax.ShapeDtypeStruct((b, t, kv_lora), F32),
                 jax.ShapeDtypeStruct((b, t, V7X_LANES), F32)]
    out_specs = [tok(hl), tok(kv_lora), tok(V7X_LANES)]
    if with_kv:
        out_shape += [jax.ShapeDtypeStruct((b, t, hl), BF16), jax.ShapeDtypeStruct((b, t, hl), BF16)]
        out_specs += [tok(hl), tok(hl)]
    else:
        out_shape.append(jax.ShapeDtypeStruct((b, t, V7X_LANES), F32))
        out_specs.append(tok(V7X_LANES))
    consts = (gmix, wd, gql, gkv, gkr, wuq, gq, wuk, gk, wuv, vone)
    return pl.pallas_call(
        functools.partial(_mla_proj_kernel, n_heads=n_heads, q_lora=q_lora, kv_lora=kv_lora, nope=nope, rope=rope,
                          scale=scale, with_kv=with_kv),
        grid=(b // bb, t // tt),
        in_specs=[tok(d)] + [_const_spec(a) for a in consts] + [tab, tab],
        out_specs=out_specs, out_shape=out_shape,
        compiler_params=_params("parallel", "parallel"), name="mla_proj",
    )(x, *consts, cos, sin)


def _attn_block(q, k_ref, v_ref, lanes, vh, past, tq, bounded):
    dn = (((1,), (1,)), ((), ()))
    row = lax.broadcasted_iota(jnp.int32, (tq, tq), 0)
    col = lax.broadcasted_iota(jnp.int32, (tq, tq), 1)
    s_d = lax.dot_general(q, k_ref[0, past:past + tq, lanes], dn, preferred_element_type=F32)
    s_d = jnp.where(col <= row, s_d, -jnp.inf)
    if past:
        s_p = lax.dot_general(q, k_ref[0, :past, lanes], dn, preferred_element_type=F32)
    if not bounded:
        m = jnp.max(s_d, axis=1, keepdims=True)
        if past:
            m = jnp.maximum(m, jnp.max(s_p, axis=1, keepdims=True))
            s_p = s_p - m
        s_d = s_d - m
    acc = jnp.dot(jnp.exp2(s_d).astype(BF16), v_ref[0, past:past + tq, lanes], preferred_element_type=F32)
    if past:
        acc = acc + jnp.dot(jnp.exp2(s_p).astype(BF16), v_ref[0, :past, lanes], preferred_element_type=F32)
    return acc[:, :vh] / acc[:, vh:vh + 1]


def _flash_kernel(bounded_ref, q_ref, k_ref, v_ref, o_ref, *, tq, vh, hps):
    is_bounded = bounded_ref[0] != 0
    for bounded in (True, False):
        @pl.when(is_bounded == bounded)
        def _(bounded=bounded):
            for n in range(q_ref.shape[1] // tq):
                rows = slice(n * tq, (n + 1) * tq)
                outs = []
                for hh in range(hps):
                    lanes = slice(V7X_LANES * hh, V7X_LANES * (hh + 1))
                    outs.append(_attn_block(q_ref[0, rows, lanes], k_ref, v_ref, lanes, vh, n * tq, tq, bounded))
                o_ref[0, rows, :] = jnp.concatenate(outs, axis=1).astype(BF16)


def _flash(bounded, q, k, v, *, tq, n_heads, vh, hps):
    b, t, _ = q.shape
    seq = lambda c: pl.BlockSpec((1, t, hps * c), lambda i, h, f: (i, 0, h))
    grid_spec = pltpu.PrefetchScalarGridSpec(
        num_scalar_prefetch=1, grid=(b, n_heads // hps),
        in_specs=[seq(V7X_LANES), seq(V7X_LANES), seq(V7X_LANES)], out_specs=seq(vh))
    return pl.pallas_call(
        functools.partial(_flash_kernel, tq=tq, vh=vh, hps=hps),
        grid_spec=grid_spec, out_shape=jax.ShapeDtypeStruct((b, t, n_heads * vh), BF16),
        compiler_params=_params("parallel", "parallel"), name="mla_prompt_attn",
    )(bounded, q, k, v)


def _absorb_kernel(q_ref, wa_ref, qa_ref, *, n_heads, kv_lora):
    for h in range(n_heads):
        qh = q_ref[:, V7X_LANES * h:V7X_LANES * (h + 1)]
        qa_ref[:, kv_lora * h:kv_lora * (h + 1)] = jnp.dot(qh, wa_ref[h], preferred_element_type=F32).astype(BF16)


def _absorb(q2d, wa, *, tr):
    r = q2d.shape[0]
    n_heads, _, kv_lora = wa.shape
    return pl.pallas_call(
        functools.partial(_absorb_kernel, n_heads=n_heads, kv_lora=kv_lora),
        grid=(r // tr,),
        in_specs=[pl.BlockSpec((tr, n_heads * V7X_LANES), lambda i: (i, 0)), _const_spec(wa)],
        out_specs=pl.BlockSpec((tr, n_heads * kv_lora), lambda i: (i, 0)),
        out_shape=jax.ShapeDtypeStruct((r, n_heads * kv_lora), BF16),
        compiler_params=_params("parallel"), name="mla_absorb_q",
    )(q2d, wa)


def _decode_attn_kernel(pt_ref, ckv_hbm, krt_hbm, wk_ref, perm_ref, qa_ref, qr_ref, cnew_ref, krnew_ref, rsnew_ref,
                        o_ref, cbuf, krbuf, cb_scr, s_scr, m_scr, l_scr, acc_scr, csem, ksem,
                        *, layer, pg, n_groups, n_heads, nope, rope):
    b = pl.program_id(0)
    page = ckv_hbm.shape[2]
    kv_lora = ckv_hbm.shape[3]
    dn_last = (((1,), (1,)), ((), ()))
    dn_first = (((0,), (0,)), ((), ()))
    qa = qa_ref[...]
    qr = qr_ref[:, :rope]

    def copies(seq, grp, slot):
        out = []
        for s in range(pg):
            pid = pt_ref[seq, grp * pg + s]
            rows = pl.ds(s * page, page)
            out.append(pltpu.make_async_copy(ckv_hbm.at[layer, pid], cbuf.at[slot, rows, :], csem.at[slot]))
            out.append(pltpu.make_async_copy(krt_hbm.at[layer, pid], krbuf.at[slot, :, rows], ksem.at[slot]))
        return out

    def fetch(seq, grp, slot):
        for n, cp in enumerate(copies(seq, grp, slot)):
            cp.start(priority=(n // 2) % 2)

    def wait(seq, grp, slot):
        for cp in copies(seq, grp, slot):
            cp.wait()

    @pl.when(b == 0)
    def _():
        fetch(0, 0, 0)

    def start_from_new_tokens():
        n_new = cnew_ref.shape[0]
        pad = 2 * V7X_SUBLANES - n_new
        c_new = jnp.concatenate([cnew_ref[...], jnp.zeros((pad, kv_lora), F32)], axis=0).astype(BF16)
        kr_new = jnp.concatenate([krnew_ref[:, :rope], jnp.zeros((pad, rope), F32)], axis=0).astype(BF16)
        rs_new = jnp.concatenate([rsnew_ref[...], jnp.ones((pad, V7X_LANES), F32)], axis=0)
        s = lax.dot_general(c_new, qa, dn_last, preferred_element_type=F32) * rs_new
        s = s + lax.dot_general(kr_new, qr, dn_last, preferred_element_type=F32)
        key = lax.broadcasted_iota(jnp.int32, s.shape, 0)
        qry = lax.broadcasted_iota(jnp.int32, s.shape, 1) // n_heads
        s = jnp.where(key <= qry, s, -jnp.inf)
        m0 = jnp.max(s, axis=0, keepdims=True)
        p = jnp.exp2(s - m0)
        m_scr[...] = m0
        l_scr[...] = jnp.sum(p, axis=0, keepdims=True)
        acc_scr[...] = lax.dot_general(c_new, p.astype(BF16), dn_first, preferred_element_type=F32)

    def stage_a(slot):
        cb = cbuf[slot].astype(BF16)
        cb_scr[slot] = cb
        rs = _key_norm_factor(cb, wk_ref, n_heads, nope)
        s = lax.dot_general(cb, qa, dn_last, preferred_element_type=F32) * rs
        krb = krbuf[slot].astype(BF16)
        s_scr[slot] = s + lax.dot_general(krb, qr, (((0,), (1,)), ((), ())), preferred_element_type=F32)

    def stage_b(slot):
        s = s_scr[slot]
        m_old = m_scr[...]
        m_new = jnp.maximum(m_old, jnp.max(s, axis=0, keepdims=True))
        p = jnp.exp2(s - m_new)
        alpha = jnp.exp2(m_old - m_new)
        l_scr[...] = alpha * l_scr[...] + jnp.sum(p, axis=0, keepdims=True)
        pv = lax.dot_general(cb_scr[slot], p.astype(BF16), dn_first, preferred_element_type=F32)
        acc_scr[...] = alpha * acc_scr[...] + pv
        m_scr[...] = m_new

    for i in range(n_groups + 1):
        slot = i % 2
        if i + 1 < n_groups:
            fetch(b, i + 1, 1 - slot)
        elif i + 1 == n_groups:
            @pl.when(b + 1 < pl.num_programs(0))
            def _():
                fetch(b + 1, 0, 0)
        if i < n_groups:
            wait(b, i, slot)
            if i == 0:
                start_from_new_tokens()
            stage_a(slot)
        if i >= 1:
            stage_b(1 - slot)
    ctx = acc_scr[...] / l_scr[...]
    o_ref[...] = jnp.dot(ctx, perm_ref[...], preferred_element_type=F32).T


def _decode_attn(page_table, cache_ckv, cache_krope, layer, wk_perm, qa, qr, c_new, kr_new, rs_new, *, pg, dims):
    n_heads, _, kv_lora, nope, rope, _, _ = dims
    b, n_pages = page_table.shape
    page = cache_ckv.shape[2]
    n_new = c_new.shape[1]
    rows = n_new * n_heads
    n_groups = n_pages // pg
    keys = pg * page
    assert rows == V7X_LANES and n_pages % pg == 0 and n_groups % 2 == 0 and n_new <= 2 * V7X_SUBLANES
    krope_t = jnp.swapaxes(cache_krope, 2, 3)
    lane = jnp.arange(rows)
    perm = (lane[:, None] == (lane[None, :] % n_new) * n_heads + lane[None, :] // n_new).astype(F32)
    per_seq = lambda r, c: pl.BlockSpec((None, r, c), lambda i, pt: (i, 0, 0))
    hbm = pl.BlockSpec(memory_space=pl.ANY)
    grid_spec = pltpu.PrefetchScalarGridSpec(
        num_scalar_prefetch=1, grid=(b,),
        in_specs=[hbm, hbm, pl.BlockSpec(wk_perm.shape, lambda i, pt: (0, 0)),
                  pl.BlockSpec(perm.shape, lambda i, pt: (0, 0)), per_seq(rows, kv_lora), per_seq(rows, V7X_LANES),
                  per_seq(n_new, kv_lora), per_seq(n_new, V7X_LANES), per_seq(n_new, V7X_LANES)],
        out_specs=per_seq(rows, kv_lora),
        scratch_shapes=[pltpu.VMEM((2, keys, kv_lora), F32), pltpu.VMEM((2, rope, keys), F32),
                        pltpu.VMEM((2, keys, kv_lora), BF16), pltpu.VMEM((2, keys, V7X_LANES), F32),
                        pltpu.VMEM((1, V7X_LANES), F32), pltpu.VMEM((1, V7X_LANES), F32),
                        pltpu.VMEM((kv_lora, V7X_LANES), F32),
                        pltpu.SemaphoreType.DMA((2,)), pltpu.SemaphoreType.DMA((2,))])
    return pl.pallas_call(
        functools.partial(_decode_attn_kernel, layer=layer, pg=pg, n_groups=n_groups, n_heads=n_heads, nope=nope,
                          rope=rope),
        grid_spec=grid_spec, out_shape=jax.ShapeDtypeStruct((b, rows, kv_lora), F32),
        compiler_params=_params("arbitrary"), name="mla_decode_attn",
    )(page_table, cache_ckv, krope_t, wk_perm, perm, qa, qr, c_new, kr_new, rs_new)


def _unabsorb_kernel(ctx_ref, wv_ref, o_ref, *, n_pairs, n_q):
    bb, _, kv_lora = ctx_ref.shape
    head = lambda h: ctx_ref[:, n_q * h:n_q * (h + 1), :].reshape(bb * n_q, kv_lora)
    for p in range(n_pairs):
        x = jnp.concatenate([head(2 * p), head(2 * p + 1)], axis=1).astype(BF16)
        o_ref[:, V7X_LANES * p:V7X_LANES * (p + 1)] = jnp.dot(x, wv_ref[p], preferred_element_type=F32).astype(BF16)


def _unabsorb(ctx, wv2, *, bb, n_q):
    b, rows, kv_lora = ctx.shape
    n_pairs = wv2.shape[0]
    return pl.pallas_call(
        functools.partial(_unabsorb_kernel, n_pairs=n_pairs, n_q=n_q),
        grid=(b // bb,),
        in_specs=[pl.BlockSpec((bb, rows, kv_lora), lambda i: (i, 0, 0)), _const_spec(wv2)],
        out_specs=pl.BlockSpec((bb * n_q, n_pairs * V7X_LANES), lambda i: (i, 0)),
        out_shape=jax.ShapeDtypeStruct((b * n_q, n_pairs * V7X_LANES), BF16),
        compiler_params=_params("parallel"), name="mla_unabsorb_v",
    )(ctx, wv2)


def _conv_prompt(g, w, b, buf_ref, carry):
    rows = g.shape[0]
    width = w.shape[0]
    buf_ref[0:V7X_SUBLANES, :] = carry
    buf_ref[V7X_SUBLANES:, :] = g
    y = b + g * w[width - 1:width, :]
    for s in range(1, width):
        y = y + buf_ref[V7X_SUBLANES - s:V7X_SUBLANES - s + rows, :] * w[width - 1 - s:width - s, :]
    return y


def _conv_decode(g, w, b, prev, t_idx):
    width = w.shape[0]
    rows = g.shape[0]
    y = b + g * w[width - 1:width, :]
    for s in range(1, width):
        sh = jnp.where(t_idx >= s, pltpu.roll(g, s, axis=0), pltpu.roll(prev, rows - V7X_SUBLANES + s, axis=0))
        y = y + sh * w[width - 1 - s:width - s, :]
    return y


def _ffn_kernel(*refs, decode, fc):
    if decode:
        x_ref, a_ref, wpre_ref, gn_ref, wup_ref, cw_ref, cb_ref, wd_ref, st_ref, y_ref, so_ref = refs
    else:
        (x_ref, a_ref, wpre_ref, gn_ref, wup_ref, cw_ref, cb_ref, wd_ref,
         y_ref, so_ref, carry_scr, buf_scr) = refs
    bb, tt, d = x_ref.shape
    rows = bb * tt
    dff = wd_ref.shape[0]
    a = a_ref[...].reshape(rows, a_ref.shape[2])
    x1 = x_ref[...].reshape(rows, d) + jnp.dot(a, wpre_ref[...], preferred_element_type=F32)
    xn = _rms(x1, gn_ref[...]).astype(BF16)
    if decode:
        t_idx = lax.broadcasted_iota(jnp.int32, (rows, fc), 0) % tt
    else:
        @pl.when(pl.program_id(1) == 0)
        def _():
            carry_scr[...] = jnp.zeros(carry_scr.shape, F32)

    acc = x1
    for c in range(dff // fc):
        cols = slice(c * fc, (c + 1) * fc)
        g = jnp.dot(xn, wup_ref[:, cols], preferred_element_type=F32)
        u = jnp.dot(xn, wup_ref[:, dff + c * fc:dff + (c + 1) * fc], preferred_element_type=F32)
        if decode:
            gc = _conv_decode(g, cw_ref[:, cols], cb_ref[:, cols], st_ref[:, :, cols].reshape(rows, fc), t_idx)
            so_ref[:, :, cols] = g.reshape(bb, tt, fc)
        else:
            gc = _conv_prompt(g, cw_ref[:, cols], cb_ref[:, cols], buf_scr.at[c], carry_scr[:, cols])
            last = g[rows - V7X_SUBLANES:, :]
            carry_scr[:, cols] = last
            so_ref[0, :, cols] = last
        h = (gc * _sigmoid(gc) * u).astype(BF16)
        acc = acc + jnp.dot(h, wd_ref[cols, :], preferred_element_type=F32)
    y_ref[...] = acc.reshape(bb, tt, d)


def _resident_spec(a, layer=None):
    if layer is None:
        nd = a.ndim
        return pl.BlockSpec(a.shape, lambda *_: (0,) * nd, pipeline_mode=pl.Buffered(1))
    nd = a.ndim - 1
    return pl.BlockSpec((None,) + a.shape[1:], lambda *_: (layer,) + (0,) * nd, pipeline_mode=pl.Buffered(1))


def _ffn(x, a, w_pre, gn, wup, cw, cb, wd, state, *, layer, bb, tt, fc):
    b, t, d = x.shape
    da = a.shape[2]
    dff = wd.shape[1]
    rows = bb * tt
    decode = state is not None
    tok = lambda c: pl.BlockSpec((bb, tt, c), lambda i, j: (i, j, 0))
    weights = [w_pre, gn, wup, cw, cb, wd]
    in_specs = [tok(d), tok(da), _resident_spec(w_pre)] + [_resident_spec(w, layer) for w in weights[1:]]
    args = [x, a] + weights
    scratch = []
    if decode:
        in_specs.append(pl.BlockSpec((bb, tt, dff), lambda i, j: (i, 0, 0)))
        args.append(state)
    else:
        assert bb == 1
        scratch = [pltpu.VMEM((V7X_SUBLANES, dff), F32), pltpu.VMEM((dff // fc, V7X_SUBLANES + rows, fc), F32)]
    so_spec = pl.BlockSpec((bb, None, V7X_SUBLANES, dff), lambda i, j: (i, j, 0, 0))
    y, so = pl.pallas_call(
        functools.partial(_ffn_kernel, decode=decode, fc=fc),
        grid=(b // bb, t // tt), in_specs=in_specs,
        out_specs=[tok(d), so_spec],
        out_shape=[jax.ShapeDtypeStruct((b, t, d), F32),
                   jax.ShapeDtypeStruct((b, t // tt, V7X_SUBLANES, dff), F32)],
        scratch_shapes=scratch,
        compiler_params=_params("parallel", "arbitrary"), name="conv_ffn",
    )(*args)
    return y, so[:, -1]


def _rec_kernel(*refs, decode):
    if decode:
        (x_ref, gn_ref, win_ref, cw_ref, cb_ref, wgate_ref, ba_ref, bi_ref, lam_ref, st_ref, h0_ref,
         a_ref, co_ref, ho_ref, a_scr, b_scr) = refs
    else:
        (x_ref, gn_ref, win_ref, cw_ref, cb_ref, wgate_ref, ba_ref, bi_ref, lam_ref,
         a_ref, co_ref, ho_ref, a_scr, b_scr, h_scr, carry_scr, hc_scr, buf_scr) = refs
    bb, tt, d = x_ref.shape
    rows = bb * tt
    w = cw_ref.shape[1]
    t = pl.program_id(1)
    xn = _rms(x_ref[...].reshape(rows, d), gn_ref[...]).astype(BF16)
    z = jnp.dot(xn, win_ref[...], preferred_element_type=F32)
    xr = z[:, :w]
    gin = z[:, w:]
    gate = 0.5 * gin * (1.0 + jnp.tanh(0.7978845608028654 * (gin + 0.044715 * (gin * gin * gin))))
    t_idx = lax.broadcasted_iota(jnp.int32, (rows, w), 0) % V7X_SUBLANES
    if decode:
        xc = _conv_decode(xr, cw_ref[...], cb_ref[...], st_ref[...].reshape(rows, w), t_idx)
        co_ref[...] = xr.reshape(bb, tt, w)
    else:
        @pl.when(t == 0)
        def _():
            carry_scr[...] = jnp.zeros(carry_scr.shape, F32)
            hc_scr[...] = jnp.zeros(hc_scr.shape, F32)

        xc = _conv_prompt(xr, cw_ref[...], cb_ref[...], buf_scr, carry_scr[...])
        last = xr[rows - V7X_SUBLANES:, :]
        carry_scr[...] = last
        co_ref[0] = last

    n_blk = wgate_ref.shape[0]
    kb = wgate_ref.shape[1]
    xcb = xc.astype(BF16)
    ra, ri = [], []
    for p in range(n_blk):
        zz = jnp.dot(xcb[:, kb * p:kb * (p + 1)], wgate_ref[p], preferred_element_type=F32)
        ra.append(zz[:, :kb])
        ri.append(zz[:, kb:])
    r = _sigmoid(jnp.concatenate(ra, axis=1) + ba_ref[...])
    gi = _sigmoid(jnp.concatenate(ri, axis=1) + bi_ref[...])
    neg_lam = -lam_ref[...]
    softplus = jnp.maximum(neg_lam, 0.0) + jnp.log(1.0 + jnp.exp(-jnp.abs(neg_lam)))
    av = jnp.exp(r * (-LRU_C * softplus))
    bv = jnp.sqrt(1.0 - av * av) * (gi * xc)

    grp3 = (rows // V7X_SUBLANES, V7X_SUBLANES, w)
    av, bv = av.reshape(grp3), bv.reshape(grp3)
    t3 = lax.broadcasted_iota(jnp.int32, grp3, 1)
    for s in (1, 2, 4):
        keep = t3 >= s
        bv = jnp.where(keep, av * pltpu.roll(bv, s, axis=1) + bv, bv)
        av = jnp.where(keep, av * pltpu.roll(av, s, axis=1), av)
    av, bv = av.reshape(rows, w), bv.reshape(rows, w)
    if decode:
        hs = av * h0_ref[...].reshape(rows, w) + bv
        ho_ref[...] = hs.reshape(bb, tt, w)
    else:
        a_scr[...] = av
        b_scr[...] = bv

        def group(i, h_in):
            sl = pl.ds(pl.multiple_of(i * V7X_SUBLANES, V7X_SUBLANES), V7X_SUBLANES)
            hg = a_scr[sl, :] * h_in + b_scr[sl, :]
            h_scr[sl, :] = hg
            return jnp.broadcast_to(hg[V7X_SUBLANES - 1:, :], (V7X_SUBLANES, w))

        h_last = lax.fori_loop(0, rows // V7X_SUBLANES, group, hc_scr[...])
        hc_scr[...] = h_last
        ho_ref[0] = h_last
        hs = h_scr[...]
    a_ref[...] = (hs * gate).astype(BF16).reshape(bb, tt, w)


def _rec(x, gn, w_in, cw, cb, w_gate, b_a, b_i, lam, state, h0, *, bb, tt):
    b, t, d = x.shape
    w = cw.shape[1]
    rows = bb * tt
    decode = state is not None
    tok = lambda c: pl.BlockSpec((bb, tt, c), lambda i, j: (i, j, 0))
    consts = (gn, w_in, cw, cb, w_gate, b_a, b_i, lam)
    in_specs = [tok(d)] + [_const_spec(a) for a in consts]
    args = [x, *consts]
    scratch = [pltpu.VMEM((rows, w), F32), pltpu.VMEM((rows, w), F32)]
    grp = pl.BlockSpec((bb, V7X_SUBLANES, w), lambda i, j: (i, 0, 0))
    if decode:
        in_specs += [grp, grp]
        args += [state, h0]
    else:
        assert bb == 1
        scratch += [pltpu.VMEM((rows, w), F32), pltpu.VMEM((V7X_SUBLANES, w), F32), pltpu.VMEM((V7X_SUBLANES, w), F32),
                    pltpu.VMEM((V7X_SUBLANES + rows, w), F32)]
    return pl.pallas_call(
        functools.partial(_rec_kernel, decode=decode),
        grid=(b // bb, t // tt), in_specs=in_specs,
        out_specs=[tok(w), grp, grp],
        out_shape=[jax.ShapeDtypeStruct((b, t, w), BF16), jax.ShapeDtypeStruct((b, V7X_SUBLANES, w), F32),
                   jax.ShapeDtypeStruct((b, V7X_SUBLANES, w), F32)],
        scratch_shapes=scratch,
        compiler_params=_params("parallel", "arbitrary"), name="rglru_mixer",
    )(*args)


def _pad_state(buf):
    return jnp.pad(buf, ((0, 0), (V7X_SUBLANES - buf.shape[1], 0), (0, 0)))


class _Tiles(NamedTuple):
    prompt_rows: int
    decode_seqs: int
    attn_q_rows: int
    attn_heads: int
    ffn_chunk: int
    pages_per_group: int


def _tiles(tp, bs, ts, n_heads, vh, dff, n_pages):
    heads = 4 if n_heads % 4 == 0 and (4 * vh) % V7X_LANES == 0 else 2
    return _Tiles(prompt_rows=min(tp, 512), decode_seqs=min(bs, 256 // ts), attn_q_rows=min(tp, 256),
                  attn_heads=heads, ffn_chunk=min(dff, 1536), pages_per_group=min(n_pages, 32))


def kernel(x_prompt, x_sample, cache_ckv, cache_krope, page_table, state_lru_h, state_lru_conv, state_ffn_conv,
           norm_mix, norm_ffn, attn_w_down, attn_g_q_lora, attn_w_uq, attn_g_kv_lora, attn_g_qn, attn_g_qr,
           attn_g_kn, attn_g_kr, attn_w_uk, attn_w_uv, attn_w_o, rec_w_in, rec_conv_w, rec_conv_b, rec_w_a,
           rec_b_a, rec_w_i, rec_b_i, rec_lambda, rec_w_out, ffn_w_up, ffn_conv_w, ffn_conv_b, ffn_w_down):
    bp, tp, d = x_prompt.shape
    bs, ts, _ = x_sample.shape
    depth = norm_mix.shape[0]
    n_mixers = 2
    kv_lora, n_heads, nope = attn_w_uk.shape[1:]
    vh = attn_w_uv.shape[3]
    rope = attn_g_qr.shape[1]
    q_lora = attn_g_q_lora.shape[1]
    half = rope // 2
    page = cache_ckv.shape[2]
    past = page_table.shape[1] * page
    dff = ffn_conv_w.shape[2]
    lru_w = rec_conv_w.shape[2]
    assert ts == V7X_SUBLANES and rope + nope + half <= V7X_LANES and vh < V7X_LANES and n_heads % 2 == 0
    scale = float((nope + rope) ** -0.5) * math.log2(math.e)
    dims = (n_heads, q_lora, kv_lora, nope, rope, vh, scale)
    pad_l = V7X_LANES - rope - nope
    row = lambda v: v.reshape(1, -1).astype(F32)

    def rope_tables(pos):
        inv = ROPE_THETA ** (-jnp.arange(0, rope, 2, dtype=F32) / rope)
        ang = pos.astype(F32)[:, None] * inv[None, :]
        cos, sin = jnp.cos(ang), jnp.sin(ang)
        n = pos.shape[0]
        ones = jnp.ones((n, V7X_LANES - rope - half), F32)
        return (jnp.concatenate([cos, cos, ones, jnp.zeros((n, half), F32)], axis=1),
                jnp.concatenate([-sin, sin, jnp.zeros((n, V7X_LANES - rope), F32)], axis=1))

    tabs_p = rope_tables(jnp.arange(tp))
    tabs_s = rope_tables(past + jnp.arange(ts))

    xp, xs = x_prompt, x_sample
    a_p = a_s = w_pre = None
    ckv_p, kr_p, ckv_s, kr_s = [], [], [], []
    lh_p, lh_s, lc_p, lc_s = [], [], [], []
    fc_p, fc_s = [], []
    tiles = _tiles(tp, bs, ts, n_heads, vh, dff, page_table.shape[1])
    ffn_stacked = (norm_ffn[:, None, :], ffn_w_up.astype(BF16), ffn_conv_w, ffn_conv_b[:, None, :],
                   ffn_w_down.astype(BF16))
    for i in range(depth):
        j = i // n_mixers
        gmix = row(norm_mix[i])
        if i % n_mixers == 0:
            wdn = attn_w_down[j]
            wd = jnp.concatenate([wdn, jnp.zeros((d, V7X_LANES - rope - half), F32), wdn[:, -half:]],
                                 axis=1).astype(BF16)
            gkr = jnp.concatenate([attn_g_kr[j], jnp.zeros((V7X_LANES - rope - half,), F32),
                                   attn_g_kr[j][half:]]).reshape(1, -1)
            wq = attn_w_uq[j].reshape(q_lora, n_heads, nope + rope)
            wuq = jnp.concatenate([wq[:, :, nope:], wq[:, :, :nope], jnp.zeros((q_lora, n_heads, pad_l - half), F32),
                                   wq[:, :, nope + half:]],
                                  axis=2).reshape(q_lora, n_heads * V7X_LANES).astype(BF16)
            gq = jnp.concatenate([attn_g_qr[j], attn_g_qn[j], jnp.zeros((pad_l - half,), F32),
                                  attn_g_qr[j][half:]]).reshape(1, -1)
            wk = attn_w_uk[j]
            wuk = jnp.concatenate([jnp.zeros((kv_lora, n_heads, rope), F32), wk,
                                   jnp.zeros((kv_lora, n_heads, pad_l), F32)],
                                  axis=2).reshape(kv_lora, n_heads * V7X_LANES).astype(BF16)
            gk = jnp.concatenate([jnp.zeros((rope,), F32), attn_g_kn[j], jnp.zeros((pad_l,), F32)]).reshape(1, -1)
            wuv = jnp.concatenate([attn_w_uv[j], jnp.zeros((kv_lora, n_heads, V7X_LANES - vh), F32)],
                                  axis=2).reshape(kv_lora, n_heads * V7X_LANES).astype(BF16)
            vone = jnp.tile((jnp.arange(V7X_LANES) == vh).astype(F32), n_heads).reshape(1, -1)
            proj_w = (gmix, wd, row(attn_g_q_lora[j]), row(attn_g_kv_lora[j]), gkr, wuq, gq, wuk, gk, wuv, vone)

            q_p, c_p, krb_p, k_p, v_p = _mla_proj(xp, *proj_w, *tabs_p, bb=1, tt=tiles.prompt_rows, dims=dims,
                                                  with_kv=True)
            wk_perm = jnp.transpose(wk, (0, 2, 1)).reshape(kv_lora, nope * n_heads).astype(BF16)
            proj_s = proj_w[:7] + (wk_perm,) + proj_w[8:]
            q_s, c_s, krb_s, rs_s = _mla_proj(xs, *proj_s, *tabs_s, bb=tiles.decode_seqs, tt=ts, dims=dims,
                                              with_kv=False)
            amax = lambda g: jnp.max(jnp.abs(g))
            score_bound = 1.02 * scale * (nope * amax(attn_g_qn[j]) * amax(attn_g_kn[j])
                                          + rope * amax(attn_g_qr[j]) * amax(attn_g_kr[j]))
            bounded = (score_bound <= SCORE_EXP2_LIMIT).astype(jnp.int32).reshape(1)
            a_p = _flash(bounded, q_p, k_p, v_p, tq=tiles.attn_q_rows, n_heads=n_heads, vh=vh,
                         hps=tiles.attn_heads)

            wa = jnp.transpose(wk * attn_g_kn[j][None, None, :], (1, 2, 0))
            wa = jnp.concatenate([jnp.zeros((n_heads, rope, kv_lora), F32), wa,
                                  jnp.zeros((n_heads, pad_l, kv_lora), F32)], axis=1).astype(BF16)
            qa = _absorb(q_s.reshape(bs * ts, n_heads * V7X_LANES), wa, tr=tiles.decode_seqs * ts)
            qa = qa.reshape(bs, ts * n_heads, kv_lora)
            qr = q_s.reshape(bs, ts * n_heads, V7X_LANES)
            ctx = _decode_attn(page_table, cache_ckv, cache_krope, j, wk_perm, qa, qr, c_s, krb_s, rs_s,
                               pg=tiles.pages_per_group, dims=dims)
            wv = attn_w_uv[j]
            zer = jnp.zeros((kv_lora, n_heads // 2, vh), F32)
            wv2 = jnp.concatenate([jnp.concatenate([wv[:, 0::2], zer], axis=2),
                                   jnp.concatenate([zer, wv[:, 1::2]], axis=2)], axis=0)
            wv2 = jnp.transpose(wv2, (1, 0, 2)).astype(BF16)
            a_s = _unabsorb(ctx, wv2, bb=tiles.decode_seqs, n_q=ts)
            a_s = a_s.reshape(bs, ts, n_heads * vh)
            w_pre = attn_w_o[j].astype(BF16)
            ckv_p.append(c_p)
            kr_p.append(krb_p[:, :, :rope])
            ckv_s.append(c_s)
            kr_s.append(krb_s[:, :, :rope])
        else:
            n_blk, blk = rec_w_a.shape[1:3]
            assert n_blk % 2 == 0

            def pair(wm):
                z = jnp.zeros((n_blk // 2, blk, blk), F32)
                return jnp.concatenate([jnp.concatenate([wm[0::2], z], axis=2),
                                        jnp.concatenate([z, wm[1::2]], axis=2)], axis=1)

            w_gate = jnp.concatenate([pair(rec_w_a[j]), pair(rec_w_i[j])], axis=2).astype(BF16)
            rec_w = (gmix, rec_w_in[j].astype(BF16), rec_conv_w[j], row(rec_conv_b[j]), w_gate, row(rec_b_a[j]),
                     row(rec_b_i[j]), row(rec_lambda[j]))
            a_p, cv_p, h_p = _rec(xp, *rec_w, None, None, bb=1, tt=tiles.prompt_rows)
            h0 = jnp.broadcast_to(state_lru_h[j][:, None, :], (bs, ts, lru_w))
            a_s, cv_s, h_s = _rec(xs, *rec_w, _pad_state(state_lru_conv[j]), h0, bb=tiles.decode_seqs, tt=ts)
            w_pre = rec_w_out[j].astype(BF16)
            kw = rec_conv_w.shape[1] - 1
            lh_p.append(h_p[:, V7X_SUBLANES - 1])
            lh_s.append(h_s[:, V7X_SUBLANES - 1])
            lc_p.append(cv_p[:, V7X_SUBLANES - kw:])
            lc_s.append(cv_s[:, V7X_SUBLANES - kw:])
        ffn_w = (w_pre,) + ffn_stacked
        xp, fb_p = _ffn(xp, a_p, *ffn_w, None, layer=i, bb=1, tt=tiles.prompt_rows, fc=tiles.ffn_chunk)
        xs, fb_s = _ffn(xs, a_s, *ffn_w, _pad_state(state_ffn_conv[i]), layer=i, bb=tiles.decode_seqs, tt=ts,
                        fc=tiles.ffn_chunk)
        kf = ffn_conv_w.shape[1] - 1
        fc_p.append(fb_p[:, V7X_SUBLANES - kf:])
        fc_s.append(fb_s[:, V7X_SUBLANES - kf:])
    return (xp, xs, jnp.stack(ckv_p), jnp.stack(kr_p), jnp.stack(ckv_s), jnp.stack(kr_s),
            jnp.stack(lh_p), jnp.stack(lh_s), jnp.stack(lc_p), jnp.stack(lc_s), jnp.stack(fc_p), jnp.stack(fc_s))
```
